```python
import math
import jax, jax.numpy as jnp
from jax import lax
import numpy as np


D_MODEL = 2048
BATCH = 4
SEQ = 4096
DEPTH = 2

PLE_DIM = 256
N_BRANCH = 3
BRANCH_WIDTH = D_MODEL // 2
A_HEADS = 8
A_HEAD_DIM = BRANCH_WIDTH // A_HEADS
MOBA_BLOCK = 256
MOBA_TOPK = 3
MOBA_CHUNK = 32
B_Q_HEADS = 16
B_KV_HEADS = 4
B_HEAD_DIM = BRANCH_WIDTH // B_Q_HEADS
B_KV_WIDTH = B_KV_HEADS * B_HEAD_DIM
WINDOW = 128
C_BLOCKS = 8
C_BLOCK_DIM = BRANCH_WIDTH // C_BLOCKS
CONV_WIDTH = 4
LRU_C = 8.0
RPE_BUCKETS = 32
RPE_MAX_DIST = 128
RPE_HEADS = A_HEADS + B_Q_HEADS
EPS = 1e-6
NEG = -1e30
IN_SPLIT_SIZES = (BRANCH_WIDTH, BRANCH_WIDTH, BRANCH_WIDTH, BRANCH_WIDTH,
                  BRANCH_WIDTH, B_KV_WIDTH, B_KV_WIDTH, BRANCH_WIDTH,
                  BRANCH_WIDTH, BRANCH_WIDTH,
                  N_BRANCH * D_MODEL)
IN_WIDTH = sum(IN_SPLIT_SIZES)

kernel_name = 'hybrid_moba_swa_rglru_gated_merge'


def rmsnorm(x, g):
    xf = x.astype(jnp.float32)
    y = xf * lax.rsqrt(jnp.mean(xf * xf, axis=-1, keepdims=True) + EPS)
    return (y * g.astype(jnp.float32)).astype(x.dtype)


def t5_bucket(dist):
    n = jnp.maximum(dist, 0)
    max_exact = RPE_BUCKETS // 2
    large = max_exact + (jnp.log(jnp.maximum(n, 1).astype(jnp.float32) / max_exact)
                         / math.log(RPE_MAX_DIST / max_exact)
                         * (RPE_BUCKETS - max_exact)).astype(jnp.int32)
    large = jnp.minimum(large, RPE_BUCKETS - 1)
    return jnp.where(n < max_exact, n, large)


def moba_attention(q, k, v, table):
    bsz, seq, nh, dh = q.shape
    L = MOBA_BLOCK
    nb = -(-seq // L)
    pad = nb * L - seq
    scale = dh ** -0.5
    qh = q.transpose(0, 2, 1, 3)
    kb = jnp.pad(k, ((0, 0), (0, pad), (0, 0), (0, 0))).reshape(bsz, nb, L, nh, dh).transpose(0, 3, 1, 2, 4)
    vb = jnp.pad(v, ((0, 0), (0, pad), (0, 0), (0, 0))).reshape(bsz, nb, L, nh, dh).transpose(0, 3, 1, 2, 4)
    qblk = jnp.arange(seq) // L
    n_sel = min(MOBA_TOPK, nb - 1)
    if n_sel > 0:
        kmean = jnp.mean(kb.astype(jnp.float32), axis=3)
        gate = jnp.einsum('bhsd,bhnd->bhsn', qh.astype(jnp.float32), kmean)
        past = jnp.arange(nb)[None, :] < qblk[:, None]
        gate = jnp.where(past, gate, NEG)
        _, sel = lax.top_k(gate, n_sel)
    bi = jnp.arange(bsz)[:, None, None, None]
    hi = jnp.arange(nh)[None, :, None, None]

    def chunk(c):
        start = c * MOBA_CHUNK
        qc = lax.dynamic_slice_in_dim(qh, start, MOBA_CHUNK, axis=2)
        qpos = start + jnp.arange(MOBA_CHUNK)
        j = start // L
        k_own = lax.dynamic_index_in_dim(kb, j, axis=2, keepdims=False)
        v_own = lax.dynamic_index_in_dim(vb, j, axis=2, keepdims=False)
        dist = qpos[:, None] - (j * L + jnp.arange(L))[None, :]
        s_own = jnp.einsum('bhcd,bhld->bhcl', qc, k_own, preferred_element_type=jnp.float32) * scale
        s_own = s_own + table[:, t5_bucket(dist)].astype(jnp.float32)
        s_own = jnp.where(dist >= 0, s_own, NEG)
        if n_sel == 0:
            p_own = jax.nn.softmax(s_own, axis=-1).astype(v.dtype)
            return jnp.einsum('bhcl,bhld->bhcd', p_own, v_own)
        idx = lax.dynamic_slice_in_dim(sel, start, MOBA_CHUNK, axis=2)
        k_g = kb[bi, hi, idx]
        v_g = vb[bi, hi, idx]
        dist_g = qpos[None, None, :, None, None] - (idx[..., None] * L + jnp.arange(L))
        s_g = jnp.einsum('bhcd,bhcnld->bhcnl', qc, k_g, preferred_element_type=jnp.float32) * scale
        s_g = s_g + table[hi[..., None], t5_bucket(dist_g)].astype(jnp.float32)
        s_g = jnp.where((idx < j)[..., None], s_g, NEG)
        logits = jnp.concatenate([s_g.reshape(bsz, nh, MOBA_CHUNK, n_sel * L), s_own], axis=-1)
        p = jax.nn.softmax(logits, axis=-1).astype(v.dtype)
        p_g = p[..., :n_sel * L].reshape(bsz, nh, MOBA_CHUNK, n_sel, L)
        p_own = p[..., n_sel * L:]
        return (jnp.einsum('bhcnl,bhcnld->bhcd', p_g, v_g)
                + jnp.einsum('bhcl,bhld->bhcd', p_own, v_own))

    out = lax.map(chunk, jnp.arange(seq // MOBA_CHUNK))
    return out.transpose(1, 0, 3, 2, 4).reshape(bsz, seq, nh * dh)


def swa_attention(q, k, v, sinks, table):
    bsz, seq, hq, dh = q.shape
    hkv = k.shape[2]
    grp = hq // hkv
    W = WINDOW
    nb = seq // W
    scale = dh ** -0.5
    qb = q.reshape(bsz, nb, W, hkv, grp, dh)
    kb = k.reshape(bsz, nb, W, hkv, dh)
    vb = v.reshape(bsz, nb, W, hkv, dh)
    kband = jnp.concatenate([jnp.pad(kb, ((0, 0), (1, 0), (0, 0), (0, 0), (0, 0)))[:, :-1], kb], axis=2)
    vband = jnp.concatenate([jnp.pad(vb, ((0, 0), (1, 0), (0, 0), (0, 0), (0, 0)))[:, :-1], vb], axis=2)
    s = jnp.einsum('bnqhgd,bnlhd->bhgnql', qb, kband, preferred_element_type=jnp.float32) * scale
    koff = jnp.arange(2 * W) - W
    dist = jnp.arange(W)[:, None] - koff[None, :]
    bias = table[t5_bucket(dist)].transpose(2, 0, 1).reshape(hkv, grp, 1, W, 2 * W)
    s = s + bias.astype(jnp.float32)
    kpos = jnp.arange(nb)[:, None, None] * W + koff[None, None, :]
    mask = (dist >= 0)[None] & (dist < W)[None] & (kpos >= 0)
    s = jnp.where(mask, s, NEG)
    sink_col = jnp.broadcast_to(sinks.astype(jnp.float32).reshape(1, hkv, grp, 1, 1, 1), s.shape[:-1] + (1,))
    p = jax.nn.softmax(jnp.concatenate([s, sink_col], axis=-1), axis=-1)[..., :-1].astype(v.dtype)
    out = jnp.einsum('bhgnql,bnlhd->bnqhgd', p, vband)
    return out.reshape(bsz, seq, hq * dh)


def rglru_branch(xc, conv_w, conv_b, w_r, b_r, w_i, b_i, lam):
    bsz, seq, ch = xc.shape
    conv = lax.conv_general_dilated(xc, conv_w.reshape(CONV_WIDTH, 1, ch), window_strides=(1,),
                                    padding=[(CONV_WIDTH - 1, 0)], dimension_numbers=('NWC', 'WIO', 'NWC'),
                                    feature_group_count=ch) + conv_b
    xb = conv.reshape(bsz, seq, C_BLOCKS, C_BLOCK_DIM)
    r = jax.nn.sigmoid((jnp.einsum('bsnc,ncd->bsnd', xb, w_r).reshape(bsz, seq, ch) + b_r).astype(jnp.float32))
    i = jax.nn.sigmoid((jnp.einsum('bsnc,ncd->bsnd', xb, w_i).reshape(bsz, seq, ch) + b_i).astype(jnp.float32))
    log_a = -LRU_C * r * jax.nn.softplus(-lam.astype(jnp.float32))
    a = jnp.exp(log_a)
    b = jnp.sqrt(jnp.maximum(-jnp.expm1(2.0 * log_a), 0.0)) * (i * conv.astype(jnp.float32))

    def combine(left, right):
        a1, b1 = left
        a2, b2 = right
        return a1 * a2, a2 * b1 + b2

    _, h = lax.associative_scan(combine, (a, b), axis=1)
    return h.astype(xc.dtype)


def mixer_layer(x, p_i, rpe_table, norm_g, w_in, sinks, conv_w, conv_b, w_r, b_r, w_i, b_i, lam,
                w_br, w_out, ple_norm_g, w_pg, w_pp):
    bsz, seq, _ = x.shape
    h = rmsnorm(x, norm_g)
    proj = h @ w_in
    cuts = [int(c) for c in np.cumsum(IN_SPLIT_SIZES)[:-1]]
    qa, ka, va, ga, qb, kb, vb, gb, xc, gc, mg = jnp.split(proj, cuts, axis=-1)
    ya = moba_attention(qa.reshape(bsz, seq, A_HEADS, A_HEAD_DIM), ka.reshape(bsz, seq, A_HEADS, A_HEAD_DIM),
                        va.reshape(bsz, seq, A_HEADS, A_HEAD_DIM), rpe_table[:, :A_HEADS].T)
    yb = swa_attention(qb.reshape(bsz, seq, B_Q_HEADS, B_HEAD_DIM), kb.reshape(bsz, seq, B_KV_HEADS, B_HEAD_DIM),
                       vb.reshape(bsz, seq, B_KV_HEADS, B_HEAD_DIM), sinks, rpe_table[:, A_HEADS:])
    yc = rglru_branch(xc, conv_w, conv_b, w_r, b_r, w_i, b_i, lam)
    y = jnp.stack([ya * jax.nn.silu(ga), yb * jax.nn.silu(gb), yc * jax.nn.silu(gc)], axis=2)
    y_d = jnp.einsum('bsnc,ncd->bsnd', y, w_br)
    gates = jax.nn.sigmoid(mg.reshape(bsz, seq, N_BRANCH, D_MODEL))
    merged = jnp.einsum('bsnd,bsnd->bsd', gates, y_d)
    x = x + merged @ w_out
    x = x + jax.nn.sigmoid(rmsnorm(x, ple_norm_g) @ w_pg) * (p_i @ w_pp)
    return x


def setup_inputs(seed: int = 0) -> dict:
    key = jax.random.key(seed)
    ks = jax.random.split(key, 20)
    f32 = jnp.float32
    nrm = lambda k, shape, s: jax.random.normal(k, shape, f32) * s
    a8 = jax.random.uniform(ks[12], (DEPTH, BRANCH_WIDTH), f32, 0.9, 0.999)
    a = a8 ** (1.0 / LRU_C)
    lam = jnp.log(a) - jnp.log1p(-a)
    return {
        'x': nrm(ks[0], (BATCH, SEQ, D_MODEL), 1.0),
        'p': nrm(ks[1], (DEPTH, BATCH, SEQ, PLE_DIM), 1.0),
        'rpe_table': nrm(ks[2], (RPE_BUCKETS, RPE_HEADS), 0.1),
        'norm_g': 1.0 + nrm(ks[3], (DEPTH, D_MODEL), 0.02),
        'w_in': nrm(ks[4], (DEPTH, D_MODEL, IN_WIDTH), D_MODEL ** -0.5),
        'sinks': nrm(ks[5], (DEPTH, B_Q_HEADS), 0.5),
        'conv_w': nrm(ks[6], (DEPTH, CONV_WIDTH, BRANCH_WIDTH), CONV_WIDTH ** -0.5),
        'conv_b': nrm(ks[7], (DEPTH, BRANCH_WIDTH), 0.01),
        'w_r': nrm(ks[8], (DEPTH, C_BLOCKS, C_BLOCK_DIM, C_BLOCK_DIM), C_BLOCK_DIM ** -0.5),
        'b_r': nrm(ks[9], (DEPTH, BRANCH_WIDTH), 0.01),
        'w_i': nrm(ks[10], (DEPTH, C_BLOCKS, C_BLOCK_DIM, C_BLOCK_DIM), C_BLOCK_DIM ** -0.5),
        'b_i': nrm(ks[11], (DEPTH, BRANCH_WIDTH), 0.01),
        'lam': lam,
        'w_br': nrm(ks[13], (DEPTH, N_BRANCH, BRANCH_WIDTH, D_MODEL), BRANCH_WIDTH ** -0.5),
        'w_out': nrm(ks[14], (DEPTH, D_MODEL, D_MODEL), D_MODEL ** -0.5),
        'ple_norm_g': 1.0 + nrm(ks[15], (DEPTH, D_MODEL), 0.02),
        'w_pg': nrm(ks[16], (DEPTH, D_MODEL, D_MODEL), D_MODEL ** -0.5),
        'w_pp': nrm(ks[17], (DEPTH, PLE_DIM, D_MODEL), PLE_DIM ** -0.5),
        'final_norm_g': 1.0 + nrm(ks[18], (D_MODEL,), 0.02),
    }


def reference(x, p, rpe_table, norm_g, w_in, sinks, conv_w, conv_b, w_r, b_r, w_i, b_i, lam,
              w_br, w_out, ple_norm_g, w_pg, w_pp, final_norm_g):
    for i in range(DEPTH):
        x = mixer_layer(x, p[i], rpe_table, norm_g[i], w_in[i], sinks[i], conv_w[i], conv_b[i],
                        w_r[i], b_r[i], w_i[i], b_i[i], lam[i], w_br[i], w_out[i],
                        ple_norm_g[i], w_pg[i], w_pp[i])
    return rmsnorm(x, final_norm_g)
```

```python
import functools
import math

import jax
import jax.numpy as jnp
import numpy as np
from jax import lax
from jax.experimental import pallas as pl
from jax.experimental.pallas import tpu as pltpu

F32 = jnp.float32
BF16 = jnp.bfloat16

D_MODEL = 2048
PLE_DIM = 256
BRANCH_WIDTH = 1024
A_HEADS = 8
A_HEAD_DIM = 128
MOBA_BLOCK = 256
MOBA_TOPK = 3
B_Q_HEADS = 16
B_KV_HEADS = 4
B_GROUP = B_Q_HEADS // B_KV_HEADS
B_HEAD_DIM = 64
WINDOW = 128
C_BLOCKS = 8
C_BLOCK_DIM = 128
CONV_WIDTH = 4
LRU_C = 8.0
RPE_BUCKETS = 32
RPE_MAX_DIST = 128
EPS = 1e-6
NEG = -1e30

LANE = 128

COL_QA = 0
COL_KA = COL_QA + BRANCH_WIDTH
COL_VA = COL_KA + BRANCH_WIDTH
COL_GA = COL_VA + BRANCH_WIDTH
COL_QB = COL_GA + BRANCH_WIDTH
COL_KB = COL_QB + BRANCH_WIDTH
COL_VB = COL_KB + B_KV_HEADS * B_HEAD_DIM
COL_GB = COL_VB + B_KV_HEADS * B_HEAD_DIM
COL_XC = COL_GB + BRANCH_WIDTH
COL_GC = COL_XC + BRANCH_WIDTH
COL_MG = COL_GC + BRANCH_WIDTH
IN_WIDTH = COL_MG + 3 * D_MODEL

PROJ_TM = 1024
PROJ_TN = 512
MERGE_TM = 512
MERGE_TN = 512
PLE_TM = 512
LRU_CHUNK = 256

VMEM_LIMIT = 56 * 1024 * 1024


def _params(sem):
    return pltpu.CompilerParams(dimension_semantics=sem, vmem_limit_bytes=VMEM_LIMIT)


def _nt_dot(a, b, precision=None):
    return lax.dot_general(a, b, (((1,), (1,)), ((), ())), precision=precision,
                           preferred_element_type=F32)


def _rms(xf, g):
    return xf * lax.rsqrt(jnp.mean(xf * xf, axis=-1, keepdims=True) + EPS) * g


def _in_proj_kernel(x_ref, g_ref, w_ref, o_ref, h_sc):
    @pl.when(pl.program_id(1) == 0)
    def _():
        h_sc[...] = _rms(x_ref[...], g_ref[...]).astype(BF16)

    o_ref[...] = jnp.dot(h_sc[...], w_ref[...], preferred_element_type=F32)


def _in_proj(x2d, g, w_bf):
    t, d = x2d.shape
    n = w_bf.shape[1]
    return pl.pallas_call(
        _in_proj_kernel,
        grid=(t // PROJ_TM, n // PROJ_TN),
        in_specs=[
            pl.BlockSpec((PROJ_TM, d), lambda i, j: (i, 0)),
            pl.BlockSpec((1, d), lambda i, j: (0, 0)),
            pl.BlockSpec((d, PROJ_TN), lambda i, j: (0, j)),
        ],
        out_specs=pl.BlockSpec((PROJ_TM, PROJ_TN), lambda i, j: (i, j)),
        out_shape=jax.ShapeDtypeStruct((t, n), F32),
        scratch_shapes=[pltpu.VMEM((PROJ_TM, d), BF16)],
        compiler_params=_params(("parallel", "arbitrary")),
        name="in_proj",
    )(x2d, g.reshape(1, d), w_bf)


def _t5_bucket_np(dist):
    n = np.maximum(dist, 0)
    max_exact = RPE_BUCKETS // 2
    ratio = np.log(np.maximum(n, 1).astype(np.float32) / np.float32(max_exact)) / np.float32(
        math.log(RPE_MAX_DIST / max_exact))
    large = max_exact + (ratio * np.float32(RPE_BUCKETS - max_exact)).astype(np.int32)
    large = np.minimum(large, RPE_BUCKETS - 1)
    return np.where(n < max_exact, n, large).astype(np.int32)


def _moba_bias(rpe_table):
    L = MOBA_BLOCK
    qi = np.arange(L)[:, None]
    ki = np.arange(L)[None, :]
    d_own = qi - ki
    d_prev = L + qi - ki
    d_far = 2 * L + qi - ki
    assert d_far.min() >= RPE_MAX_DIST
    table = rpe_table[:, :A_HEADS].T.astype(F32)
    own = jnp.where(d_own >= 0, table[:, _t5_bucket_np(d_own)], NEG)
    prev = table[:, _t5_bucket_np(d_prev)]
    far = table[:, _t5_bucket_np(d_far)]
    return jnp.stack([own, prev, far], axis=1)


def _swa_bias(rpe_table, sinks):
    W = WINDOW
    qi = np.arange(W)[:, None]
    kk = np.arange(2 * W)[None, :]
    dist = qi - kk + W
    ok = (dist >= 0) & (dist < W)
    ok_first = ok & (kk >= W)
    table = rpe_table[:, A_HEADS:].T.astype(F32)
    b = table[:, _t5_bucket_np(dist)]
    planes = jnp.stack([jnp.where(ok, b, NEG), jnp.where(ok_first, b, NEG)], axis=1)
    planes = planes.reshape(B_KV_HEADS, B_GROUP, 2, W, 2 * W).transpose(0, 2, 1, 3, 4)
    planes = planes.reshape(B_KV_HEADS, 2, B_GROUP * W, 2 * W)
    sink = jnp.broadcast_to(sinks.astype(F32).reshape(B_KV_HEADS, B_GROUP, 1, 1),
                            (B_KV_HEADS, B_GROUP, W, LANE)).reshape(B_KV_HEADS, B_GROUP * W, LANE)
    return planes, sink


def _silu(x):
    return x * jax.nn.sigmoid(x)


def _moba_kernel(q_ref, k_ref, v_ref, g_ref, bias_ref, o_ref,
                 kaug_sc, v_sc, kmean_sc, qaug_sc, m_sc, l_sc, acc_sc, *, nb):
    L = MOBA_BLOCK
    dh = A_HEAD_DIM
    j = pl.program_id(2)

    @pl.when(j == 0)
    def _():
        kaug_sc[:, :dh] = k_ref[...].astype(BF16)
        rows = lax.broadcasted_iota(jnp.int32, (nb * L, LANE), 0)
        lanes = lax.broadcasted_iota(jnp.int32, (nb * L, LANE), 1)
        lo = lanes * L
        kaug_sc[:, dh:] = jnp.where((rows >= lo) & (rows < lo + L), 1.0, 0.0).astype(BF16)
        v_sc[...] = v_ref[...].astype(BF16)
        kmean_sc[...] = jnp.zeros_like(kmean_sc)
        for n in range(nb):
            kmean_sc[n:n + 1, :] = jnp.sum(k_ref[n * L:(n + 1) * L, :], axis=0,
                                           keepdims=True) * (1.0 / L)

    q = q_ref[...]
    gate = _nt_dot(q, kmean_sc[...], precision=lax.Precision.HIGHEST)
    lane = lax.broadcasted_iota(jnp.int32, (L, LANE), 1).astype(F32)
    jf = j.astype(F32)
    gm = jnp.where(lane < jf, gate, NEG)
    picked = jnp.zeros((L, LANE), F32)
    for _ in range(MOBA_TOPK):
        mx = jnp.max(gm, axis=1, keepdims=True)
        first = jnp.min(jnp.where(gm == mx, lane, float(LANE)), axis=1, keepdims=True)
        hit = lane == first
        picked = jnp.where(hit, 1.0, picked)
        gm = jnp.where(hit, -jnp.inf, gm)
    allowed = ((picked > 0.0) & (lane < jf)) | (lane == jf)
    qaug_sc[:, :dh] = q.astype(BF16)
    qaug_sc[:, dh:] = jnp.where(allowed, 0.0, NEG).astype(BF16)

    m_sc[...] = jnp.full_like(m_sc, NEG)
    l_sc[...] = jnp.zeros_like(l_sc)
    acc_sc[...] = jnp.zeros_like(acc_sc)
    scale = dh ** -0.5

    def block(n, carry):
        start = pl.multiple_of(n * L, L)
        s = _nt_dot(qaug_sc[...], kaug_sc[pl.ds(start, L), :])
        plane = jnp.where(n == j, 0, jnp.where(n == j - 1, 1, 2))
        s = s * scale + bias_ref[0, plane]
        m_old = m_sc[...]
        m_new = jnp.maximum(m_old, jnp.max(s, axis=1, keepdims=True))
        alpha = jnp.exp(m_old - m_new)
        p = jnp.exp(s - m_new)
        l_sc[...] = alpha * l_sc[...] + jnp.sum(p, axis=1, keepdims=True)
        acc_sc[...] = alpha * acc_sc[...] + jnp.dot(
            p.astype(BF16), v_sc[pl.ds(start, L), :], preferred_element_type=F32)
        m_sc[...] = m_new
        return carry

    lax.fori_loop(0, j + 1, block, 0)
    o_ref[...] = (acc_sc[...] / l_sc[...] * _silu(g_ref[...])).astype(o_ref.dtype)


def _moba_branch(proj, bias, bsz, seq):
    L = MOBA_BLOCK
    dh = A_HEAD_DIM
    nb = seq // L
    assert min(MOBA_TOPK, nb - 1) == MOBA_TOPK and nb <= LANE
    cq, ck, cv, cg = (c // dh for c in (COL_QA, COL_KA, COL_VA, COL_GA))
    return pl.pallas_call(
        functools.partial(_moba_kernel, nb=nb),
        grid=(bsz, A_HEADS, nb),
        in_specs=[
            pl.BlockSpec((L, dh), lambda b, h, j: (b * nb + j, cq + h)),
            pl.BlockSpec((seq, dh), lambda b, h, j: (b, ck + h)),
            pl.BlockSpec((seq, dh), lambda b, h, j: (b, cv + h)),
            pl.BlockSpec((L, dh), lambda b, h, j: (b * nb + j, cg + h)),
            pl.BlockSpec((1, 3, L, L), lambda b, h, j: (h, 0, 0, 0)),
        ],
        out_specs=pl.BlockSpec((L, dh), lambda b, h, j: (b * nb + j, h)),
        out_shape=jax.ShapeDtypeStruct((bsz * seq, BRANCH_WIDTH), BF16),
        scratch_shapes=[
            pltpu.VMEM((seq, 2 * dh), BF16),
            pltpu.VMEM((seq, dh), BF16),
            pltpu.VMEM((LANE, dh), F32),
            pltpu.VMEM((L, 2 * dh), BF16),
            pltpu.VMEM((L, 1), F32),
            pltpu.VMEM((L, 1), F32),
            pltpu.VMEM((L, dh), F32),
        ],
        compiler_params=_params(("parallel", "parallel", "arbitrary")),
        name="moba",
    )(proj, proj, proj, proj, bias)


def _swa_kernel(q_ref, kp_ref, kc_ref, vp_ref, vc_ref, g0_ref, g1_ref, bias_ref, sink_ref, o_ref):
    W = WINDOW
    dh = B_HEAD_DIM
    gw = B_GROUP * dh
    plane = jnp.where(pl.program_id(1) == 0, 1, 0)
    scale = dh ** -0.5
    for g in range(B_KV_HEADS):
        qg = q_ref[:, g * gw:(g + 1) * gw]
        q_stack = jnp.concatenate([qg[:, h * dh:(h + 1) * dh] for h in range(B_GROUP)],
                                  axis=0).astype(BF16)
        kband = jnp.concatenate([kp_ref[:, g * dh:(g + 1) * dh], kc_ref[:, g * dh:(g + 1) * dh]],
                                axis=0).astype(BF16)
        vband = jnp.concatenate([vp_ref[:, g * dh:(g + 1) * dh], vc_ref[:, g * dh:(g + 1) * dh]],
                                axis=0).astype(BF16)
        s = _nt_dot(q_stack, kband) * scale + bias_ref[g, plane]
        sink = sink_ref[g][:, :1]
        m = jnp.maximum(jnp.max(s, axis=1, keepdims=True), sink)
        p = jnp.exp(s - m)
        denom = jnp.sum(p, axis=1, keepdims=True) + jnp.exp(sink - m)
        o = jnp.dot(p.astype(BF16), vband, preferred_element_type=F32) / denom
        og = jnp.concatenate([o[h * W:(h + 1) * W, :] for h in range(B_GROUP)], axis=1)
        half = (g % 2) * gw
        gate = (g0_ref if g < 2 else g1_ref)[:, half:half + gw]
        o_ref[:, g * gw:(g + 1) * gw] = (og * _silu(gate)).astype(o_ref.dtype)


def _swa_branch(proj, bias, sink, bsz, seq):
    W = WINDOW
    nb = seq // W
    kvw = B_KV_HEADS * B_HEAD_DIM
    cq = COL_QB // BRANCH_WIDTH
    ck = COL_KB // kvw
    cv = COL_VB // kvw
    cg = COL_GB // (BRANCH_WIDTH // 2)
    assert COL_QB % BRANCH_WIDTH == 0 and COL_KB % kvw == 0 and COL_VB % kvw == 0
    assert COL_GB % (BRANCH_WIDTH // 2) == 0
    prev = lambda b, n: (b * nb + jnp.maximum(n - 1, 0))
    return pl.pallas_call(
        _swa_kernel,
        grid=(bsz, nb),
        in_specs=[
            pl.BlockSpec((W, BRANCH_WIDTH), lambda b, n: (b * nb + n, cq)),
            pl.BlockSpec((W, kvw), lambda b, n: (prev(b, n), ck)),
            pl.BlockSpec((W, kvw), lambda b, n: (b * nb + n, ck)),
            pl.BlockSpec((W, kvw), lambda b, n: (prev(b, n), cv)),
            pl.BlockSpec((W, kvw), lambda b, n: (b * nb + n, cv)),
            pl.BlockSpec((W, BRANCH_WIDTH // 2), lambda b, n: (b * nb + n, cg)),
            pl.BlockSpec((W, BRANCH_WIDTH // 2), lambda b, n: (b * nb + n, cg + 1)),
            pl.BlockSpec(bias.shape, lambda b, n: (0, 0, 0, 0)),
            pl.BlockSpec(sink.shape, lambda b, n: (0, 0, 0)),
        ],
        out_specs=pl.BlockSpec((W, BRANCH_WIDTH), lambda b, n: (b * nb + n, 0)),
        out_shape=jax.ShapeDtypeStruct((bsz * seq, BRANCH_WIDTH), BF16),
        compiler_params=_params(("parallel", "arbitrary")),
        name="swa",
    )(proj, proj, proj, proj, proj, proj, proj, bias, sink)


def _rglru_kernel(xc_ref, gc_ref, cw_ref, cb_ref, wr_ref, br_ref, wi_ref, bi_ref, lam_ref, o_ref,
                  xpad_sc, h_sc, *, seq):
    T = LRU_CHUNK
    PAD = 8
    xpad_sc[:PAD, :] = jnp.zeros((PAD, C_BLOCK_DIM), F32)
    xpad_sc[PAD:, :] = xc_ref[...]
    h_sc[...] = jnp.zeros_like(h_sc)
    nlam = -lam_ref[...]
    softplus = jnp.maximum(nlam, 0.0) + jnp.log1p(jnp.exp(-jnp.abs(nlam)))
    decay = -LRU_C * softplus
    row = lax.broadcasted_iota(jnp.int32, (T, C_BLOCK_DIM), 0)

    def chunk(c, carry):
        t0 = pl.multiple_of(c * T, T)
        xw = xpad_sc[pl.ds(t0, T + PAD), :]
        conv = cb_ref[...]
        for w in range(CONV_WIDTH):
            off = PAD - (CONV_WIDTH - 1) + w
            conv = conv + cw_ref[w:w + 1, :] * xw[off:off + T, :]
        cbf = conv.astype(BF16)
        r = jax.nn.sigmoid(jnp.dot(cbf, wr_ref[0], preferred_element_type=F32) + br_ref[...])
        i = jax.nn.sigmoid(jnp.dot(cbf, wi_ref[0], preferred_element_type=F32) + bi_ref[...])
        log_a = r * decay
        a = jnp.exp(log_a)
        b = jnp.sqrt(jnp.maximum(-jnp.tanh(log_a) * (a * a + 1.0), 0.0)) * (i * conv)
        k = 1
        while k < T:
            keep = row >= k
            a_prev = jnp.where(keep, pltpu.roll(a, k, 0), 1.0)
            b_prev = jnp.where(keep, pltpu.roll(b, k, 0), 0.0)
            b = a * b_prev + b
            a = a * a_prev
            k *= 2
        h = a * h_sc[...] + b
        h_sc[...] = h[T - 1:T, :]
        o_ref[pl.ds(t0, T), :] = (h * _silu(gc_ref[pl.ds(t0, T), :])).astype(o_ref.dtype)
        return carry

    lax.fori_loop(0, seq // T, chunk, 0)


def _rglru_branch(proj, conv_w, conv_b, w_r, b_r, w_i, b_i, lam, bsz, seq):
    cd = C_BLOCK_DIM
    cx = COL_XC // cd
    cg = COL_GC // cd
    vec = lambda v: v.reshape(1, BRANCH_WIDTH)
    vspec = pl.BlockSpec((1, cd), lambda b, c: (0, c))
    wspec = pl.BlockSpec((1, cd, cd), lambda b, c: (c, 0, 0))
    return pl.pallas_call(
        functools.partial(_rglru_kernel, seq=seq),
        grid=(bsz, C_BLOCKS),
        in_specs=[
            pl.BlockSpec((seq, cd), lambda b, c: (b, cx + c)),
            pl.BlockSpec((seq, cd), lambda b, c: (b, cg + c)),
            pl.BlockSpec((CONV_WIDTH, cd), lambda b, c: (0, c)),
            vspec, wspec, vspec, wspec, vspec, vspec,
        ],
        out_specs=pl.BlockSpec((seq, cd), lambda b, c: (b, c)),
        out_shape=jax.ShapeDtypeStruct((bsz * seq, BRANCH_WIDTH), BF16),
        scratch_shapes=[pltpu.VMEM((seq + 8, cd), F32), pltpu.VMEM((1, cd), F32)],
        compiler_params=_params(("parallel", "parallel")),
        name="rglru",
    )(proj, proj, conv_w, vec(conv_b), w_r.astype(BF16), vec(b_r), w_i.astype(BF16), vec(b_i),
      vec(lam))


def _merge_kernel(ya_ref, yb_ref, yc_ref, ma_ref, mb_ref, mc_ref, wbr_ref, wout_ref, x_ref, o_ref):
    merged = None
    for n, (y_ref, m_ref) in enumerate(((ya_ref, ma_ref), (yb_ref, mb_ref), (yc_ref, mc_ref))):
        term = jax.nn.sigmoid(m_ref[...]) * jnp.dot(y_ref[...], wbr_ref[n],
                                                    preferred_element_type=F32)
        merged = term if merged is None else merged + term
    contrib = jnp.dot(merged.astype(BF16), wout_ref[...], preferred_element_type=F32)

    @pl.when(pl.program_id(1) == 0)
    def _():
        o_ref[...] = x_ref[...] + contrib

    @pl.when(pl.program_id(1) != 0)
    def _():
        o_ref[...] += contrib


def _merge(ya, yb, yc, proj, wbr_bf, wout_bf, x2d):
    t, d = x2d.shape
    tm, tn = MERGE_TM, MERGE_TN
    assert COL_MG % tn == 0 and d % tn == 0
    cm = COL_MG // tn
    per = d // tn
    yspec = pl.BlockSpec((tm, BRANCH_WIDTH), lambda i, c: (i, 0))
    mspec = lambda n: pl.BlockSpec((tm, tn), lambda i, c: (i, cm + n * per + c))
    return pl.pallas_call(
        _merge_kernel,
        grid=(t // tm, per),
        in_specs=[
            yspec, yspec, yspec, mspec(0), mspec(1), mspec(2),
            pl.BlockSpec((3, BRANCH_WIDTH, tn), lambda i, c: (0, 0, c)),
            pl.BlockSpec((tn, d), lambda i, c: (c, 0)),
            pl.BlockSpec((tm, d), lambda i, c: (i, 0)),
        ],
        out_specs=pl.BlockSpec((tm, d), lambda i, c: (i, 0)),
        out_shape=jax.ShapeDtypeStruct((t, d), F32),
        compiler_params=_params(("parallel", "arbitrary")),
        name="merge",
    )(ya, yb, yc, proj, proj, proj, wbr_bf, wout_bf, x2d)


def _ple_kernel(x_ref, p_ref, g_ref, wpg_ref, wpp_ref, fg_ref, o_ref, *, final):
    x = x_ref[...]
    h = _rms(x, g_ref[...]).astype(BF16)
    gate = jax.nn.sigmoid(jnp.dot(h, wpg_ref[...], preferred_element_type=F32))
    emb = jnp.dot(p_ref[...].astype(BF16), wpp_ref[...], preferred_element_type=F32)
    y = x + gate * emb
    if final:
        y = _rms(y, fg_ref[...])
    o_ref[...] = y


def _ple(x2d, p2d, g, wpg_bf, wpp_bf, final_g, final):
    t, d = x2d.shape
    tm = PLE_TM
    gspec = pl.BlockSpec((1, d), lambda i: (0, 0))
    return pl.pallas_call(
        functools.partial(_ple_kernel, final=final),
        grid=(t // tm,),
        in_specs=[
            pl.BlockSpec((tm, d), lambda i: (i, 0)),
            pl.BlockSpec((tm, PLE_DIM), lambda i: (i, 0)),
            gspec,
            pl.BlockSpec((d, d), lambda i: (0, 0)),
            pl.BlockSpec((PLE_DIM, d), lambda i: (0, 0)),
            gspec,
        ],
        out_specs=pl.BlockSpec((tm, d), lambda i: (i, 0)),
        out_shape=jax.ShapeDtypeStruct((t, d), F32),
        compiler_params=_params(("parallel",)),
        name="ple",
    )(x2d, p2d, g.reshape(1, d), wpg_bf, wpp_bf, final_g.reshape(1, d))


def kernel(x, p, rpe_table, norm_g, w_in, sinks, conv_w, conv_b, w_r, b_r, w_i, b_i, lam, w_br,
           w_out, ple_norm_g, w_pg, w_pp, final_norm_g):
    bsz, seq, d = x.shape
    depth = w_in.shape[0]
    assert d == D_MODEL and w_in.shape[2] == IN_WIDTH
    assert seq % MOBA_BLOCK == 0 and seq % LRU_CHUNK == 0 and (bsz * seq) % PROJ_TM == 0
    t = bsz * seq
    xf = x.reshape(t, d)
    bias_a = _moba_bias(rpe_table)
    for i in range(depth):
        proj = _in_proj(xf, norm_g[i], w_in[i].astype(BF16))
        bias_b, sink_b = _swa_bias(rpe_table, sinks[i])
        ya = _moba_branch(proj, bias_a, bsz, seq)
        yb = _swa_branch(proj, bias_b, sink_b, bsz, seq)
        yc = _rglru_branch(proj, conv_w[i], conv_b[i], w_r[i], b_r[i], w_i[i], b_i[i], lam[i],
                           bsz, seq)
        x1 = _merge(ya, yb, yc, proj, w_br[i].astype(BF16), w_out[i].astype(BF16), xf)
        xf = _ple(x1, p[i].reshape(t, PLE_DIM), ple_norm_g[i], w_pg[i].astype(BF16),
                  w_pp[i].astype(BF16), final_norm_g, final=(i == depth - 1))
    return xf.reshape(bsz, seq, d)
```

```python
import functools
import math

import jax
import jax.numpy as jnp
import numpy as np
from jax import lax
from jax.experimental import pallas as pl
from jax.experimental.pallas import tpu as pltpu

F32 = jnp.float32
BF16 = jnp.bfloat16

D_MODEL = 2048
PLE_DIM = 256
BRANCH_WIDTH = 1024
A_HEADS = 8
A_HEAD_DIM = 128
MOBA_BLOCK = 256
MOBA_TOPK = 3
B_Q_HEADS = 16
B_KV_HEADS = 4
B_GROUP = B_Q_HEADS // B_KV_HEADS
B_HEAD_DIM = 64
WINDOW = 128
C_BLOCKS = 8
C_BLOCK_DIM = 128
CONV_WIDTH = 4
LRU_C = 8.0
RPE_BUCKETS = 32
RPE_MAX_DIST = 128
EPS = 1e-6
NEG = -1e30

COL_QA = 0
COL_KA = COL_QA + BRANCH_WIDTH
COL_VA = COL_KA + BRANCH_WIDTH
COL_GA = COL_VA + BRANCH_WIDTH
COL_QB = COL_GA + BRANCH_WIDTH
COL_KB = COL_QB + BRANCH_WIDTH
COL_VB = COL_KB + B_KV_HEADS * B_HEAD_DIM
COL_GB = COL_VB + B_KV_HEADS * B_HEAD_DIM
COL_XC = COL_GB + BRANCH_WIDTH
COL_GC = COL_XC + BRANCH_WIDTH
COL_MG = COL_GC + BRANCH_WIDTH
IN_WIDTH = COL_MG + 3 * D_MODEL

PROJ_TM = 1024
PROJ_TN = 512
MERGE_TM = 512
MERGE_TN = 512
PLE_TM = 512
LRU_CHUNK = 256

VMEM_LIMIT = 56 * 1024 * 1024


def _params(sem):
    return pltpu.CompilerParams(dimension_semantics=sem, vmem_limit_bytes=VMEM_LIMIT)


def _rms(xf, g):
    return xf * lax.rsqrt(jnp.mean(xf * xf, axis=-1, keepdims=True) + EPS) * g


def _in_proj_kernel(x_ref, g_ref, w_ref, o_ref, h_sc):
    @pl.when(pl.program_id(1) == 0)
    def _():
        h_sc[...] = _rms(x_ref[...], g_ref[...]).astype(BF16)

    o_ref[...] = jnp.dot(h_sc[...], w_ref[...], preferred_element_type=F32)


def _in_proj(x2d, g, w_bf):
    t, d = x2d.shape
    n = w_bf.shape[1]
    return pl.pallas_call(
        _in_proj_kernel,
        grid=(t // PROJ_TM, n // PROJ_TN),
        in_specs=[
            pl.BlockSpec((PROJ_TM, d), lambda i, j: (i, 0)),
            pl.BlockSpec((1, d), lambda i, j: (0, 0)),
            pl.BlockSpec((d, PROJ_TN), lambda i, j: (0, j)),
        ],
        out_specs=pl.BlockSpec((PROJ_TM, PROJ_TN), lambda i, j: (i, j)),
        out_shape=jax.ShapeDtypeStruct((t, n), F32),
        scratch_shapes=[pltpu.VMEM((PROJ_TM, d), BF16)],
        compiler_params=_params(("parallel", "arbitrary")),
        name="in_proj",
    )(x2d, g.reshape(1, d), w_bf)


MASKED_BUCKET = RPE_BUCKETS


def _t5_bucket_np(dist):
    n = np.maximum(dist, 0)
    max_exact = RPE_BUCKETS // 2
    ratio = np.log(np.maximum(n, 1).astype(np.float32) / np.float32(max_exact)) / np.float32(
        math.log(RPE_MAX_DIST / max_exact))
    large = max_exact + (ratio * np.float32(RPE_BUCKETS - max_exact)).astype(np.int32)
    large = np.minimum(large, RPE_BUCKETS - 1)
    return np.where(n < max_exact, n, large).astype(np.int32)


def _moba_bucket_planes():
    L = MOBA_BLOCK
    ki = np.arange(L)[:, None]
    qi = np.arange(L)[None, :]
    d_own = qi - ki
    d_far = 2 * L + qi - ki
    assert d_far.min() >= RPE_MAX_DIST
    own = np.where(d_own >= 0, _t5_bucket_np(d_own), MASKED_BUCKET)
    return np.stack([own, _t5_bucket_np(L + qi - ki), _t5_bucket_np(d_far)]).astype(np.int32)


def _swa_bucket_planes():
    W = WINDOW
    kk = np.arange(2 * W)[:, None]
    qi = np.arange(W)[None, :]
    dist = qi - kk + W
    ok = (dist >= 0) & (dist < W)
    b = _t5_bucket_np(dist)
    return np.stack([np.where(ok, b, MASKED_BUCKET),
                     np.where(ok & (kk >= W), b, MASKED_BUCKET)]).astype(np.int32)


def _bias_lookup_kernel(tab_ref, idx_ref, o_ref):
    h = pl.program_id(0)
    idx = idx_ref[0]
    acc = jnp.full(idx.shape, NEG, F32)
    for b in range(RPE_BUCKETS):
        acc = jnp.where(idx == b, tab_ref[h * RPE_BUCKETS + b], acc)
    o_ref[0, 0] = acc


def _bias_lookup(table_hb, planes, heads_per_row):
    nh = table_hb.shape[0]
    npl, r, c = planes.shape
    return pl.pallas_call(
        _bias_lookup_kernel,
        grid=(nh, npl),
        in_specs=[
            pl.BlockSpec(memory_space=pltpu.SMEM),
            pl.BlockSpec((1, r, c), lambda h, p: (p, 0, 0)),
        ],
        out_specs=pl.BlockSpec((1, 1, r, c),
                               lambda h, p: (h // heads_per_row, p, 0, h % heads_per_row)),
        out_shape=jax.ShapeDtypeStruct((nh // heads_per_row, npl, r, c * heads_per_row), F32),
        compiler_params=_params(("parallel", "parallel")),
        name="bias_lookup",
    )(table_hb.reshape(-1), jnp.asarray(planes))


GATE_ROWS = 16


def _silu(x):
    return x * jax.nn.sigmoid(x)


MOBA_GROUP = 2
MOBA_HEADS = 2


def _moba_kernel(q_ref, k_ref, v_ref, g_ref, bias_ref, o_ref,
                 kaug_sc, vt_sc, kmean_sc, qaug_sc, m_sc, l_sc, acc_sc, *, nb):
    L = MOBA_BLOCK
    dh = A_HEAD_DIM
    G = MOBA_GROUP
    heads = range(MOBA_HEADS)
    hcol = lambda u: slice(u * dh, (u + 1) * dh)

    lanes = lax.broadcasted_iota(jnp.int32, (L, dh), 1)
    kmean_sc[...] = jnp.zeros_like(kmean_sc)
    for u in heads:
        for n in range(nb):
            rows = slice((n % G) * L, (n % G + 1) * L)
            kn = k_ref[n * L:(n + 1) * L, hcol(u)]
            kaug_sc[u, n // G, rows, :dh] = kn.astype(BF16)
            kaug_sc[u, n // G, rows, dh:] = jnp.where(lanes == n, 1.0, 0.0).astype(BF16)
            kmean_sc[u, n:n + 1, :] = jnp.sum(kn, axis=0, keepdims=True) * (1.0 / L)
            vt_sc[u, n // G, :, rows] = v_ref[n * L:(n + 1) * L, hcol(u)].T.astype(BF16)

    blk = lax.broadcasted_iota(jnp.int32, (GATE_ROWS, L), 0)
    blk_f = blk.astype(F32)
    for u in heads:
        for j in range(nb):
            qt = q_ref[j * L:(j + 1) * L, hcol(u)].T
            gate = jnp.dot(kmean_sc[u], qt, precision=lax.Precision.HIGHEST,
                           preferred_element_type=F32)
            gm = jnp.where(blk < j, gate, NEG)
            picked = jnp.zeros((GATE_ROWS, L), F32)
            for _ in range(MOBA_TOPK):
                mx = jnp.max(gm, axis=0, keepdims=True)
                first = jnp.min(jnp.where(gm == mx, blk_f, float(GATE_ROWS)), axis=0,
                                keepdims=True)
                hit = blk_f == first
                picked = jnp.where(hit, 1.0, picked)
                gm = jnp.where(hit, -jnp.inf, gm)
            allowed = ((picked > 0.0) & (blk < j)) | (blk == j)
            qaug_sc[u, j, :dh, :] = qt.astype(BF16)
            qaug_sc[u, j, dh:dh + GATE_ROWS, :] = jnp.where(allowed, 0.0, NEG).astype(BF16)
            qaug_sc[u, j, dh + GATE_ROWS:, :] = jnp.zeros((dh - GATE_ROWS, L), BF16)

    scale = dh ** -0.5

    def qblock(j, carry):
        m_sc[...] = jnp.full_like(m_sc, NEG)
        l_sc[...] = jnp.zeros_like(l_sc)
        acc_sc[...] = jnp.zeros_like(acc_sc)

        def group(i, c):
            planes = []
            for t in range(G):
                n = i * G + t
                planes.append(jnp.where(n == j, 0, jnp.where(n == j - 1, 1, 2)))
            for u in heads:
                s = jnp.dot(kaug_sc[u, i], qaug_sc[u, j], preferred_element_type=F32)
                parts = [s[t * L:(t + 1) * L, :] * scale + bias_ref[u, planes[t]]
                         for t in range(G)]
                m_old = m_sc[u]
                m_new = m_old
                for sp in parts:
                    m_new = jnp.maximum(m_new, jnp.max(sp, axis=0, keepdims=True))
                alpha = jnp.exp(m_old - m_new)
                probs = [jnp.exp(sp - m_new) for sp in parts]
                l_new = alpha * l_sc[u]
                for pr in probs:
                    l_new = l_new + jnp.sum(pr, axis=0, keepdims=True)
                l_sc[u] = l_new
                p = jnp.concatenate(probs, axis=0).astype(BF16)
                acc_sc[u] = alpha * acc_sc[u] + jnp.dot(vt_sc[u, i], p,
                                                        preferred_element_type=F32)
                m_sc[u] = m_new
            return c

        lax.fori_loop(0, (j + G) // G, group, 0)
        rows = pl.ds(pl.multiple_of(j * L, L), L)
        for u in heads:
            out_t = acc_sc[u] / l_sc[u]
            o_ref[rows, hcol(u)] = (out_t.T * _silu(g_ref[rows, hcol(u)])).astype(o_ref.dtype)
        return carry

    lax.fori_loop(0, nb, qblock, 0)


def _moba_branch(proj, bias, bsz, seq):
    L = MOBA_BLOCK
    dh = A_HEAD_DIM
    nb = seq // L
    hp = MOBA_HEADS
    assert min(MOBA_TOPK, nb - 1) == MOBA_TOPK and nb <= GATE_ROWS and nb % MOBA_GROUP == 0
    assert A_HEADS % hp == 0
    cq, ck, cv, cg = (c // (hp * dh) for c in (COL_QA, COL_KA, COL_VA, COL_GA))
    col = lambda c: pl.BlockSpec((seq, hp * dh), lambda b, h: (b, c + h))
    return pl.pallas_call(
        functools.partial(_moba_kernel, nb=nb),
        grid=(bsz, A_HEADS // hp),
        in_specs=[col(cq), col(ck), col(cv), col(cg),
                  pl.BlockSpec((hp, 3, L, L), lambda b, h: (h, 0, 0, 0))],
        out_specs=col(0),
        out_shape=jax.ShapeDtypeStruct((bsz * seq, BRANCH_WIDTH), BF16),
        scratch_shapes=[
            pltpu.VMEM((hp, nb // MOBA_GROUP, MOBA_GROUP * L, 2 * dh), BF16),
            pltpu.VMEM((hp, nb // MOBA_GROUP, dh, MOBA_GROUP * L), BF16),
            pltpu.VMEM((hp, GATE_ROWS, dh), F32),
            pltpu.VMEM((hp, nb, 2 * dh, L), BF16),
            pltpu.VMEM((hp, 1, L), F32),
            pltpu.VMEM((hp, 1, L), F32),
            pltpu.VMEM((hp, dh, L), F32),
        ],
        compiler_params=_params(("parallel", "parallel")),
        name="moba",
    )(proj, proj, proj, proj, bias)


def _swa_kernel(q_ref, kp_ref, kc_ref, vp_ref, vc_ref, g0_ref, g1_ref, bias_ref, sink_ref, o_ref,
                ot_sc):
    W = WINDOW
    dh = B_HEAD_DIM
    kvw = B_KV_HEADS * dh
    plane = jnp.where(pl.program_id(1) == 0, 1, 0)
    scale = dh ** -0.5
    qt = q_ref[...].T
    head = lambda g, h: slice((g * B_GROUP + h) * dh, (g * B_GROUP + h + 1) * dh)
    x = jnp.concatenate(
        [jnp.concatenate([qt[head(g, h), :] for g in range(B_KV_HEADS)], axis=0)
         for h in range(B_GROUP)], axis=1).astype(BF16)
    kband = jnp.concatenate([kp_ref[...], kc_ref[...]], axis=0)
    vband_t = jnp.concatenate([vp_ref[...], vc_ref[...]], axis=0).T.astype(BF16)
    lane_group = lax.broadcasted_iota(jnp.int32, (2 * W, kvw), 1) // dh
    for g in range(B_KV_HEADS):
        kg = jnp.where(lane_group == g, kband, 0.0).astype(BF16)
        s = jnp.dot(kg, x, preferred_element_type=F32) * scale + bias_ref[g, plane]
        sink = sink_ref[g]
        m = jnp.maximum(jnp.max(s, axis=0, keepdims=True), sink)
        p = jnp.exp(s - m)
        denom = jnp.sum(p, axis=0, keepdims=True) + jnp.exp(sink - m)
        o = jnp.dot(vband_t[g * dh:(g + 1) * dh, :], p.astype(BF16),
                    preferred_element_type=F32) / denom
        for h in range(B_GROUP):
            ot_sc[head(g, h), :] = o[:, h * W:(h + 1) * W]
    gate = jnp.concatenate([g0_ref[...], g1_ref[...]], axis=1)
    o_ref[...] = (ot_sc[...].T * _silu(gate)).astype(o_ref.dtype)


def _swa_branch(proj, bias, sink, bsz, seq):
    W = WINDOW
    nb = seq // W
    kvw = B_KV_HEADS * B_HEAD_DIM
    cq = COL_QB // BRANCH_WIDTH
    ck = COL_KB // kvw
    cv = COL_VB // kvw
    cg = COL_GB // (BRANCH_WIDTH // 2)
    assert COL_QB % BRANCH_WIDTH == 0 and COL_KB % kvw == 0 and COL_VB % kvw == 0
    assert COL_GB % (BRANCH_WIDTH // 2) == 0
    prev = lambda b, n: (b * nb + jnp.maximum(n - 1, 0))
    return pl.pallas_call(
        _swa_kernel,
        grid=(bsz, nb),
        in_specs=[
            pl.BlockSpec((W, BRANCH_WIDTH), lambda b, n: (b * nb + n, cq)),
            pl.BlockSpec((W, kvw), lambda b, n: (prev(b, n), ck)),
            pl.BlockSpec((W, kvw), lambda b, n: (b * nb + n, ck)),
            pl.BlockSpec((W, kvw), lambda b, n: (prev(b, n), cv)),
            pl.BlockSpec((W, kvw), lambda b, n: (b * nb + n, cv)),
            pl.BlockSpec((W, BRANCH_WIDTH // 2), lambda b, n: (b * nb + n, cg)),
            pl.BlockSpec((W, BRANCH_WIDTH // 2), lambda b, n: (b * nb + n, cg + 1)),
            pl.BlockSpec(bias.shape, lambda b, n: (0, 0, 0, 0)),
            pl.BlockSpec(sink.shape, lambda b, n: (0, 0, 0)),
        ],
        out_specs=pl.BlockSpec((W, BRANCH_WIDTH), lambda b, n: (b * nb + n, 0)),
        out_shape=jax.ShapeDtypeStruct((bsz * seq, BRANCH_WIDTH), BF16),
        scratch_shapes=[pltpu.VMEM((BRANCH_WIDTH, W), F32)],
        compiler_params=_params(("parallel", "arbitrary")),
        name="swa",
    )(proj, proj, proj, proj, proj, proj, proj, bias, sink)


def _swa_sink_rows(sinks):
    s = sinks.astype(F32).reshape(B_KV_HEADS, 1, B_GROUP, 1)
    return jnp.broadcast_to(s, (B_KV_HEADS, 1, B_GROUP, WINDOW)).reshape(
        B_KV_HEADS, 1, B_GROUP * WINDOW)


def _rglru_kernel(xc_ref, gc_ref, cw_ref, cb_ref, wr_ref, br_ref, wi_ref, bi_ref, lam_ref, o_ref,
                  xpad_sc, h_sc, *, seq):
    T = LRU_CHUNK
    PAD = 8
    xpad_sc[:PAD, :] = jnp.zeros((PAD, C_BLOCK_DIM), F32)
    xpad_sc[PAD:, :] = xc_ref[...]
    h_sc[...] = jnp.zeros_like(h_sc)
    nlam = -lam_ref[...]
    softplus = jnp.maximum(nlam, 0.0) + jnp.log1p(jnp.exp(-jnp.abs(nlam)))
    decay = -LRU_C * softplus
    row = lax.broadcasted_iota(jnp.int32, (T, C_BLOCK_DIM), 0)

    def chunk(c, carry):
        t0 = pl.multiple_of(c * T, T)
        xw = xpad_sc[pl.ds(t0, T + PAD), :]
        conv = cb_ref[...]
        for w in range(CONV_WIDTH):
            off = PAD - (CONV_WIDTH - 1) + w
            conv = conv + cw_ref[w:w + 1, :] * xw[off:off + T, :]
        cbf = conv.astype(BF16)
        r = jax.nn.sigmoid(jnp.dot(cbf, wr_ref[0], preferred_element_type=F32) + br_ref[...])
        i = jax.nn.sigmoid(jnp.dot(cbf, wi_ref[0], preferred_element_type=F32) + bi_ref[...])
        log_a = r * decay
        a = jnp.exp(log_a)
        b = jnp.sqrt(jnp.maximum(-jnp.tanh(log_a) * (a * a + 1.0), 0.0)) * (i * conv)
        k = 1
        while k < T:
            keep = row >= k
            a_prev = jnp.where(keep, pltpu.roll(a, k, 0), 1.0)
            b_prev = jnp.where(keep, pltpu.roll(b, k, 0), 0.0)
            b = a * b_prev + b
            a = a * a_prev
            k *= 2
        h = a * h_sc[...] + b
        h_sc[...] = h[T - 1:T, :]
        o_ref[pl.ds(t0, T), :] = (h * _silu(gc_ref[pl.ds(t0, T), :])).astype(o_ref.dtype)
        return carry

    lax.fori_loop(0, seq // T, chunk, 0)


def _rglru_branch(proj, conv_w, conv_b, w_r, b_r, w_i, b_i, lam, bsz, seq):
    cd = C_BLOCK_DIM
    cx = COL_XC // cd
    cg = COL_GC // cd
    vec = lambda v: v.reshape(1, BRANCH_WIDTH)
    vspec = pl.BlockSpec((1, cd), lambda b, c: (0, c))
    wspec = pl.BlockSpec((1, cd, cd), lambda b, c: (c, 0, 0))
    return pl.pallas_call(
        functools.partial(_rglru_kernel, seq=seq),
        grid=(bsz, C_BLOCKS),
        in_specs=[
            pl.BlockSpec((seq, cd), lambda b, c: (b, cx + c)),
            pl.BlockSpec((seq, cd), lambda b, c: (b, cg + c)),
            pl.BlockSpec((CONV_WIDTH, cd), lambda b, c: (0, c)),
            vspec, wspec, vspec, wspec, vspec, vspec,
        ],
        out_specs=pl.BlockSpec((seq, cd), lambda b, c: (b, c)),
        out_shape=jax.ShapeDtypeStruct((bsz * seq, BRANCH_WIDTH), BF16),
        scratch_shapes=[pltpu.VMEM((seq + 8, cd), F32), pltpu.VMEM((1, cd), F32)],
        compiler_params=_params(("parallel", "parallel")),
        name="rglru",
    )(proj, proj, conv_w, vec(conv_b), w_r.astype(BF16), vec(b_r), w_i.astype(BF16), vec(b_i),
      vec(lam))


def _merge_kernel(ya_ref, yb_ref, yc_ref, ma_ref, mb_ref, mc_ref, wbr_ref, wout_ref, x_ref, o_ref):
    merged = None
    for n, (y_ref, m_ref) in enumerate(((ya_ref, ma_ref), (yb_ref, mb_ref), (yc_ref, mc_ref))):
        term = jax.nn.sigmoid(m_ref[...]) * jnp.dot(y_ref[...], wbr_ref[n],
                                                    preferred_element_type=F32)
        merged = term if merged is None else merged + term
    contrib = jnp.dot(merged.astype(BF16), wout_ref[...], preferred_element_type=F32)

    @pl.when(pl.program_id(1) == 0)
    def _():
        o_ref[...] = x_ref[...] + contrib

    @pl.when(pl.program_id(1) != 0)
    def _():
        o_ref[...] += contrib


def _merge(ya, yb, yc, proj, wbr_bf, wout_bf, x2d):
    t, d = x2d.shape
    tm, tn = MERGE_TM, MERGE_TN
    assert COL_MG % tn == 0 and d % tn == 0
    cm = COL_MG // tn
    per = d // tn
    yspec = pl.BlockSpec((tm, BRANCH_WIDTH), lambda i, c: (i, 0))
    mspec = lambda n: pl.BlockSpec((tm, tn), lambda i, c: (i, cm + n * per + c))
    return pl.pallas_call(
        _merge_kernel,
        grid=(t // tm, per),
        in_specs=[
            yspec, yspec, yspec, mspec(0), mspec(1), mspec(2),
            pl.BlockSpec((3, BRANCH_WIDTH, tn), lambda i, c: (0, 0, c)),
            pl.BlockSpec((tn, d), lambda i, c: (c, 0)),
            pl.BlockSpec((tm, d), lambda i, c: (i, 0)),
        ],
        out_specs=pl.BlockSpec((tm, d), lambda i, c: (i, 0)),
        out_shape=jax.ShapeDtypeStruct((t, d), F32),
        compiler_params=_params(("parallel", "arbitrary")),
        name="merge",
    )(ya, yb, yc, proj, proj, proj, wbr_bf, wout_bf, x2d)


def _ple_kernel(x_ref, p_ref, g_ref, wpg_ref, wpp_ref, fg_ref, o_ref, *, final):
    x = x_ref[...]
    h = _rms(x, g_ref[...]).astype(BF16)
    gate = jax.nn.sigmoid(jnp.dot(h, wpg_ref[...], preferred_element_type=F32))
    emb = jnp.dot(p_ref[...].astype(BF16), wpp_ref[...], preferred_element_type=F32)
    y = x + gate * emb
    if final:
        y = _rms(y, fg_ref[...])
    o_ref[...] = y


def _ple(x2d, p2d, g, wpg_bf, wpp_bf, final_g, final):
    t, d = x2d.shape
    tm = PLE_TM
    gspec = pl.BlockSpec((1, d), lambda i: (0, 0))
    return pl.pallas_call(
        functools.partial(_ple_kernel, final=final),
        grid=(t // tm,),
        in_specs=[
            pl.BlockSpec((tm, d), lambda i: (i, 0)),
            pl.BlockSpec((tm, PLE_DIM), lambda i: (i, 0)),
            gspec,
            pl.BlockSpec((d, d), lambda i: (0, 0)),
            pl.BlockSpec((PLE_DIM, d), lambda i: (0, 0)),
            gspec,
        ],
        out_specs=pl.BlockSpec((tm, d), lambda i: (i, 0)),
        out_shape=jax.ShapeDtypeStruct((t, d), F32),
        compiler_params=_params(("parallel",)),
        name="ple",
    )(x2d, p2d, g.reshape(1, d), wpg_bf, wpp_bf, final_g.reshape(1, d))


def kernel(x, p, rpe_table, norm_g, w_in, sinks, conv_w, conv_b, w_r, b_r, w_i, b_i, lam, w_br,
           w_out, ple_norm_g, w_pg, w_pp, final_norm_g):
    bsz, seq, d = x.shape
    depth = w_in.shape[0]
    assert d == D_MODEL and w_in.shape[2] == IN_WIDTH
    assert seq % MOBA_BLOCK == 0 and seq % LRU_CHUNK == 0 and (bsz * seq) % PROJ_TM == 0
    t = bsz * seq
    xf = x.reshape(t, d)
    table = rpe_table.astype(F32).T
    bias_a = _bias_lookup(table[:A_HEADS], _moba_bucket_planes(), 1)
    bias_b = _bias_lookup(table[A_HEADS:], _swa_bucket_planes(), B_GROUP)
    for i in range(depth):
        proj = _in_proj(xf, norm_g[i], w_in[i].astype(BF16))
        ya = _moba_branch(proj, bias_a, bsz, seq)
        yb = _swa_branch(proj, bias_b, _swa_sink_rows(sinks[i]), bsz, seq)
        yc = _rglru_branch(proj, conv_w[i], conv_b[i], w_r[i], b_r[i], w_i[i], b_i[i], lam[i],
                           bsz, seq)
        x1 = _merge(ya, yb, yc, proj, w_br[i].astype(BF16), w_out[i].astype(BF16), xf)
        xf = _ple(x1, p[i].reshape(t, PLE_DIM), ple_norm_g[i], w_pg[i].astype(BF16),
                  w_pp[i].astype(BF16), final_norm_g, final=(i == depth - 1))
    return xf.reshape(bsz, seq, d)
```

```python
import functools
import math

import jax
import jax.numpy as jnp
import numpy as np
from jax import lax
from jax.experimental import pallas as pl
from jax.experimental.pallas import tpu as pltpu

F32 = jnp.float32
BF16 = jnp.bfloat16

D_MODEL = 2048
PLE_DIM = 256
BRANCH_WIDTH = 1024
A_HEADS = 8
A_HEAD_DIM = 128
MOBA_BLOCK = 256
MOBA_TOPK = 3
B_Q_HEADS = 16
B_KV_HEADS = 4
B_GROUP = B_Q_HEADS // B_KV_HEADS
B_HEAD_DIM = 64
WINDOW = 128
C_BLOCKS = 8
C_BLOCK_DIM = 128
CONV_WIDTH = 4
LRU_C = 8.0
RPE_BUCKETS = 32
RPE_MAX_DIST = 128
EPS = 1e-6
NEG = -1e30
LOG2E = math.log2(math.e)

COL_QA = 0
COL_KA = COL_QA + BRANCH_WIDTH
COL_VA = COL_KA + BRANCH_WIDTH
COL_GA = COL_VA + BRANCH_WIDTH
COL_QB = COL_GA + BRANCH_WIDTH
COL_KB = COL_QB + BRANCH_WIDTH
COL_VB = COL_KB + B_KV_HEADS * B_HEAD_DIM
COL_GB = COL_VB + B_KV_HEADS * B_HEAD_DIM
COL_XC = COL_GB + BRANCH_WIDTH
COL_GC = COL_XC + BRANCH_WIDTH
COL_MG = COL_GC + BRANCH_WIDTH
IN_WIDTH = COL_MG + 3 * D_MODEL

PROJ_TM = 1024
PROJ_TN = 512
MERGE_TM = 1024
MERGE_TN = 512
PLE_TM = 512
LRU_CHUNK = 256

VMEM_LIMIT = 56 * 1024 * 1024


def _params(sem):
    return pltpu.CompilerParams(dimension_semantics=sem, vmem_limit_bytes=VMEM_LIMIT)


def _rms(xf, g):
    return xf * lax.rsqrt(jnp.mean(xf * xf, axis=-1, keepdims=True) + EPS) * g


def _in_proj_kernel(x_ref, g_ref, w_ref, o_ref, h_sc):
    @pl.when(pl.program_id(1) == 0)
    def _():
        h_sc[...] = _rms(x_ref[...], g_ref[...]).astype(BF16)

    o_ref[...] = jnp.dot(h_sc[...], w_ref[...], preferred_element_type=F32)


def _in_proj(x2d, g, w_bf):
    t, d = x2d.shape
    n = w_bf.shape[1]
    return pl.pallas_call(
        _in_proj_kernel,
        grid=(t // PROJ_TM, n // PROJ_TN),
        in_specs=[
            pl.BlockSpec((PROJ_TM, d), lambda i, j: (i, 0)),
            pl.BlockSpec((1, d), lambda i, j: (0, 0)),
            pl.BlockSpec((d, PROJ_TN), lambda i, j: (0, j)),
        ],
        out_specs=pl.BlockSpec((PROJ_TM, PROJ_TN), lambda i, j: (i, j)),
        out_shape=jax.ShapeDtypeStruct((t, n), F32),
        scratch_shapes=[pltpu.VMEM((PROJ_TM, d), BF16)],
        compiler_params=_params(("parallel", "arbitrary")),
        name="in_proj",
    )(x2d, g.reshape(1, d), w_bf)


MASKED_BUCKET = RPE_BUCKETS


def _t5_bucket_np(dist):
    n = np.maximum(dist, 0)
    max_exact = RPE_BUCKETS // 2
    ratio = np.log(np.maximum(n, 1).astype(np.float32) / np.float32(max_exact)) / np.float32(
        math.log(RPE_MAX_DIST / max_exact))
    large = max_exact + (ratio * np.float32(RPE_BUCKETS - max_exact)).astype(np.int32)
    large = np.minimum(large, RPE_BUCKETS - 1)
    return np.where(n < max_exact, n, large).astype(np.int32)


def _moba_bucket_planes():
    L = MOBA_BLOCK
    ki = np.arange(L)[:, None]
    qi = np.arange(L)[None, :]
    d_own = qi - ki
    d_far = 2 * L + qi - ki
    assert d_far.min() >= RPE_MAX_DIST
    own = np.where(d_own >= 0, _t5_bucket_np(d_own), MASKED_BUCKET)
    return np.stack([own, _t5_bucket_np(L + qi - ki), _t5_bucket_np(d_far)]).astype(np.int32)


def _swa_bucket_planes():
    W = WINDOW
    kk = np.arange(2 * W)[:, None]
    qi = np.arange(W)[None, :]
    dist = qi - kk + W
    ok = (dist >= 0) & (dist < W)
    b = _t5_bucket_np(dist)
    return np.stack([np.where(ok, b, MASKED_BUCKET),
                     np.where(ok & (kk >= W), b, MASKED_BUCKET)]).astype(np.int32)


def _bias_lookup_kernel(tab_ref, idx_ref, o_ref):
    h = pl.program_id(0)
    idx = idx_ref[0]
    acc = jnp.full(idx.shape, NEG, F32)
    for b in range(RPE_BUCKETS):
        acc = jnp.where(idx == b, tab_ref[h * RPE_BUCKETS + b], acc)
    o_ref[0, 0] = acc


def _bias_lookup(table_hb, planes, heads_per_row):
    nh = table_hb.shape[0]
    npl, r, c = planes.shape
    return pl.pallas_call(
        _bias_lookup_kernel,
        grid=(nh, npl),
        in_specs=[
            pl.BlockSpec(memory_space=pltpu.SMEM),
            pl.BlockSpec((1, r, c), lambda h, p: (p, 0, 0)),
        ],
        out_specs=pl.BlockSpec((1, 1, r, c),
                               lambda h, p: (h // heads_per_row, p, 0, h % heads_per_row)),
        out_shape=jax.ShapeDtypeStruct((nh // heads_per_row, npl, r, c * heads_per_row), F32),
        compiler_params=_params(("parallel", "parallel")),
        name="bias_lookup",
    )(table_hb.reshape(-1), jnp.asarray(planes))


GATE_ROWS = 16


def _silu(x):
    return x * jax.nn.sigmoid(x)


MOBA_GROUP = 2
MOBA_HEADS = 2


def _moba_prep_kernel(q_ref, k_ref, v_ref, kaug_ref, vt_ref, qaug_ref, kmean_sc, *, nb):
    L = MOBA_BLOCK
    dh = A_HEAD_DIM
    G = MOBA_GROUP

    lanes = lax.broadcasted_iota(jnp.int32, (L, dh), 1)
    kmean_sc[...] = jnp.zeros_like(kmean_sc)
    for n in range(nb):
        rows = slice((n % G) * L, (n % G + 1) * L)
        kn = k_ref[n * L:(n + 1) * L, :]
        kaug_ref[0, n // G, rows, :dh] = kn.astype(BF16)
        kaug_ref[0, n // G, rows, dh:] = jnp.where(lanes == n, 1.0, 0.0).astype(BF16)
        kmean_sc[n:n + 1, :] = jnp.sum(kn, axis=0, keepdims=True) * (1.0 / L)
        vt_ref[0, n // G, :, rows] = v_ref[n * L:(n + 1) * L, :].T.astype(BF16)

    blk = lax.broadcasted_iota(jnp.int32, (GATE_ROWS, L), 0)
    blk_f = blk.astype(F32)
    for j in range(nb):
        qt = q_ref[j * L:(j + 1) * L, :].T
        gate = jnp.dot(kmean_sc[...], qt, precision=lax.Precision.HIGHEST,
                       preferred_element_type=F32)
        gm = jnp.where(blk < j, gate, NEG)
        picked = jnp.zeros((GATE_ROWS, L), F32)
        for _ in range(MOBA_TOPK):
            mx = jnp.max(gm, axis=0, keepdims=True)
            first = jnp.min(jnp.where(gm == mx, blk_f, float(GATE_ROWS)), axis=0, keepdims=True)
            hit = blk_f == first
            picked = jnp.where(hit, 1.0, picked)
            gm = jnp.where(hit, -jnp.inf, gm)
        allowed = ((picked > 0.0) & (blk < j)) | (blk == j)
        qaug_ref[0, j, :dh, :] = (qt * (dh ** -0.5 * LOG2E)).astype(BF16)
        qaug_ref[0, j, dh:dh + GATE_ROWS, :] = jnp.where(allowed, 0.0, NEG).astype(BF16)
        qaug_ref[0, j, dh + GATE_ROWS:, :] = jnp.zeros((dh - GATE_ROWS, L), BF16)


def _moba_prep(proj, bsz, seq):
    L = MOBA_BLOCK
    dh = A_HEAD_DIM
    G = MOBA_GROUP
    nb = seq // L
    nh = bsz * A_HEADS
    cq, ck, cv = (c // dh for c in (COL_QA, COL_KA, COL_VA))
    col = lambda c: pl.BlockSpec((seq, dh), lambda b, h: (b, c + h))
    shapes = [(nh, nb // G, G * L, 2 * dh), (nh, nb // G, dh, G * L), (nh, nb, 2 * dh, L)]
    return pl.pallas_call(
        functools.partial(_moba_prep_kernel, nb=nb),
        grid=(bsz, A_HEADS),
        in_specs=[col(cq), col(ck), col(cv)],
        out_specs=[pl.BlockSpec((1,) + s[1:], lambda b, h: (b * A_HEADS + h, 0, 0, 0))
                   for s in shapes],
        out_shape=[jax.ShapeDtypeStruct(s, BF16) for s in shapes],
        scratch_shapes=[pltpu.VMEM((GATE_ROWS, dh), F32)],
        compiler_params=_params(("parallel", "parallel")),
        name="moba_prep",
    )(proj, proj, proj)


def _moba_schedule(nb):
    G = MOBA_GROUP
    return [(j, i) for j in range(nb) for i in range((j + G) // G)]


def _moba_kernel(jt_ref, it_ref, kaug_ref, vt_ref, qaug_ref, g_ref, bias_ref, o_ref,
                 s_sc, p_sc, alpha_sc, lfin_sc, lout_sc, m_sc, l_sc, acc_sc, *, sched):
    L = MOBA_BLOCK
    dh = A_HEAD_DIM
    G = MOBA_GROUP
    nt = len(sched)
    heads = range(MOBA_HEADS)
    hcol = lambda u: slice(u * dh, (u + 1) * dh)
    last_group = lambda j: j // G

    def scores(u, t, slot):
        s_sc[u, slot] = jnp.dot(kaug_ref[u, it_ref[t]], qaug_ref[u, jt_ref[t]],
                                preferred_element_type=F32)

    def softmax(u, t, slot):
        j = jt_ref[t]
        i = it_ref[t]
        s = s_sc[u, slot]
        parts = []
        for g in range(G):
            n = i * G + g
            plane = jnp.where(n == j, 0, jnp.where(n == j - 1, 1, 2))
            parts.append(s[g * L:(g + 1) * L, :] + bias_ref[u, plane])
        m_old = jnp.where(i == 0, NEG, m_sc[u])
        m_new = m_old
        for sp in parts:
            m_new = jnp.maximum(m_new, jnp.max(sp, axis=0, keepdims=True))
        alpha = jnp.exp2(m_old - m_new)
        probs = [jnp.exp2(sp - m_new) for sp in parts]
        l_new = alpha * l_sc[u]
        for pr in probs:
            l_new = l_new + jnp.sum(pr, axis=0, keepdims=True)
        p_sc[u, slot] = jnp.concatenate(probs, axis=0).astype(BF16)
        alpha_sc[u, slot] = alpha
        lfin_sc[u, slot] = l_new
        l_sc[u] = l_new
        m_sc[u] = m_new

    def weighted_values(u, t, slot):
        acc_sc[u, slot] = alpha_sc[u, slot] * acc_sc[u, 1 - slot] + jnp.dot(
            vt_ref[u, it_ref[t]], p_sc[u, slot], preferred_element_type=F32)
        lout_sc[u, slot] = lfin_sc[u, slot]

    def finish(u, j, slot):
        out_t = acc_sc[u, slot] / lout_sc[u, slot]
        rows = pl.ds(pl.multiple_of(j * L, L), L)
        o_ref[rows, hcol(u)] = (out_t.T * _silu(g_ref[rows, hcol(u)])).astype(o_ref.dtype)

    s_sc[:, 1] = jnp.full(s_sc.shape[:1] + s_sc.shape[2:], -jnp.inf, F32)
    p_sc[:, 0] = jnp.zeros(p_sc.shape[:1] + p_sc.shape[2:], BF16)
    alpha_sc[...] = jnp.ones_like(alpha_sc)
    lfin_sc[...] = jnp.ones_like(lfin_sc)
    m_sc[...] = jnp.full_like(m_sc, NEG)
    l_sc[...] = jnp.zeros_like(l_sc)
    acc_sc[...] = jnp.zeros_like(acc_sc)

    def two_steps(k, carry):
        t0 = 2 * k
        t1 = t0 + 1
        tp0 = jnp.maximum(t0 - 2, 0)
        tp1 = jnp.maximum(t0 - 1, 0)
        for u in heads:
            weighted_values(u, tp0, 0)
            softmax(u, tp1, 1)
            scores(u, t0, 0)
        for u in heads:
            weighted_values(u, tp1, 1)
            softmax(u, t0, 0)
            scores(u, t1, 1)
        for tp, slot in ((tp0, 0), (tp1, 1)):
            j_done = jt_ref[tp]

            @pl.when((k >= 1) & (it_ref[tp] == last_group(j_done)))
            def _():
                for u in heads:
                    finish(u, j_done, slot)

        return carry

    assert nt % 2 == 0
    lax.fori_loop(0, nt // 2, two_steps, 0)
    for u in heads:
        weighted_values(u, nt - 2, 0)
        if sched[nt - 2][1] == last_group(sched[nt - 2][0]):
            finish(u, sched[nt - 2][0], 0)
        softmax(u, nt - 1, 1)
        weighted_values(u, nt - 1, 1)
        finish(u, sched[nt - 1][0], 1)


def _moba_branch(proj, bias, bsz, seq):
    L = MOBA_BLOCK
    dh = A_HEAD_DIM
    G = MOBA_GROUP
    nb = seq // L
    hp = MOBA_HEADS
    assert min(MOBA_TOPK, nb - 1) == MOBA_TOPK and nb <= GATE_ROWS and nb % G == 0
    assert A_HEADS % hp == 0
    kaug, vt, qaug = _moba_prep(proj, bsz, seq)
    sched = _moba_schedule(nb)
    jt = jnp.asarray(np.array([j for j, _ in sched], np.int32))
    it = jnp.asarray(np.array([i for _, i in sched], np.int32))
    pairs = A_HEADS // hp
    cg = COL_GA // (hp * dh)
    smem = pl.BlockSpec(memory_space=pltpu.SMEM)
    per_pair = lambda a: pl.BlockSpec((hp,) + a.shape[1:], lambda b, h: (b * pairs + h, 0, 0, 0))
    return pl.pallas_call(
        functools.partial(_moba_kernel, sched=sched),
        grid=(bsz, pairs),
        in_specs=[smem, smem, per_pair(kaug), per_pair(vt), per_pair(qaug),
                  pl.BlockSpec((seq, hp * dh), lambda b, h: (b, cg + h)),
                  pl.BlockSpec((hp, 3, L, L), lambda b, h: (h, 0, 0, 0))],
        out_specs=pl.BlockSpec((seq, hp * dh), lambda b, h: (b, h)),
        out_shape=jax.ShapeDtypeStruct((bsz * seq, BRANCH_WIDTH), BF16),
        scratch_shapes=[
            pltpu.VMEM((hp, 2, G * L, L), F32),
            pltpu.VMEM((hp, 2, G * L, L), BF16),
            pltpu.VMEM((hp, 2, 1, L), F32),
            pltpu.VMEM((hp, 2, 1, L), F32),
            pltpu.VMEM((hp, 2, 1, L), F32),
            pltpu.VMEM((hp, 1, L), F32),
            pltpu.VMEM((hp, 1, L), F32),
            pltpu.VMEM((hp, 2, dh, L), F32),
        ],
        compiler_params=_params(("parallel", "parallel")),
        name="moba",
    )(jt, it, kaug, vt, qaug, proj, bias)


def _swa_kernel(q_ref, kp_ref, kc_ref, vp_ref, vc_ref, g0_ref, g1_ref, bias_ref, sink_ref, o_ref,
                ot_sc):
    W = WINDOW
    dh = B_HEAD_DIM
    kvw = B_KV_HEADS * dh
    plane = jnp.where(pl.program_id(1) == 0, 1, 0)
    qt = q_ref[...].T * (dh ** -0.5 * LOG2E)
    head = lambda g, h: slice((g * B_GROUP + h) * dh, (g * B_GROUP + h + 1) * dh)
    x = jnp.concatenate(
        [jnp.concatenate([qt[head(g, h), :] for g in range(B_KV_HEADS)], axis=0)
         for h in range(B_GROUP)], axis=1).astype(BF16)
    kband = jnp.concatenate([kp_ref[...], kc_ref[...]], axis=0)
    vband_t = jnp.concatenate([vp_ref[...], vc_ref[...]], axis=0).T.astype(BF16)
    lane_group = lax.broadcasted_iota(jnp.int32, (2 * W, kvw), 1) // dh
    for g in range(B_KV_HEADS):
        kg = jnp.where(lane_group == g, kband, 0.0).astype(BF16)
        s = jnp.dot(kg, x, preferred_element_type=F32) + bias_ref[g, plane]
        sink = sink_ref[g]
        m = jnp.maximum(jnp.max(s, axis=0, keepdims=True), sink)
        p = jnp.exp2(s - m)
        denom = jnp.sum(p, axis=0, keepdims=True) + jnp.exp2(sink - m)
        o = jnp.dot(vband_t[g * dh:(g + 1) * dh, :], p.astype(BF16),
                    preferred_element_type=F32) / denom
        for h in range(B_GROUP):
            ot_sc[head(g, h), :] = o[:, h * W:(h + 1) * W]
    gate = jnp.concatenate([g0_ref[...], g1_ref[...]], axis=1)
    o_ref[...] = (ot_sc[...].T * _silu(gate)).astype(o_ref.dtype)


def _swa_branch(proj, bias, sink, bsz, seq):
    W = WINDOW
    nb = seq // W
    kvw = B_KV_HEADS * B_HEAD_DIM
    cq = COL_QB // BRANCH_WIDTH
    ck = COL_KB // kvw
    cv = COL_VB // kvw
    cg = COL_GB // (BRANCH_WIDTH // 2)
    assert COL_QB % BRANCH_WIDTH == 0 and COL_KB % kvw == 0 and COL_VB % kvw == 0
    assert COL_GB % (BRANCH_WIDTH // 2) == 0
    prev = lambda b, n: (b * nb + jnp.maximum(n - 1, 0))
    return pl.pallas_call(
        _swa_kernel,
        grid=(bsz, nb),
        in_specs=[
            pl.BlockSpec((W, BRANCH_WIDTH), lambda b, n: (b * nb + n, cq)),
            pl.BlockSpec((W, kvw), lambda b, n: (prev(b, n), ck)),
            pl.BlockSpec((W, kvw), lambda b, n: (b * nb + n, ck)),
            pl.BlockSpec((W, kvw), lambda b, n: (prev(b, n), cv)),
            pl.BlockSpec((W, kvw), lambda b, n: (b * nb + n, cv)),
            pl.BlockSpec((W, BRANCH_WIDTH // 2), lambda b, n: (b * nb + n, cg)),
            pl.BlockSpec((W, BRANCH_WIDTH // 2), lambda b, n: (b * nb + n, cg + 1)),
            pl.BlockSpec(bias.shape, lambda b, n: (0, 0, 0, 0)),
            pl.BlockSpec(sink.shape, lambda b, n: (0, 0, 0)),
        ],
        out_specs=pl.BlockSpec((W, BRANCH_WIDTH), lambda b, n: (b * nb + n, 0)),
        out_shape=jax.ShapeDtypeStruct((bsz * seq, BRANCH_WIDTH), BF16),
        scratch_shapes=[pltpu.VMEM((BRANCH_WIDTH, W), F32)],
        compiler_params=_params(("parallel", "arbitrary")),
        name="swa",
    )(proj, proj, proj, proj, proj, proj, proj, bias, sink)


def _swa_sink_rows(sinks):
    s = (sinks.astype(F32) * LOG2E).reshape(B_KV_HEADS, 1, B_GROUP, 1)
    return jnp.broadcast_to(s, (B_KV_HEADS, 1, B_GROUP, WINDOW)).reshape(
        B_KV_HEADS, 1, B_GROUP * WINDOW)


def _rglru_kernel(xc_ref, gc_ref, cw_ref, cb_ref, wr_ref, br_ref, wi_ref, bi_ref, lam_ref, o_ref,
                  xpad_sc, h_sc, *, seq):
    T = LRU_CHUNK
    PAD = 8
    xpad_sc[:PAD, :] = jnp.zeros((PAD, C_BLOCK_DIM), F32)
    xpad_sc[PAD:, :] = xc_ref[...]
    h_sc[...] = jnp.zeros_like(h_sc)
    nlam = -lam_ref[...]
    softplus = jnp.maximum(nlam, 0.0) + jnp.log1p(jnp.exp(-jnp.abs(nlam)))
    decay = -LRU_C * softplus
    row = lax.broadcasted_iota(jnp.int32, (T, C_BLOCK_DIM), 0)

    def chunk(c, carry):
        t0 = pl.multiple_of(c * T, T)
        xw = xpad_sc[pl.ds(t0, T + PAD), :]
        conv = cb_ref[...]
        for w in range(CONV_WIDTH):
            off = PAD - (CONV_WIDTH - 1) + w
            conv = conv + cw_ref[w:w + 1, :] * xw[off:off + T, :]
        cbf = conv.astype(BF16)
        r = jax.nn.sigmoid(jnp.dot(cbf, wr_ref[0], preferred_element_type=F32) + br_ref[...])
        i = jax.nn.sigmoid(jnp.dot(cbf, wi_ref[0], preferred_element_type=F32) + bi_ref[...])
        log_a = r * decay
        a = jnp.exp(log_a)
        b = jnp.sqrt(jnp.maximum(-jnp.tanh(log_a) * (a * a + 1.0), 0.0)) * (i * conv)
        k = 1
        while k < T:
            keep = row >= k
            a_prev = jnp.where(keep, pltpu.roll(a, k, 0), 1.0)
            b_prev = jnp.where(keep, pltpu.roll(b, k, 0), 0.0)
            b = a * b_prev + b
            a = a * a_prev
            k *= 2
        h = a * h_sc[...] + b
        h_sc[...] = h[T - 1:T, :]
        o_ref[pl.ds(t0, T), :] = (h * _silu(gc_ref[pl.ds(t0, T), :])).astype(o_ref.dtype)
        return carry

    lax.fori_loop(0, seq // T, chunk, 0)


def _rglru_branch(proj, conv_w, conv_b, w_r, b_r, w_i, b_i, lam, bsz, seq):
    cd = C_BLOCK_DIM
    cx = COL_XC // cd
    cg = COL_GC // cd
    vec = lambda v: v.reshape(1, BRANCH_WIDTH)
    vspec = pl.BlockSpec((1, cd), lambda b, c: (0, c))
    wspec = pl.BlockSpec((1, cd, cd), lambda b, c: (c, 0, 0))
    return pl.pallas_call(
        functools.partial(_rglru_kernel, seq=seq),
        grid=(bsz, C_BLOCKS),
        in_specs=[
            pl.BlockSpec((seq, cd), lambda b, c: (b, cx + c)),
            pl.BlockSpec((seq, cd), lambda b, c: (b, cg + c)),
            pl.BlockSpec((CONV_WIDTH, cd), lambda b, c: (0, c)),
            vspec, wspec, vspec, wspec, vspec, vspec,
        ],
        out_specs=pl.BlockSpec((seq, cd), lambda b, c: (b, c)),
        out_shape=jax.ShapeDtypeStruct((bsz * seq, BRANCH_WIDTH), BF16),
        scratch_shapes=[pltpu.VMEM((seq + 8, cd), F32), pltpu.VMEM((1, cd), F32)],
        compiler_params=_params(("parallel", "parallel")),
        name="rglru",
    )(proj, proj, conv_w, vec(conv_b), w_r.astype(BF16), vec(b_r), w_i.astype(BF16), vec(b_i),
      vec(lam))


def _merge_kernel(ya_ref, yb_ref, yc_ref, ma_ref, mb_ref, mc_ref, wbr_ref, o_ref):
    merged = None
    for n, (y_ref, m_ref) in enumerate(((ya_ref, ma_ref), (yb_ref, mb_ref), (yc_ref, mc_ref))):
        term = jax.nn.sigmoid(m_ref[...]) * jnp.dot(y_ref[...], wbr_ref[n],
                                                    preferred_element_type=F32)
        merged = term if merged is None else merged + term
    o_ref[...] = merged.astype(o_ref.dtype)


def _merge(ya, yb, yc, proj, wbr_bf):
    t = ya.shape[0]
    d = wbr_bf.shape[2]
    tm, tn = MERGE_TM, MERGE_TN
    assert COL_MG % tn == 0 and d % tn == 0
    cm = COL_MG // tn
    per = d // tn
    yspec = pl.BlockSpec((tm, BRANCH_WIDTH), lambda i, c: (i, 0))
    mspec = lambda n: pl.BlockSpec((tm, tn), lambda i, c: (i, cm + n * per + c))
    return pl.pallas_call(
        _merge_kernel,
        grid=(t // tm, per),
        in_specs=[
            yspec, yspec, yspec, mspec(0), mspec(1), mspec(2),
            pl.BlockSpec((3, BRANCH_WIDTH, tn), lambda i, c: (0, 0, c)),
        ],
        out_specs=pl.BlockSpec((tm, tn), lambda i, c: (i, c)),
        out_shape=jax.ShapeDtypeStruct((t, d), BF16),
        compiler_params=_params(("parallel", "arbitrary")),
        name="merge",
    )(ya, yb, yc, proj, proj, proj, wbr_bf)


def _out_ple_kernel(mg_ref, x_ref, p_ref, wout_ref, g_ref, wpg_ref, wpp_ref, fg_ref, o_ref, *,
                    final):
    x = x_ref[...] + jnp.dot(mg_ref[...], wout_ref[...], preferred_element_type=F32)
    h = _rms(x, g_ref[...]).astype(BF16)
    gate = jax.nn.sigmoid(jnp.dot(h, wpg_ref[...], preferred_element_type=F32))
    emb = jnp.dot(p_ref[...].astype(BF16), wpp_ref[...], preferred_element_type=F32)
    y = x + gate * emb
    if final:
        y = _rms(y, fg_ref[...])
    o_ref[...] = y


def _out_ple(merged, x2d, p2d, wout_bf, g, wpg_bf, wpp_bf, final_g, final):
    t, d = x2d.shape
    tm = PLE_TM
    gspec = pl.BlockSpec((1, d), lambda i: (0, 0))
    wspec = lambda r: pl.BlockSpec((r, d), lambda i: (0, 0), pipeline_mode=pl.Buffered(1))
    row = lambda w: pl.BlockSpec((tm, w), lambda i: (i, 0))
    return pl.pallas_call(
        functools.partial(_out_ple_kernel, final=final),
        grid=(t // tm,),
        in_specs=[row(d), row(d), row(PLE_DIM), wspec(d), gspec, wspec(d), wspec(PLE_DIM), gspec],
        out_specs=row(d),
        out_shape=jax.ShapeDtypeStruct((t, d), F32),
        compiler_params=_params(("parallel",)),
        name="out_ple",
    )(merged, x2d, p2d, wout_bf, g.reshape(1, d), wpg_bf, wpp_bf, final_g.reshape(1, d))


def kernel(x, p, rpe_table, norm_g, w_in, sinks, conv_w, conv_b, w_r, b_r, w_i, b_i, lam, w_br,
           w_out, ple_norm_g, w_pg, w_pp, final_norm_g):
    bsz, seq, d = x.shape
    depth = w_in.shape[0]
    assert d == D_MODEL and w_in.shape[2] == IN_WIDTH
    assert seq % MOBA_BLOCK == 0 and seq % LRU_CHUNK == 0 and (bsz * seq) % PROJ_TM == 0
    t = bsz * seq
    xf = x.reshape(t, d)
    table = rpe_table.astype(F32).T * LOG2E
    bias_a = _bias_lookup(table[:A_HEADS], _moba_bucket_planes(), 1)
    bias_b = _bias_lookup(table[A_HEADS:], _swa_bucket_planes(), B_GROUP)
    for i in range(depth):
        proj = _in_proj(xf, norm_g[i], w_in[i].astype(BF16))
        ya = _moba_branch(proj, bias_a, bsz, seq)
        yb = _swa_branch(proj, bias_b, _swa_sink_rows(sinks[i]), bsz, seq)
        yc = _rglru_branch(proj, conv_w[i], conv_b[i], w_r[i], b_r[i], w_i[i], b_i[i], lam[i],
                           bsz, seq)
        merged = _merge(ya, yb, yc, proj, w_br[i].astype(BF16))
        xf = _out_ple(merged, xf, p[i].reshape(t, PLE_DIM), w_out[i].astype(BF16), ple_norm_g[i],
                      w_pg[i].astype(BF16), w_pp[i].astype(BF16), final_norm_g,
                      final=(i == depth - 1))
    return xf.reshape(bsz, seq, d)
```

```python
import functools
import math

import jax
import jax.numpy as jnp
import numpy as np
from jax import lax
from jax.experimental import pallas as pl
from jax.experimental.pallas import tpu as pltpu

F32 = jnp.float32
BF16 = jnp.bfloat16

D_MODEL = 2048
PLE_DIM = 256
BRANCH_WIDTH = 1024
A_HEADS = 8
A_HEAD_DIM = 128
MOBA_BLOCK = 256
MOBA_TOPK = 3
B_Q_HEADS = 16
B_KV_HEADS = 4
B_GROUP = B_Q_HEADS // B_KV_HEADS
B_HEAD_DIM = 64
WINDOW = 128
C_BLOCKS = 8
C_BLOCK_DIM = 128
CONV_WIDTH = 4
LRU_C = 8.0
RPE_BUCKETS = 32
RPE_MAX_DIST = 128
EPS = 1e-6
NEG = -1e30
LOG2E = math.log2(math.e)

COL_QA = 0
COL_KA = COL_QA + BRANCH_WIDTH
COL_VA = COL_KA + BRANCH_WIDTH
COL_GA = COL_VA + BRANCH_WIDTH
COL_QB = COL_GA + BRANCH_WIDTH
COL_KB = COL_QB + BRANCH_WIDTH
COL_VB = COL_KB + B_KV_HEADS * B_HEAD_DIM
COL_GB = COL_VB + B_KV_HEADS * B_HEAD_DIM
COL_XC = COL_GB + BRANCH_WIDTH
COL_GC = COL_XC + BRANCH_WIDTH
COL_MG = COL_GC + BRANCH_WIDTH
IN_WIDTH = COL_MG + 3 * D_MODEL

PROJ_TM = 2048
PROJ_TN = 512
MERGE_TM = 1024
MERGE_TN = 512
PLE_TM = 512
LRU_CHUNK = 256

VMEM_LIMIT = 60 * 1024 * 1024


def _params(sem):
    return pltpu.CompilerParams(dimension_semantics=sem, vmem_limit_bytes=VMEM_LIMIT)


def _rms(xf, g):
    return xf * lax.rsqrt(jnp.mean(xf * xf, axis=-1, keepdims=True) + EPS) * g


def _in_proj_kernel(x_ref, g_ref, w_ref, o_ref, h_sc):
    @pl.when(pl.program_id(1) == 0)
    def _():
        h_sc[...] = _rms(x_ref[...], g_ref[...]).astype(BF16)

    o_ref[...] = jnp.dot(h_sc[...], w_ref[...], preferred_element_type=F32)


def _in_proj(x2d, g, w_bf, layer):
    t, d = x2d.shape
    n = w_bf.shape[2]
    return pl.pallas_call(
        _in_proj_kernel,
        grid=(t // PROJ_TM, n // PROJ_TN),
        in_specs=[
            pl.BlockSpec((PROJ_TM, d), lambda i, j: (i, 0)),
            pl.BlockSpec((None, 1, d), lambda i, j: (layer, 0, 0)),
            pl.BlockSpec((None, d, PROJ_TN), lambda i, j: (layer, 0, j)),
        ],
        out_specs=pl.BlockSpec((PROJ_TM, PROJ_TN), lambda i, j: (i, j)),
        out_shape=jax.ShapeDtypeStruct((t, n), F32),
        scratch_shapes=[pltpu.VMEM((PROJ_TM, d), BF16)],
        compiler_params=_params(("parallel", "arbitrary")),
        name="in_proj",
    )(x2d, g, w_bf)


MASKED_BUCKET = RPE_BUCKETS


def _t5_bucket_np(dist):
    n = np.maximum(dist, 0)
    max_exact = RPE_BUCKETS // 2
    ratio = np.log(np.maximum(n, 1).astype(np.float32) / np.float32(max_exact)) / np.float32(
        math.log(RPE_MAX_DIST / max_exact))
    large = max_exact + (ratio * np.float32(RPE_BUCKETS - max_exact)).astype(np.int32)
    large = np.minimum(large, RPE_BUCKETS - 1)
    return np.where(n < max_exact, n, large).astype(np.int32)


def _moba_bucket_planes():
    L = MOBA_BLOCK
    ki = np.arange(L)[:, None]
    qi = np.arange(L)[None, :]
    d_own = qi - ki
    d_far = 2 * L + qi - ki
    assert d_far.min() >= RPE_MAX_DIST
    own = np.where(d_own >= 0, _t5_bucket_np(d_own), MASKED_BUCKET)
    return np.stack([own, _t5_bucket_np(L + qi - ki), _t5_bucket_np(d_far)]).astype(np.int32)


def _swa_bucket_planes():
    W = WINDOW
    kk = np.arange(2 * W)[:, None]
    qi = np.arange(W)[None, :]
    dist = qi - kk + W
    ok = (dist >= 0) & (dist < W)
    b = _t5_bucket_np(dist)
    return np.stack([np.where(ok, b, MASKED_BUCKET),
                     np.where(ok & (kk >= W), b, MASKED_BUCKET)]).astype(np.int32)


def _bias_lookup_kernel(tab_ref, idx_ref, o_ref):
    h = pl.program_id(0)
    idx = idx_ref[0]
    acc = jnp.full(idx.shape, NEG, F32)
    for b in range(RPE_BUCKETS):
        acc = jnp.where(idx == b, tab_ref[h * RPE_BUCKETS + b], acc)
    o_ref[0, 0] = acc


def _bias_lookup(table_hb, planes, heads_per_row):
    nh = table_hb.shape[0]
    npl, r, c = planes.shape
    return pl.pallas_call(
        _bias_lookup_kernel,
        grid=(nh, npl),
        in_specs=[
            pl.BlockSpec(memory_space=pltpu.SMEM),
            pl.BlockSpec((1, r, c), lambda h, p: (p, 0, 0)),
        ],
        out_specs=pl.BlockSpec((1, 1, r, c),
                               lambda h, p: (h // heads_per_row, p, 0, h % heads_per_row)),
        out_shape=jax.ShapeDtypeStruct((nh // heads_per_row, npl, r, c * heads_per_row), F32),
        compiler_params=_params(("parallel", "parallel")),
        name="bias_lookup",
    )(table_hb.reshape(-1), jnp.asarray(planes))


GATE_ROWS = 16


def _silu(x):
    return x * jax.nn.sigmoid(x)


MOBA_GROUP = 2
MOBA_UNROLL = 2
MOBA_HEADS = 2


def _moba_prep_kernel(q_ref, k_ref, v_ref, kaug_ref, vt_ref, qaug_ref, kmean_sc, *, nb):
    L = MOBA_BLOCK
    dh = A_HEAD_DIM
    G = MOBA_GROUP

    lanes = lax.broadcasted_iota(jnp.int32, (L, dh), 1)
    kmean_sc[...] = jnp.zeros_like(kmean_sc)
    for n in range(nb):
        rows = slice((n % G) * L, (n % G + 1) * L)
        kn = k_ref[n * L:(n + 1) * L, :]
        kaug_ref[0, n // G, rows, :dh] = kn.astype(BF16)
        kaug_ref[0, n // G, rows, dh:] = jnp.where(lanes == n, 1.0, 0.0).astype(BF16)
        kmean_sc[n:n + 1, :] = jnp.sum(kn, axis=0, keepdims=True) * (1.0 / L)
        vt_ref[0, n // G, :, rows] = v_ref[n * L:(n + 1) * L, :].T.astype(BF16)

    blk = lax.broadcasted_iota(jnp.int32, (GATE_ROWS, L), 0)
    blk_f = blk.astype(F32)
    for j in range(nb):
        qt = q_ref[j * L:(j + 1) * L, :].T
        gate = jnp.dot(kmean_sc[...], qt, precision=lax.Precision.HIGHEST,
                       preferred_element_type=F32)
        gm = jnp.where(blk < j, gate, NEG)
        picked = jnp.zeros((GATE_ROWS, L), F32)
        for _ in range(MOBA_TOPK):
            mx = jnp.max(gm, axis=0, keepdims=True)
            first = jnp.min(jnp.where(gm == mx, blk_f, float(GATE_ROWS)), axis=0, keepdims=True)
            hit = blk_f == first
            picked = jnp.where(hit, 1.0, picked)
            gm = jnp.where(hit, -jnp.inf, gm)
        allowed = ((picked > 0.0) & (blk < j)) | (blk == j)
        qaug_ref[0, j, :dh, :] = (qt * (dh ** -0.5 * LOG2E)).astype(BF16)
        qaug_ref[0, j, dh:dh + GATE_ROWS, :] = jnp.where(allowed, 0.0, NEG).astype(BF16)
        qaug_ref[0, j, dh + GATE_ROWS:, :] = jnp.zeros((dh - GATE_ROWS, L), BF16)


def _moba_prep(proj, bsz, seq):
    L = MOBA_BLOCK
    dh = A_HEAD_DIM
    G = MOBA_GROUP
    nb = seq // L
    nh = bsz * A_HEADS
    cq, ck, cv = (c // dh for c in (COL_QA, COL_KA, COL_VA))
    col = lambda c: pl.BlockSpec((seq, dh), lambda b, h: (b, c + h))
    shapes = [(nh, nb // G, G * L, 2 * dh), (nh, nb // G, dh, G * L), (nh, nb, 2 * dh, L)]
    return pl.pallas_call(
        functools.partial(_moba_prep_kernel, nb=nb),
        grid=(bsz, A_HEADS),
        in_specs=[col(cq), col(ck), col(cv)],
        out_specs=[pl.BlockSpec((1,) + s[1:], lambda b, h: (b * A_HEADS + h, 0, 0, 0))
                   for s in shapes],
        out_shape=[jax.ShapeDtypeStruct(s, BF16) for s in shapes],
        scratch_shapes=[pltpu.VMEM((GATE_ROWS, dh), F32)],
        compiler_params=_params(("parallel", "parallel")),
        name="moba_prep",
    )(proj, proj, proj)


def _moba_schedule(nb):
    G = MOBA_GROUP
    return [(j, i) for j in range(nb) for i in range((j + G) // G)]


def _moba_kernel(jt_ref, it_ref, kaug_ref, vt_ref, qaug_ref, g_ref, bias_ref, o_ref,
                 s_sc, p_sc, alpha_sc, lfin_sc, lout_sc, m_sc, l_sc, acc_sc, *, sched):
    L = MOBA_BLOCK
    dh = A_HEAD_DIM
    G = MOBA_GROUP
    U = MOBA_UNROLL
    nt = len(sched)
    heads = range(MOBA_HEADS)
    hcol = lambda u: slice(u * dh, (u + 1) * dh)
    last_group = lambda j: j // G

    def scores(u, t, slot):
        s_sc[u, slot] = jnp.dot(kaug_ref[u, it_ref[t]], qaug_ref[u, jt_ref[t]],
                                preferred_element_type=F32)

    def softmax(u, t, slot):
        j = jt_ref[t]
        i = it_ref[t]
        s = s_sc[u, slot]
        parts = []
        for g in range(G):
            n = i * G + g
            plane = jnp.where(n == j, 0, jnp.where(n == j - 1, 1, 2))
            parts.append(s[g * L:(g + 1) * L, :] + bias_ref[u, plane])
        m_old = jnp.where(i == 0, NEG, m_sc[u])
        m_new = m_old
        for sp in parts:
            m_new = jnp.maximum(m_new, jnp.max(sp, axis=0, keepdims=True))
        alpha = jnp.exp2(m_old - m_new)
        probs = [jnp.exp2(sp - m_new) for sp in parts]
        l_new = alpha * l_sc[u]
        for pr in probs:
            l_new = l_new + jnp.sum(pr, axis=0, keepdims=True)
        p_sc[u, slot] = jnp.concatenate(probs, axis=0).astype(BF16)
        alpha_sc[u, slot] = alpha
        lfin_sc[u, slot] = l_new
        l_sc[u] = l_new
        m_sc[u] = m_new

    def weighted_values(u, t, slot, r):
        acc_sc[u, r] = alpha_sc[u, slot] * acc_sc[u, (r - 1) % U] + jnp.dot(
            vt_ref[u, it_ref[t]], p_sc[u, slot], preferred_element_type=F32)
        lout_sc[u, r] = lfin_sc[u, slot]

    def finish(u, j, r):
        out_t = acc_sc[u, r] / lout_sc[u, r]
        rows = pl.ds(pl.multiple_of(j * L, L), L)
        o_ref[rows, hcol(u)] = (out_t.T * _silu(g_ref[rows, hcol(u)])).astype(o_ref.dtype)

    s_sc[:, 1] = jnp.full(s_sc.shape[:1] + s_sc.shape[2:], -jnp.inf, F32)
    p_sc[:, 0] = jnp.zeros(p_sc.shape[:1] + p_sc.shape[2:], BF16)
    alpha_sc[...] = jnp.ones_like(alpha_sc)
    lfin_sc[...] = jnp.ones_like(lfin_sc)
    m_sc[...] = jnp.full_like(m_sc, NEG)
    l_sc[...] = jnp.zeros_like(l_sc)
    acc_sc[...] = jnp.zeros_like(acc_sc)

    def unrolled_steps(k, carry):
        base = U * k
        done = []
        for r in range(U):
            t = base + r
            t_pv = jnp.maximum(t - 2, 0)
            t_sm = jnp.maximum(t - 1, 0)
            for u in heads:
                weighted_values(u, t_pv, r % 2, r)
                softmax(u, t_sm, (r + 1) % 2)
                scores(u, t, r % 2)
            done.append((t - 2, t_pv, r))
        for t_real, t_pv, r in done:
            j_done = jt_ref[t_pv]

            @pl.when((t_real >= 0) & (it_ref[t_pv] == last_group(j_done)))
            def _():
                for u in heads:
                    finish(u, j_done, r)

        return carry

    assert U % 2 == 0
    n_loop = nt // U
    lax.fori_loop(0, n_loop, unrolled_steps, 0)
    for t in range(n_loop * U, nt + 2):
        r = t % U
        for u in heads:
            if 0 <= t - 2:
                weighted_values(u, t - 2, r % 2, r)
                j, i = sched[t - 2]
                if i == last_group(j):
                    finish(u, j, r)
            if 0 <= t - 1 < nt:
                softmax(u, t - 1, (r + 1) % 2)
            if t < nt:
                scores(u, t, r % 2)


def _moba_branch(proj, bias, bsz, seq):
    L = MOBA_BLOCK
    dh = A_HEAD_DIM
    G = MOBA_GROUP
    nb = seq // L
    hp = MOBA_HEADS
    assert min(MOBA_TOPK, nb - 1) == MOBA_TOPK and nb <= GATE_ROWS and nb % G == 0
    assert A_HEADS % hp == 0
    kaug, vt, qaug = _moba_prep(proj, bsz, seq)
    sched = _moba_schedule(nb)
    jt = jnp.asarray(np.array([j for j, _ in sched], np.int32))
    it = jnp.asarray(np.array([i for _, i in sched], np.int32))
    pairs = A_HEADS // hp
    cg = COL_GA // (hp * dh)
    smem = pl.BlockSpec(memory_space=pltpu.SMEM)
    per_pair = lambda a: pl.BlockSpec((hp,) + a.shape[1:], lambda b, h: (b * pairs + h, 0, 0, 0))
    return pl.pallas_call(
        functools.partial(_moba_kernel, sched=sched),
        grid=(bsz, pairs),
        in_specs=[smem, smem, per_pair(kaug), per_pair(vt), per_pair(qaug),
                  pl.BlockSpec((seq, hp * dh), lambda b, h: (b, cg + h)),
                  pl.BlockSpec((hp, 3, L, L), lambda b, h: (h, 0, 0, 0))],
        out_specs=pl.BlockSpec((seq, hp * dh), lambda b, h: (b, h)),
        out_shape=jax.ShapeDtypeStruct((bsz * seq, BRANCH_WIDTH), BF16),
        scratch_shapes=[
            pltpu.VMEM((hp, 2, G * L, L), F32),
            pltpu.VMEM((hp, 2, G * L, L), BF16),
            pltpu.VMEM((hp, 2, 1, L), F32),
            pltpu.VMEM((hp, 2, 1, L), F32),
            pltpu.VMEM((hp, MOBA_UNROLL, 1, L), F32),
            pltpu.VMEM((hp, 1, L), F32),
            pltpu.VMEM((hp, 1, L), F32),
            pltpu.VMEM((hp, MOBA_UNROLL, dh, L), F32),
        ],
        compiler_params=_params(("parallel", "parallel")),
        name="moba",
    )(jt, it, kaug, vt, qaug, proj, bias)


def _swa_kernel(q_ref, kp_ref, kc_ref, vp_ref, vc_ref, g0_ref, g1_ref, bias_ref, sink_ref, o_ref,
                ot_sc):
    W = WINDOW
    dh = B_HEAD_DIM
    kvw = B_KV_HEADS * dh
    plane = jnp.where(pl.program_id(1) == 0, 1, 0)
    qt = q_ref[...].T * (dh ** -0.5 * LOG2E)
    head = lambda g, h: slice((g * B_GROUP + h) * dh, (g * B_GROUP + h + 1) * dh)
    x = jnp.concatenate(
        [jnp.concatenate([qt[head(g, h), :] for g in range(B_KV_HEADS)], axis=0)
         for h in range(B_GROUP)], axis=1).astype(BF16)
    kband = jnp.concatenate([kp_ref[...], kc_ref[...]], axis=0)
    vband_t = jnp.concatenate([vp_ref[...], vc_ref[...]], axis=0).T.astype(BF16)
    lane_group = lax.broadcasted_iota(jnp.int32, (2 * W, kvw), 1) // dh
    for g in range(B_KV_HEADS):
        kg = jnp.where(lane_group == g, kband, 0.0).astype(BF16)
        s = jnp.dot(kg, x, preferred_element_type=F32) + bias_ref[g, plane]
        sink = sink_ref[g]
        m = jnp.maximum(jnp.max(s, axis=0, keepdims=True), sink)
        p = jnp.exp2(s - m)
        denom = jnp.sum(p, axis=0, keepdims=True) + jnp.exp2(sink - m)
        o = jnp.dot(vband_t[g * dh:(g + 1) * dh, :], p.astype(BF16),
                    preferred_element_type=F32) / denom
        for h in range(B_GROUP):
            ot_sc[head(g, h), :] = o[:, h * W:(h + 1) * W]
    gate = jnp.concatenate([g0_ref[...], g1_ref[...]], axis=1)
    o_ref[...] = (ot_sc[...].T * _silu(gate)).astype(o_ref.dtype)


def _swa_branch(proj, bias, sink, bsz, seq):
    W = WINDOW
    nb = seq // W
    kvw = B_KV_HEADS * B_HEAD_DIM
    cq = COL_QB // BRANCH_WIDTH
    ck = COL_KB // kvw
    cv = COL_VB // kvw
    cg = COL_GB // (BRANCH_WIDTH // 2)
    assert COL_QB % BRANCH_WIDTH == 0 and COL_KB % kvw == 0 and COL_VB % kvw == 0
    assert COL_GB % (BRANCH_WIDTH // 2) == 0
    prev = lambda b, n: (b * nb + jnp.maximum(n - 1, 0))
    return pl.pallas_call(
        _swa_kernel,
        grid=(bsz, nb),
        in_specs=[
            pl.BlockSpec((W, BRANCH_WIDTH), lambda b, n: (b * nb + n, cq)),
            pl.BlockSpec((W, kvw), lambda b, n: (prev(b, n), ck)),
            pl.BlockSpec((W, kvw), lambda b, n: (b * nb + n, ck)),
            pl.BlockSpec((W, kvw), lambda b, n: (prev(b, n), cv)),
            pl.BlockSpec((W, kvw), lambda b, n: (b * nb + n, cv)),
            pl.BlockSpec((W, BRANCH_WIDTH // 2), lambda b, n: (b * nb + n, cg)),
            pl.BlockSpec((W, BRANCH_WIDTH // 2), lambda b, n: (b * nb + n, cg + 1)),
            pl.BlockSpec(bias.shape, lambda b, n: (0, 0, 0, 0)),
            pl.BlockSpec(sink.shape, lambda b, n: (0, 0, 0)),
        ],
        out_specs=pl.BlockSpec((W, BRANCH_WIDTH), lambda b, n: (b * nb + n, 0)),
        out_shape=jax.ShapeDtypeStruct((bsz * seq, BRANCH_WIDTH), BF16),
        scratch_shapes=[pltpu.VMEM((BRANCH_WIDTH, W), F32)],
        compiler_params=_params(("parallel", "arbitrary")),
        name="swa",
    )(proj, proj, proj, proj, proj, proj, proj, bias, sink)


def _swa_sink_rows(sinks):
    s = (sinks.astype(F32) * LOG2E).reshape(B_KV_HEADS, 1, B_GROUP, 1)
    return jnp.broadcast_to(s, (B_KV_HEADS, 1, B_GROUP, WINDOW)).reshape(
        B_KV_HEADS, 1, B_GROUP * WINDOW)


SUBLANES = 8


def _scan_steps(a, b, index, length, axis):
    k = 1
    while k < length:
        keep = index >= k
        a_prev = jnp.where(keep, pltpu.roll(a, k, axis), 1.0)
        b_prev = jnp.where(keep, pltpu.roll(b, k, axis), 0.0)
        b = a * b_prev + b
        a = a * a_prev
        k *= 2
    return a, b


def _rglru_kernel(xc_ref, gc_ref, cw_ref, cb_ref, wr_ref, br_ref, wi_ref, bi_ref, lam_ref, o_ref,
                  xpad_sc, h_sc, a_sc, b_sc, hin_sc, *, seq):
    T = LRU_CHUNK
    PAD = 8
    xpad_sc[:PAD, :] = jnp.zeros((PAD, C_BLOCK_DIM), F32)
    xpad_sc[PAD:, :] = xc_ref[...]
    h_sc[...] = jnp.zeros_like(h_sc)
    nlam = -lam_ref[...]
    softplus = jnp.maximum(nlam, 0.0) + jnp.log1p(jnp.exp(-jnp.abs(nlam)))
    decay = -LRU_C * softplus
    tiles = T // SUBLANES
    sub = lax.broadcasted_iota(jnp.int32, (tiles, SUBLANES, C_BLOCK_DIM), 1)
    tile_row = lax.broadcasted_iota(jnp.int32, (tiles, C_BLOCK_DIM), 0)

    def chunk(c, carry):
        t0 = pl.multiple_of(c * T, T)
        xw = xpad_sc[pl.ds(t0, T + PAD), :]
        conv = cb_ref[...]
        for w in range(CONV_WIDTH):
            off = PAD - (CONV_WIDTH - 1) + w
            conv = conv + cw_ref[w:w + 1, :] * xw[off:off + T, :]
        cbf = conv.astype(BF16)
        r = jax.nn.sigmoid(jnp.dot(cbf, wr_ref[0], preferred_element_type=F32) + br_ref[...])
        i = jax.nn.sigmoid(jnp.dot(cbf, wi_ref[0], preferred_element_type=F32) + bi_ref[...])
        log_a = r * decay
        a = jnp.exp(log_a)
        b = jnp.sqrt(jnp.maximum(-jnp.tanh(log_a) * (a * a + 1.0), 0.0)) * (i * conv)
        a, b = _scan_steps(a.reshape(tiles, SUBLANES, C_BLOCK_DIM),
                           b.reshape(tiles, SUBLANES, C_BLOCK_DIM), sub, SUBLANES, 1)
        a_sc[...] = a.reshape(T, C_BLOCK_DIM)
        b_sc[...] = b.reshape(T, C_BLOCK_DIM)
        last = pl.ds(SUBLANES - 1, tiles, stride=SUBLANES)
        a_tile, b_tile = _scan_steps(a_sc[last, :], b_sc[last, :], tile_row, tiles, 0)
        h_out = a_tile * h_sc[...] + b_tile
        hin_sc[...] = jnp.where(tile_row >= 1, pltpu.roll(h_out, 1, 0), h_sc[...])
        h_sc[...] = h_out[tiles - 1:tiles, :]
        h_in = jnp.concatenate(
            [jnp.broadcast_to(hin_sc[t:t + 1, :], (SUBLANES, C_BLOCK_DIM)) for t in range(tiles)],
            axis=0)
        h = a_sc[...] * h_in + b_sc[...]
        o_ref[pl.ds(t0, T), :] = (h * _silu(gc_ref[pl.ds(t0, T), :])).astype(o_ref.dtype)
        return carry

    lax.fori_loop(0, seq // T, chunk, 0)


def _rglru_branch(proj, conv_w, conv_b, w_r_bf, b_r, w_i_bf, b_i, lam, layer, bsz, seq):
    cd = C_BLOCK_DIM
    cx = COL_XC // cd
    cg = COL_GC // cd
    vspec = pl.BlockSpec((None, 1, cd), lambda b, c: (layer, 0, c))
    wspec = pl.BlockSpec((None, 1, cd, cd), lambda b, c: (layer, c, 0, 0))
    return pl.pallas_call(
        functools.partial(_rglru_kernel, seq=seq),
        grid=(bsz, C_BLOCKS),
        in_specs=[
            pl.BlockSpec((seq, cd), lambda b, c: (b, cx + c)),
            pl.BlockSpec((seq, cd), lambda b, c: (b, cg + c)),
            pl.BlockSpec((None, CONV_WIDTH, cd), lambda b, c: (layer, 0, c)),
            vspec, wspec, vspec, wspec, vspec, vspec,
        ],
        out_specs=pl.BlockSpec((seq, cd), lambda b, c: (b, c)),
        out_shape=jax.ShapeDtypeStruct((bsz * seq, BRANCH_WIDTH), BF16),
        scratch_shapes=[pltpu.VMEM((seq + 8, cd), F32), pltpu.VMEM((1, cd), F32),
                        pltpu.VMEM((LRU_CHUNK, cd), F32), pltpu.VMEM((LRU_CHUNK, cd), F32),
                        pltpu.VMEM((LRU_CHUNK // SUBLANES, cd), F32)],
        compiler_params=_params(("parallel", "parallel")),
        name="rglru",
    )(proj, proj, conv_w, conv_b, w_r_bf, b_r, w_i_bf, b_i, lam)


def _merge_kernel(ya_ref, yb_ref, yc_ref, ma_ref, mb_ref, mc_ref, wbr_ref, o_ref):
    merged = None
    for n, (y_ref, m_ref) in enumerate(((ya_ref, ma_ref), (yb_ref, mb_ref), (yc_ref, mc_ref))):
        term = jax.nn.sigmoid(m_ref[...]) * jnp.dot(y_ref[...], wbr_ref[n],
                                                    preferred_element_type=F32)
        merged = term if merged is None else merged + term
    o_ref[...] = merged.astype(o_ref.dtype)


def _merge(ya, yb, yc, proj, wbr_bf, layer):
    t = ya.shape[0]
    d = wbr_bf.shape[3]
    tm, tn = MERGE_TM, MERGE_TN
    assert COL_MG % tn == 0 and d % tn == 0
    cm = COL_MG // tn
    per = d // tn
    yspec = pl.BlockSpec((tm, BRANCH_WIDTH), lambda i, c: (i, 0))
    mspec = lambda n: pl.BlockSpec((tm, tn), lambda i, c: (i, cm + n * per + c))
    return pl.pallas_call(
        _merge_kernel,
        grid=(t // tm, per),
        in_specs=[
            yspec, yspec, yspec, mspec(0), mspec(1), mspec(2),
            pl.BlockSpec((None, 3, BRANCH_WIDTH, tn), lambda i, c: (layer, 0, 0, c)),
        ],
        out_specs=pl.BlockSpec((tm, tn), lambda i, c: (i, c)),
        out_shape=jax.ShapeDtypeStruct((t, d), BF16),
        compiler_params=_params(("parallel", "arbitrary")),
        name="merge",
    )(ya, yb, yc, proj, proj, proj, wbr_bf)


def _out_ple_kernel(mg_ref, x_ref, p_ref, wout_ref, g_ref, wpg_ref, wpp_ref, fg_ref, o_ref, *,
                    final):
    x = x_ref[...] + jnp.dot(mg_ref[...], wout_ref[...], preferred_element_type=F32)
    h = _rms(x, g_ref[...]).astype(BF16)
    gate = jax.nn.sigmoid(jnp.dot(h, wpg_ref[...], preferred_element_type=F32))
    emb = jnp.dot(p_ref[...].astype(BF16), wpp_ref[...], preferred_element_type=F32)
    y = x + gate * emb
    if final:
        y = _rms(y, fg_ref[...])
    o_ref[...] = y


def _out_ple(merged, x2d, p3d, wout_bf, g, wpg_bf, wpp_bf, final_g, layer, final):
    t, d = x2d.shape
    tm = PLE_TM
    wspec = lambda r: pl.BlockSpec((None, r, d), lambda i: (layer, 0, 0),
                                   pipeline_mode=pl.Buffered(1))
    row = lambda w: pl.BlockSpec((tm, w), lambda i: (i, 0))
    return pl.pallas_call(
        functools.partial(_out_ple_kernel, final=final),
        grid=(t // tm,),
        in_specs=[row(d), row(d), pl.BlockSpec((None, tm, PLE_DIM), lambda i: (layer, i, 0)),
                  wspec(d), pl.BlockSpec((None, 1, d), lambda i: (layer, 0, 0)), wspec(d),
                  wspec(PLE_DIM), pl.BlockSpec((1, d), lambda i: (0, 0))],
        out_specs=row(d),
        out_shape=jax.ShapeDtypeStruct((t, d), F32),
        compiler_params=_params(("parallel",)),
        name="out_ple",
    )(merged, x2d, p3d, wout_bf, g, wpg_bf, wpp_bf, final_g.reshape(1, d))


def kernel(x, p, rpe_table, norm_g, w_in, sinks, conv_w, conv_b, w_r, b_r, w_i, b_i, lam, w_br,
           w_out, ple_norm_g, w_pg, w_pp, final_norm_g):
    bsz, seq, d = x.shape
    depth = w_in.shape[0]
    assert d == D_MODEL and w_in.shape[2] == IN_WIDTH
    assert seq % MOBA_BLOCK == 0 and seq % LRU_CHUNK == 0 and (bsz * seq) % PROJ_TM == 0
    t = bsz * seq
    xf = x.reshape(t, d)
    table = rpe_table.astype(F32).T * LOG2E
    bias_a = _bias_lookup(table[:A_HEADS], _moba_bucket_planes(), 1)
    bias_b = _bias_lookup(table[A_HEADS:], _swa_bucket_planes(), B_GROUP)
    bf = lambda w: w.astype(BF16)
    vec = lambda v: v.reshape(depth, 1, v.shape[-1])
    w_in_bf, w_r_bf, w_i_bf, w_br_bf, w_out_bf, w_pg_bf, w_pp_bf = map(
        bf, (w_in, w_r, w_i, w_br, w_out, w_pg, w_pp))
    p3d = p.reshape(depth, t, PLE_DIM)
    for i in range(depth):
        proj = _in_proj(xf, vec(norm_g), w_in_bf, i)
        ya = _moba_branch(proj, bias_a, bsz, seq)
        yb = _swa_branch(proj, bias_b, _swa_sink_rows(sinks[i]), bsz, seq)
        yc = _rglru_branch(proj, conv_w, vec(conv_b), w_r_bf, vec(b_r), w_i_bf, vec(b_i),
                           vec(lam), i, bsz, seq)
        merged = _merge(ya, yb, yc, proj, w_br_bf, i)
        xf = _out_ple(merged, xf, p3d, w_out_bf, vec(ple_norm_g), w_pg_bf, w_pp_bf,
                      final_norm_g, i, final=(i == depth - 1))
    return xf.reshape(bsz, seq, d)
```

```python
import functools
import math

import jax
import jax.numpy as jnp
import numpy as np
from jax import lax
from jax.experimental import pallas as pl
from jax.experimental.pallas import tpu as pltpu

F32 = jnp.float32
BF16 = jnp.bfloat16

D_MODEL = 2048
PLE_DIM = 256
BRANCH_WIDTH = 1024
A_HEADS = 8
A_HEAD_DIM = 128
MOBA_BLOCK = 256
MOBA_TOPK = 3
B_Q_HEADS = 16
B_KV_HEADS = 4
B_GROUP = B_Q_HEADS // B_KV_HEADS
B_HEAD_DIM = 64
WINDOW = 128
C_BLOCKS = 8
C_BLOCK_DIM = 128
CONV_WIDTH = 4
LRU_C = 8.0
RPE_BUCKETS = 32
RPE_MAX_DIST = 128
EPS = 1e-6
NEG = -1e30
LOG2E = math.log2(math.e)

COL_QA = 0
COL_KA = COL_QA + BRANCH_WIDTH
COL_VA = COL_KA + BRANCH_WIDTH
COL_GA = COL_VA + BRANCH_WIDTH
COL_QB = COL_GA + BRANCH_WIDTH
COL_KB = COL_QB + BRANCH_WIDTH
COL_VB = COL_KB + B_KV_HEADS * B_HEAD_DIM
COL_GB = COL_VB + B_KV_HEADS * B_HEAD_DIM
COL_XC = COL_GB + BRANCH_WIDTH
COL_GC = COL_XC + BRANCH_WIDTH
COL_MG = COL_GC + BRANCH_WIDTH
IN_WIDTH = COL_MG + 3 * D_MODEL

PROJ_TM = 2048
PROJ_TN = 512
MERGE_TM = 2048
MERGE_TN = 512
PLE_TM = 512
LRU_CHUNK = 256

VMEM_LIMIT = 62 * 1024 * 1024


def _params(sem):
    return pltpu.CompilerParams(dimension_semantics=sem, vmem_limit_bytes=VMEM_LIMIT)


def _rms(xf, g):
    return xf * lax.rsqrt(jnp.mean(xf * xf, axis=-1, keepdims=True) + EPS) * g


def _in_proj_kernel(x_ref, g_ref, w_ref, o_ref, mg_ref, h_sc):
    @pl.when(pl.program_id(1) == 0)
    def _():
        h_sc[...] = _rms(x_ref[...], g_ref[...]).astype(BF16)

    acc = jnp.dot(h_sc[...], w_ref[...], preferred_element_type=F32)
    o_ref[...] = acc
    mg_ref[...] = acc.astype(mg_ref.dtype)


def _in_proj(x2d, g, w_bf, layer):
    t, d = x2d.shape
    n = w_bf.shape[2]
    assert COL_MG % PROJ_TN == 0
    main_tiles = COL_MG // PROJ_TN
    return pl.pallas_call(
        _in_proj_kernel,
        grid=(t // PROJ_TM, n // PROJ_TN),
        in_specs=[
            pl.BlockSpec((PROJ_TM, d), lambda i, j: (i, 0)),
            pl.BlockSpec((None, 1, d), lambda i, j: (layer, 0, 0)),
            pl.BlockSpec((None, d, PROJ_TN), lambda i, j: (layer, 0, j)),
        ],
        out_specs=[
            pl.BlockSpec((PROJ_TM, PROJ_TN), lambda i, j: (i, jnp.minimum(j, main_tiles))),
            pl.BlockSpec((PROJ_TM, PROJ_TN), lambda i, j: (i, jnp.maximum(j - main_tiles, 0))),
        ],
        out_shape=[jax.ShapeDtypeStruct((t, COL_MG + PROJ_TN), F32),
                   jax.ShapeDtypeStruct((t, n - COL_MG), BF16)],
        scratch_shapes=[pltpu.VMEM((PROJ_TM, d), BF16)],
        compiler_params=_params(("parallel", "arbitrary")),
        name="in_proj",
    )(x2d, g, w_bf)


MASKED_BUCKET = RPE_BUCKETS


def _t5_bucket_np(dist):
    n = np.maximum(dist, 0)
    max_exact = RPE_BUCKETS // 2
    ratio = np.log(np.maximum(n, 1).astype(np.float32) / np.float32(max_exact)) / np.float32(
        math.log(RPE_MAX_DIST / max_exact))
    large = max_exact + (ratio * np.float32(RPE_BUCKETS - max_exact)).astype(np.int32)
    large = np.minimum(large, RPE_BUCKETS - 1)
    return np.where(n < max_exact, n, large).astype(np.int32)


def _moba_bucket_planes():
    L = MOBA_BLOCK
    ki = np.arange(L)[:, None]
    qi = np.arange(L)[None, :]
    d_own = qi - ki
    d_far = 2 * L + qi - ki
    assert d_far.min() >= RPE_MAX_DIST
    own = np.where(d_own >= 0, _t5_bucket_np(d_own), MASKED_BUCKET)
    return np.stack([own, _t5_bucket_np(L + qi - ki), _t5_bucket_np(d_far)]).astype(np.int32)


def _swa_bucket_planes():
    W = WINDOW
    kk = np.arange(2 * W)[:, None]
    qi = np.arange(W)[None, :]
    dist = qi - kk + W
    ok = (dist >= 0) & (dist < W)
    b = _t5_bucket_np(dist)
    return np.stack([np.where(ok, b, MASKED_BUCKET),
                     np.where(ok & (kk >= W), b, MASKED_BUCKET)]).astype(np.int32)


def _bias_lookup_kernel(tab_ref, idx_ref, o_ref):
    h = pl.program_id(0)
    idx = idx_ref[0]
    acc = jnp.full(idx.shape, NEG, F32)
    for b in range(RPE_BUCKETS):
        acc = jnp.where(idx == b, tab_ref[h * RPE_BUCKETS + b], acc)
    o_ref[0, 0] = acc


def _bias_lookup(table_hb, planes, heads_per_row):
    nh = table_hb.shape[0]
    npl, r, c = planes.shape
    return pl.pallas_call(
        _bias_lookup_kernel,
        grid=(nh, npl),
        in_specs=[
            pl.BlockSpec(memory_space=pltpu.SMEM),
            pl.BlockSpec((1, r, c), lambda h, p: (p, 0, 0)),
        ],
        out_specs=pl.BlockSpec((1, 1, r, c),
                               lambda h, p: (h // heads_per_row, p, 0, h % heads_per_row)),
        out_shape=jax.ShapeDtypeStruct((nh // heads_per_row, npl, r, c * heads_per_row), F32),
        compiler_params=_params(("parallel", "parallel")),
        name="bias_lookup",
    )(table_hb.reshape(-1), jnp.asarray(planes))


GATE_ROWS = 16


def _silu(x):
    return x * jax.nn.sigmoid(x)


MOBA_GROUP = 2
MOBA_UNROLL = 2
MOBA_HEADS = 2


def _moba_prep_kernel(q_ref, k_ref, v_ref, kaug_ref, vt_ref, qaug_ref, kmean_sc, *, nb):
    L = MOBA_BLOCK
    dh = A_HEAD_DIM
    G = MOBA_GROUP

    lanes = lax.broadcasted_iota(jnp.int32, (L, dh), 1)
    kmean_sc[...] = jnp.zeros_like(kmean_sc)
    for n in range(nb):
        rows = slice((n % G) * L, (n % G + 1) * L)
        kn = k_ref[n * L:(n + 1) * L, :]
        kaug_ref[0, n // G, rows, :dh] = kn.astype(BF16)
        kaug_ref[0, n // G, rows, dh:] = jnp.where(lanes == n, 1.0, 0.0).astype(BF16)
        kmean_sc[n:n + 1, :] = jnp.sum(kn, axis=0, keepdims=True) * (1.0 / L)
        vt_ref[0, n // G, :, rows] = v_ref[n * L:(n + 1) * L, :].T.astype(BF16)

    blk = lax.broadcasted_iota(jnp.int32, (GATE_ROWS, L), 0)
    blk_f = blk.astype(F32)
    for j in range(nb):
        qt = q_ref[j * L:(j + 1) * L, :].T
        gate = jnp.dot(kmean_sc[...], qt, precision=lax.Precision.HIGHEST,
                       preferred_element_type=F32)
        gm = jnp.where(blk < j, gate, NEG)
        picked = jnp.zeros((GATE_ROWS, L), F32)
        for _ in range(MOBA_TOPK):
            mx = jnp.max(gm, axis=0, keepdims=True)
            first = jnp.min(jnp.where(gm == mx, blk_f, float(GATE_ROWS)), axis=0, keepdims=True)
            hit = blk_f == first
            picked = jnp.where(hit, 1.0, picked)
            gm = jnp.where(hit, -jnp.inf, gm)
        allowed = ((picked > 0.0) & (blk < j)) | (blk == j)
        qaug_ref[0, j, :dh, :] = (qt * (dh ** -0.5 * LOG2E)).astype(BF16)
        qaug_ref[0, j, dh:dh + GATE_ROWS, :] = jnp.where(allowed, 0.0, NEG).astype(BF16)
        qaug_ref[0, j, dh + GATE_ROWS:, :] = jnp.zeros((dh - GATE_ROWS, L), BF16)


def _moba_prep(proj, bsz, seq):
    L = MOBA_BLOCK
    dh = A_HEAD_DIM
    G = MOBA_GROUP
    nb = seq // L
    nh = bsz * A_HEADS
    cq, ck, cv = (c // dh for c in (COL_QA, COL_KA, COL_VA))
    col = lambda c: pl.BlockSpec((seq, dh), lambda b, h: (b, c + h))
    shapes = [(nh, nb // G, G * L, 2 * dh), (nh, nb // G, dh, G * L), (nh, nb, 2 * dh, L)]
    return pl.pallas_call(
        functools.partial(_moba_prep_kernel, nb=nb),
        grid=(bsz, A_HEADS),
        in_specs=[col(cq), col(ck), col(cv)],
        out_specs=[pl.BlockSpec((1,) + s[1:], lambda b, h: (b * A_HEADS + h, 0, 0, 0))
                   for s in shapes],
        out_shape=[jax.ShapeDtypeStruct(s, BF16) for s in shapes],
        scratch_shapes=[pltpu.VMEM((GATE_ROWS, dh), F32)],
        compiler_params=_params(("parallel", "parallel")),
        name="moba_prep",
    )(proj, proj, proj)


def _moba_schedule(nb):
    G = MOBA_GROUP
    return [(j, i) for j in range(nb) for i in range((j + G) // G)]


def _moba_kernel(jt_ref, it_ref, kaug_ref, vt_ref, qaug_ref, g_ref, bias_ref, o_ref,
                 s_sc, p_sc, alpha_sc, lfin_sc, lout_sc, m_sc, l_sc, acc_sc, *, sched):
    L = MOBA_BLOCK
    dh = A_HEAD_DIM
    G = MOBA_GROUP
    U = MOBA_UNROLL
    nt = len(sched)
    heads = range(MOBA_HEADS)
    hcol = lambda u: slice(u * dh, (u + 1) * dh)
    last_group = lambda j: j // G

    def scores(u, t, slot):
        s_sc[u, slot] = jnp.dot(kaug_ref[u, it_ref[t]], qaug_ref[u, jt_ref[t]],
                                preferred_element_type=F32)

    def softmax(u, t, slot):
        j = jt_ref[t]
        i = it_ref[t]
        s = s_sc[u, slot]
        parts = []
        for g in range(G):
            n = i * G + g
            plane = jnp.where(n == j, 0, jnp.where(n == j - 1, 1, 2))
            parts.append(s[g * L:(g + 1) * L, :] + bias_ref[u, plane])
        m_old = jnp.where(i == 0, NEG, m_sc[u])
        m_new = m_old
        for sp in parts:
            m_new = jnp.maximum(m_new, jnp.max(sp, axis=0, keepdims=True))
        alpha = jnp.exp2(m_old - m_new)
        probs = [jnp.exp2(sp - m_new) for sp in parts]
        l_new = alpha * l_sc[u]
        for pr in probs:
            l_new = l_new + jnp.sum(pr, axis=0, keepdims=True)
        p_sc[u, slot] = jnp.concatenate(probs, axis=0).astype(BF16)
        alpha_sc[u, slot] = alpha
        lfin_sc[u, slot] = l_new
        l_sc[u] = l_new
        m_sc[u] = m_new

    def weighted_values(u, t, slot, r):
        acc_sc[u, r] = alpha_sc[u, slot] * acc_sc[u, (r - 1) % U] + jnp.dot(
            vt_ref[u, it_ref[t]], p_sc[u, slot], preferred_element_type=F32)
        lout_sc[u, r] = lfin_sc[u, slot]

    def finish(u, j, r):
        out_t = acc_sc[u, r] / lout_sc[u, r]
        rows = pl.ds(pl.multiple_of(j * L, L), L)
        o_ref[rows, hcol(u)] = (out_t.T * _silu(g_ref[rows, hcol(u)])).astype(o_ref.dtype)

    s_sc[:, 1] = jnp.full(s_sc.shape[:1] + s_sc.shape[2:], -jnp.inf, F32)
    p_sc[:, 0] = jnp.zeros(p_sc.shape[:1] + p_sc.shape[2:], BF16)
    alpha_sc[...] = jnp.ones_like(alpha_sc)
    lfin_sc[...] = jnp.ones_like(lfin_sc)
    m_sc[...] = jnp.full_like(m_sc, NEG)
    l_sc[...] = jnp.zeros_like(l_sc)
    acc_sc[...] = jnp.zeros_like(acc_sc)

    def unrolled_steps(k, carry):
        base = U * k
        done = []
        for r in range(U):
            t = base + r
            t_pv = jnp.maximum(t - 2, 0)
            t_sm = jnp.maximum(t - 1, 0)
            for u in heads:
                weighted_values(u, t_pv, r % 2, r)
                softmax(u, t_sm, (r + 1) % 2)
                scores(u, t, r % 2)
            done.append((t - 2, t_pv, r))
        for t_real, t_pv, r in done:
            j_done = jt_ref[t_pv]

            @pl.when((t_real >= 0) & (it_ref[t_pv] == last_group(j_done)))
            def _():
                for u in heads:
                    finish(u, j_done, r)

        return carry

    assert U % 2 == 0
    n_loop = nt // U
    lax.fori_loop(0, n_loop, unrolled_steps, 0)
    for t in range(n_loop * U, nt + 2):
        r = t % U
        for u in heads:
            if 0 <= t - 2:
                weighted_values(u, t - 2, r % 2, r)
                j, i = sched[t - 2]
                if i == last_group(j):
                    finish(u, j, r)
            if 0 <= t - 1 < nt:
                softmax(u, t - 1, (r + 1) % 2)
            if t < nt:
                scores(u, t, r % 2)


def _moba_branch(proj, bias, bsz, seq):
    L = MOBA_BLOCK
    dh = A_HEAD_DIM
    G = MOBA_GROUP
    nb = seq // L
    hp = MOBA_HEADS
    assert min(MOBA_TOPK, nb - 1) == MOBA_TOPK and nb <= GATE_ROWS and nb % G == 0
    assert A_HEADS % hp == 0
    kaug, vt, qaug = _moba_prep(proj, bsz, seq)
    sched = _moba_schedule(nb)
    jt = jnp.asarray(np.array([j for j, _ in sched], np.int32))
    it = jnp.asarray(np.array([i for _, i in sched], np.int32))
    pairs = A_HEADS // hp
    cg = COL_GA // (hp * dh)
    smem = pl.BlockSpec(memory_space=pltpu.SMEM)
    per_pair = lambda a: pl.BlockSpec((hp,) + a.shape[1:], lambda b, h: (b * pairs + h, 0, 0, 0))
    return pl.pallas_call(
        functools.partial(_moba_kernel, sched=sched),
        grid=(bsz, pairs),
        in_specs=[smem, smem, per_pair(kaug), per_pair(vt), per_pair(qaug),
                  pl.BlockSpec((seq, hp * dh), lambda b, h: (b, cg + h)),
                  pl.BlockSpec((hp, 3, L, L), lambda b, h: (h, 0, 0, 0))],
        out_specs=pl.BlockSpec((seq, hp * dh), lambda b, h: (b, h)),
        out_shape=jax.ShapeDtypeStruct((bsz * seq, BRANCH_WIDTH), BF16),
        scratch_shapes=[
            pltpu.VMEM((hp, 2, G * L, L), F32),
            pltpu.VMEM((hp, 2, G * L, L), BF16),
            pltpu.VMEM((hp, 2, 1, L), F32),
            pltpu.VMEM((hp, 2, 1, L), F32),
            pltpu.VMEM((hp, MOBA_UNROLL, 1, L), F32),
            pltpu.VMEM((hp, 1, L), F32),
            pltpu.VMEM((hp, 1, L), F32),
            pltpu.VMEM((hp, MOBA_UNROLL, dh, L), F32),
        ],
        compiler_params=_params(("parallel", "parallel")),
        name="moba",
    )(jt, it, kaug, vt, qaug, proj, bias)


def _swa_kernel(q_ref, kp_ref, kc_ref, vp_ref, vc_ref, g0_ref, g1_ref, bias_ref, sink_ref, o_ref,
                ot_sc):
    W = WINDOW
    dh = B_HEAD_DIM
    kvw = B_KV_HEADS * dh
    plane = jnp.where(pl.program_id(1) == 0, 1, 0)
    qt = q_ref[...].T * (dh ** -0.5 * LOG2E)
    head = lambda g, h: slice((g * B_GROUP + h) * dh, (g * B_GROUP + h + 1) * dh)
    x = jnp.concatenate(
        [jnp.concatenate([qt[head(g, h), :] for g in range(B_KV_HEADS)], axis=0)
         for h in range(B_GROUP)], axis=1).astype(BF16)
    kband = jnp.concatenate([kp_ref[...], kc_ref[...]], axis=0)
    vband_t = jnp.concatenate([vp_ref[...], vc_ref[...]], axis=0).T.astype(BF16)
    lane_group = lax.broadcasted_iota(jnp.int32, (2 * W, kvw), 1) // dh
    for g in range(B_KV_HEADS):
        kg = jnp.where(lane_group == g, kband, 0.0).astype(BF16)
        s = jnp.dot(kg, x, preferred_element_type=F32) + bias_ref[g, plane]
        sink = sink_ref[g]
        m = jnp.maximum(jnp.max(s, axis=0, keepdims=True), sink)
        p = jnp.exp2(s - m)
        denom = jnp.sum(p, axis=0, keepdims=True) + jnp.exp2(sink - m)
        o = jnp.dot(vband_t[g * dh:(g + 1) * dh, :], p.astype(BF16),
                    preferred_element_type=F32) / denom
        for h in range(B_GROUP):
            ot_sc[head(g, h), :] = o[:, h * W:(h + 1) * W]
    gate = jnp.concatenate([g0_ref[...], g1_ref[...]], axis=1)
    o_ref[...] = (ot_sc[...].T * _silu(gate)).astype(o_ref.dtype)


def _swa_branch(proj, bias, sink, bsz, seq):
    W = WINDOW
    nb = seq // W
    kvw = B_KV_HEADS * B_HEAD_DIM
    cq = COL_QB // BRANCH_WIDTH
    ck = COL_KB // kvw
    cv = COL_VB // kvw
    cg = COL_GB // (BRANCH_WIDTH // 2)
    assert COL_QB % BRANCH_WIDTH == 0 and COL_KB % kvw == 0 and COL_VB % kvw == 0
    assert COL_GB % (BRANCH_WIDTH // 2) == 0
    prev = lambda b, n: (b * nb + jnp.maximum(n - 1, 0))
    return pl.pallas_call(
        _swa_kernel,
        grid=(bsz, nb),
        in_specs=[
            pl.BlockSpec((W, BRANCH_WIDTH), lambda b, n: (b * nb + n, cq)),
            pl.BlockSpec((W, kvw), lambda b, n: (prev(b, n), ck)),
            pl.BlockSpec((W, kvw), lambda b, n: (b * nb + n, ck)),
            pl.BlockSpec((W, kvw), lambda b, n: (prev(b, n), cv)),
            pl.BlockSpec((W, kvw), lambda b, n: (b * nb + n, cv)),
            pl.BlockSpec((W, BRANCH_WIDTH // 2), lambda b, n: (b * nb + n, cg)),
            pl.BlockSpec((W, BRANCH_WIDTH // 2), lambda b, n: (b * nb + n, cg + 1)),
            pl.BlockSpec(bias.shape, lambda b, n: (0, 0, 0, 0)),
            pl.BlockSpec(sink.shape, lambda b, n: (0, 0, 0)),
        ],
        out_specs=pl.BlockSpec((W, BRANCH_WIDTH), lambda b, n: (b * nb + n, 0)),
        out_shape=jax.ShapeDtypeStruct((bsz * seq, BRANCH_WIDTH), BF16),
        scratch_shapes=[pltpu.VMEM((BRANCH_WIDTH, W), F32)],
        compiler_params=_params(("parallel", "arbitrary")),
        name="swa",
    )(proj, proj, proj, proj, proj, proj, proj, bias, sink)


def _swa_sink_rows(sinks):
    s = (sinks.astype(F32) * LOG2E).reshape(B_KV_HEADS, 1, B_GROUP, 1)
    return jnp.broadcast_to(s, (B_KV_HEADS, 1, B_GROUP, WINDOW)).reshape(
        B_KV_HEADS, 1, B_GROUP * WINDOW)


SUBLANES = 8


def _scan_steps(a, b, index, length, axis):
    k = 1
    while k < length:
        keep = index >= k
        a_prev = jnp.where(keep, pltpu.roll(a, k, axis), 1.0)
        b_prev = jnp.where(keep, pltpu.roll(b, k, axis), 0.0)
        b = a * b_prev + b
        a = a * a_prev
        k *= 2
    return a, b


def _rglru_kernel(xc_ref, gc_ref, cw_ref, cb_ref, wr_ref, br_ref, wi_ref, bi_ref, lam_ref, o_ref,
                  xpad_sc, h_sc, a_sc, b_sc, hin_sc, *, seq):
    T = LRU_CHUNK
    PAD = 8
    xpad_sc[:PAD, :] = jnp.zeros((PAD, C_BLOCK_DIM), F32)
    xpad_sc[PAD:, :] = xc_ref[...]
    h_sc[...] = jnp.zeros_like(h_sc)
    nlam = -lam_ref[...]
    softplus = jnp.maximum(nlam, 0.0) + jnp.log1p(jnp.exp(-jnp.abs(nlam)))
    decay = -LRU_C * softplus
    tiles = T // SUBLANES
    sub = lax.broadcasted_iota(jnp.int32, (tiles, SUBLANES, C_BLOCK_DIM), 1)
    tile_row = lax.broadcasted_iota(jnp.int32, (tiles, C_BLOCK_DIM), 0)

    def chunk(c, carry):
        t0 = pl.multiple_of(c * T, T)
        xw = xpad_sc[pl.ds(t0, T + PAD), :]
        conv = cb_ref[...]
        for w in range(CONV_WIDTH):
            off = PAD - (CONV_WIDTH - 1) + w
            conv = conv + cw_ref[w:w + 1, :] * xw[off:off + T, :]
        cbf = conv.astype(BF16)
        r = jax.nn.sigmoid(jnp.dot(cbf, wr_ref[0], preferred_element_type=F32) + br_ref[...])
        i = jax.nn.sigmoid(jnp.dot(cbf, wi_ref[0], preferred_element_type=F32) + bi_ref[...])
        log_a = r * decay
        a = jnp.exp(log_a)
        b = jnp.sqrt(jnp.maximum(-jnp.tanh(log_a) * (a * a + 1.0), 0.0)) * (i * conv)
        a, b = _scan_steps(a.reshape(tiles, SUBLANES, C_BLOCK_DIM),
                           b.reshape(tiles, SUBLANES, C_BLOCK_DIM), sub, SUBLANES, 1)
        a_sc[...] = a.reshape(T, C_BLOCK_DIM)
        b_sc[...] = b.reshape(T, C_BLOCK_DIM)
        last = pl.ds(SUBLANES - 1, tiles, stride=SUBLANES)
        a_tile, b_tile = _scan_steps(a_sc[last, :], b_sc[last, :], tile_row, tiles, 0)
        h_out = a_tile * h_sc[...] + b_tile
        hin_sc[...] = jnp.where(tile_row >= 1, pltpu.roll(h_out, 1, 0), h_sc[...])
        h_sc[...] = h_out[tiles - 1:tiles, :]
        h_in = jnp.concatenate(
            [jnp.broadcast_to(hin_sc[t:t + 1, :], (SUBLANES, C_BLOCK_DIM)) for t in range(tiles)],
            axis=0)
        h = a_sc[...] * h_in + b_sc[...]
        o_ref[pl.ds(t0, T), :] = (h * _silu(gc_ref[pl.ds(t0, T), :])).astype(o_ref.dtype)
        return carry

    lax.fori_loop(0, seq // T, chunk, 0)


def _rglru_branch(proj, conv_w, conv_b, w_r_bf, b_r, w_i_bf, b_i, lam, layer, bsz, seq):
    cd = C_BLOCK_DIM
    cx = COL_XC // cd
    cg = COL_GC // cd
    vspec = pl.BlockSpec((None, 1, cd), lambda b, c: (layer, 0, c))
    wspec = pl.BlockSpec((None, 1, cd, cd), lambda b, c: (layer, c, 0, 0))
    return pl.pallas_call(
        functools.partial(_rglru_kernel, seq=seq),
        grid=(bsz, C_BLOCKS),
        in_specs=[
            pl.BlockSpec((seq, cd), lambda b, c: (b, cx + c)),
            pl.BlockSpec((seq, cd), lambda b, c: (b, cg + c)),
            pl.BlockSpec((None, CONV_WIDTH, cd), lambda b, c: (layer, 0, c)),
            vspec, wspec, vspec, wspec, vspec, vspec,
        ],
        out_specs=pl.BlockSpec((seq, cd), lambda b, c: (b, c)),
        out_shape=jax.ShapeDtypeStruct((bsz * seq, BRANCH_WIDTH), BF16),
        scratch_shapes=[pltpu.VMEM((seq + 8, cd), F32), pltpu.VMEM((1, cd), F32),
                        pltpu.VMEM((LRU_CHUNK, cd), F32), pltpu.VMEM((LRU_CHUNK, cd), F32),
                        pltpu.VMEM((LRU_CHUNK // SUBLANES, cd), F32)],
        compiler_params=_params(("parallel", "parallel")),
        name="rglru",
    )(proj, proj, conv_w, conv_b, w_r_bf, b_r, w_i_bf, b_i, lam)


def _merge_kernel(ya_ref, yb_ref, yc_ref, ma_ref, mb_ref, mc_ref, wbr_ref, o_ref):
    merged = None
    for n, (y_ref, m_ref) in enumerate(((ya_ref, ma_ref), (yb_ref, mb_ref), (yc_ref, mc_ref))):
        term = jax.nn.sigmoid(m_ref[...].astype(F32)) * jnp.dot(y_ref[...], wbr_ref[n],
                                                                preferred_element_type=F32)
        merged = term if merged is None else merged + term
    o_ref[...] = merged.astype(o_ref.dtype)


def _merge(ya, yb, yc, gates, wbr_bf, layer):
    t = ya.shape[0]
    d = wbr_bf.shape[3]
    tm, tn = MERGE_TM, MERGE_TN
    assert d % tn == 0
    cm = 0
    per = d // tn
    yspec = pl.BlockSpec((tm, BRANCH_WIDTH), lambda i, c: (i, 0))
    mspec = lambda n: pl.BlockSpec((tm, tn), lambda i, c: (i, cm + n * per + c))
    return pl.pallas_call(
        _merge_kernel,
        grid=(t // tm, per),
        in_specs=[
            yspec, yspec, yspec, mspec(0), mspec(1), mspec(2),
            pl.BlockSpec((None, 3, BRANCH_WIDTH, tn), lambda i, c: (layer, 0, 0, c)),
        ],
        out_specs=pl.BlockSpec((tm, tn), lambda i, c: (i, c)),
        out_shape=jax.ShapeDtypeStruct((t, d), BF16),
        compiler_params=_params(("parallel", "arbitrary")),
        name="merge",
    )(ya, yb, yc, gates, gates, gates, wbr_bf)


def _out_ple_kernel(mg_ref, x_ref, p_ref, wout_ref, g_ref, wpg_ref, wpp_ref, fg_ref, o_ref, *,
                    final):
    x = x_ref[...] + jnp.dot(mg_ref[...], wout_ref[...], preferred_element_type=F32)
    h = _rms(x, g_ref[...]).astype(BF16)
    gate = jax.nn.sigmoid(jnp.dot(h, wpg_ref[...], preferred_element_type=F32))
    emb = jnp.dot(p_ref[...].astype(BF16), wpp_ref[...], preferred_element_type=F32)
    y = x + gate * emb
    if final:
        y = _rms(y, fg_ref[...])
    o_ref[...] = y


def _out_ple(merged, x2d, p3d, wout_bf, g, wpg_bf, wpp_bf, final_g, layer, final):
    t, d = x2d.shape
    tm = PLE_TM
    wspec = lambda r: pl.BlockSpec((None, r, d), lambda i: (layer, 0, 0),
                                   pipeline_mode=pl.Buffered(1))
    row = lambda w: pl.BlockSpec((tm, w), lambda i: (i, 0))
    return pl.pallas_call(
        functools.partial(_out_ple_kernel, final=final),
        grid=(t // tm,),
        in_specs=[row(d), row(d), pl.BlockSpec((None, tm, PLE_DIM), lambda i: (layer, i, 0)),
                  wspec(d), pl.BlockSpec((None, 1, d), lambda i: (layer, 0, 0)), wspec(d),
                  wspec(PLE_DIM), pl.BlockSpec((1, d), lambda i: (0, 0))],
        out_specs=row(d),
        out_shape=jax.ShapeDtypeStruct((t, d), F32),
        compiler_params=_params(("parallel",)),
        name="out_ple",
    )(merged, x2d, p3d, wout_bf, g, wpg_bf, wpp_bf, final_g.reshape(1, d))


def kernel(x, p, rpe_table, norm_g, w_in, sinks, conv_w, conv_b, w_r, b_r, w_i, b_i, lam, w_br,
           w_out, ple_norm_g, w_pg, w_pp, final_norm_g):
    bsz, seq, d = x.shape
    depth = w_in.shape[0]
    assert d == D_MODEL and w_in.shape[2] == IN_WIDTH
    assert seq % MOBA_BLOCK == 0 and seq % LRU_CHUNK == 0 and (bsz * seq) % PROJ_TM == 0
    t = bsz * seq
    xf = x.reshape(t, d)
    table = rpe_table.astype(F32).T * LOG2E
    bias_a = _bias_lookup(table[:A_HEADS], _moba_bucket_planes(), 1)
    bias_b = _bias_lookup(table[A_HEADS:], _swa_bucket_planes(), B_GROUP)
    bf = lambda w: w.astype(BF16)
    vec = lambda v: v.reshape(depth, 1, v.shape[-1])
    w_in_bf, w_r_bf, w_i_bf, w_br_bf, w_out_bf, w_pg_bf, w_pp_bf = map(
        bf, (w_in, w_r, w_i, w_br, w_out, w_pg, w_pp))
    p3d = p.reshape(depth, t, PLE_DIM)
    for i in range(depth):
        proj, gates = _in_proj(xf, vec(norm_g), w_in_bf, i)
        ya = _moba_branch(proj, bias_a, bsz, seq)
        yb = _swa_branch(proj, bias_b, _swa_sink_rows(sinks[i]), bsz, seq)
        yc = _rglru_branch(proj, conv_w, vec(conv_b), w_r_bf, vec(b_r), w_i_bf, vec(b_i),
                           vec(lam), i, bsz, seq)
        merged = _merge(ya, yb, yc, gates, w_br_bf, i)
        xf = _out_ple(merged, xf, p3d, w_out_bf, vec(ple_norm_g), w_pg_bf, w_pp_bf,
                      final_norm_g, i, final=(i == depth - 1))
    return xf.reshape(bsz, seq, d)
```

```python
import functools
import math

import jax
import jax.numpy as jnp
import numpy as np
from jax import lax
from jax.experimental import pallas as pl
from jax.experimental.pallas import tpu as pltpu

F32 = jnp.float32
BF16 = jnp.bfloat16

D_MODEL = 2048
PLE_DIM = 256
BRANCH_WIDTH = 1024
A_HEADS = 8
A_HEAD_DIM = 128
MOBA_BLOCK = 256
MOBA_TOPK = 3
B_Q_HEADS = 16
B_KV_HEADS = 4
B_GROUP = B_Q_HEADS // B_KV_HEADS
B_HEAD_DIM = 64
WINDOW = 128
C_BLOCKS = 8
C_BLOCK_DIM = 128
CONV_WIDTH = 4
LRU_C = 8.0
RPE_BUCKETS = 32
RPE_MAX_DIST = 128
EPS = 1e-6
NEG = -1e30
LOG2E = math.log2(math.e)

COL_QA = 0
COL_KA = COL_QA + BRANCH_WIDTH
COL_VA = COL_KA + BRANCH_WIDTH
COL_GA = COL_VA + BRANCH_WIDTH
COL_QB = COL_GA + BRANCH_WIDTH
COL_KB = COL_QB + BRANCH_WIDTH
COL_VB = COL_KB + B_KV_HEADS * B_HEAD_DIM
COL_GB = COL_VB + B_KV_HEADS * B_HEAD_DIM
COL_XC = COL_GB + BRANCH_WIDTH
COL_GC = COL_XC + BRANCH_WIDTH
COL_MG = COL_GC + BRANCH_WIDTH
IN_WIDTH = COL_MG + 3 * D_MODEL

PROJ_TM = 2048
PROJ_TN = 512
MERGE_TM = 2048
MERGE_TN = 512
PLE_TM = 512
LRU_CHUNK = 256

VMEM_LIMIT = 62 * 1024 * 1024


def _params(sem):
    return pltpu.CompilerParams(dimension_semantics=sem, vmem_limit_bytes=VMEM_LIMIT)


def _rms(xf, g):
    return xf * lax.rsqrt(jnp.mean(xf * xf, axis=-1, keepdims=True) + EPS) * g


def _in_proj_kernel(x_ref, g_ref, w_ref, o_ref, mg_ref, h_sc):
    @pl.when(pl.program_id(1) == 0)
    def _():
        h_sc[...] = _rms(x_ref[...], g_ref[...]).astype(BF16)

    acc = jnp.dot(h_sc[...], w_ref[...], preferred_element_type=F32)
    o_ref[...] = acc
    mg_ref[...] = acc.astype(mg_ref.dtype)


def _in_proj(x2d, g, w_bf, layer):
    t, d = x2d.shape
    n = w_bf.shape[2]
    assert COL_MG % PROJ_TN == 0
    main_tiles = COL_MG // PROJ_TN
    return pl.pallas_call(
        _in_proj_kernel,
        grid=(t // PROJ_TM, n // PROJ_TN),
        in_specs=[
            pl.BlockSpec((PROJ_TM, d), lambda i, j: (i, 0)),
            pl.BlockSpec((None, 1, d), lambda i, j: (layer, 0, 0)),
            pl.BlockSpec((None, d, PROJ_TN), lambda i, j: (layer, 0, j)),
        ],
        out_specs=[
            pl.BlockSpec((PROJ_TM, PROJ_TN), lambda i, j: (i, jnp.minimum(j, main_tiles))),
            pl.BlockSpec((PROJ_TM, PROJ_TN), lambda i, j: (i, jnp.maximum(j - main_tiles, 0))),
        ],
        out_shape=[jax.ShapeDtypeStruct((t, COL_MG + PROJ_TN), F32),
                   jax.ShapeDtypeStruct((t, n - COL_MG), BF16)],
        scratch_shapes=[pltpu.VMEM((PROJ_TM, d), BF16)],
        compiler_params=_params(("parallel", "arbitrary")),
        name="in_proj",
    )(x2d, g, w_bf)


MASKED_BUCKET = RPE_BUCKETS


def _t5_bucket_np(dist):
    n = np.maximum(dist, 0)
    max_exact = RPE_BUCKETS // 2
    ratio = np.log(np.maximum(n, 1).astype(np.float32) / np.float32(max_exact)) / np.float32(
        math.log(RPE_MAX_DIST / max_exact))
    large = max_exact + (ratio * np.float32(RPE_BUCKETS - max_exact)).astype(np.int32)
    large = np.minimum(large, RPE_BUCKETS - 1)
    return np.where(n < max_exact, n, large).astype(np.int32)


def _moba_bucket_planes():
    L = MOBA_BLOCK
    ki = np.arange(L)[:, None]
    qi = np.arange(L)[None, :]
    d_own = qi - ki
    d_far = 2 * L + qi - ki
    assert d_far.min() >= RPE_MAX_DIST
    own = np.where(d_own >= 0, _t5_bucket_np(d_own), MASKED_BUCKET)
    return np.stack([own, _t5_bucket_np(L + qi - ki), _t5_bucket_np(d_far)]).astype(np.int32)


def _swa_bucket_planes():
    W = WINDOW
    kk = np.arange(2 * W)[:, None]
    qi = np.arange(W)[None, :]
    dist = qi - kk + W
    ok = (dist >= 0) & (dist < W)
    b = _t5_bucket_np(dist)
    return np.stack([np.where(ok, b, MASKED_BUCKET),
                     np.where(ok & (kk >= W), b, MASKED_BUCKET)]).astype(np.int32)


def _bias_lookup_kernel(tab_ref, idx_ref, o_ref):
    h = pl.program_id(0)
    idx = idx_ref[0]
    acc = jnp.full(idx.shape, NEG, F32)
    for b in range(RPE_BUCKETS):
        acc = jnp.where(idx == b, tab_ref[h * RPE_BUCKETS + b], acc)
    o_ref[0, 0] = acc


def _bias_lookup(table_hb, planes, heads_per_row):
    nh = table_hb.shape[0]
    npl, r, c = planes.shape
    return pl.pallas_call(
        _bias_lookup_kernel,
        grid=(nh, npl),
        in_specs=[
            pl.BlockSpec(memory_space=pltpu.SMEM),
            pl.BlockSpec((1, r, c), lambda h, p: (p, 0, 0)),
        ],
        out_specs=pl.BlockSpec((1, 1, r, c),
                               lambda h, p: (h // heads_per_row, p, 0, h % heads_per_row)),
        out_shape=jax.ShapeDtypeStruct((nh // heads_per_row, npl, r, c * heads_per_row), F32),
        compiler_params=_params(("parallel", "parallel")),
        name="bias_lookup",
    )(table_hb.reshape(-1), jnp.asarray(planes))


GATE_ROWS = 16


def _silu(x):
    return x * jax.nn.sigmoid(x)


MOBA_GROUP = 2
MOBA_UNROLL = 2
MOBA_HEADS = 2


def _moba_prep_kernel(q_ref, k_ref, v_ref, kaug_ref, vt_ref, qaug_ref, kmean_sc, *, nb):
    L = MOBA_BLOCK
    dh = A_HEAD_DIM
    G = MOBA_GROUP

    lanes = lax.broadcasted_iota(jnp.int32, (L, dh), 1)
    kmean_sc[...] = jnp.zeros_like(kmean_sc)
    for n in range(nb):
        rows = slice((n % G) * L, (n % G + 1) * L)
        kn = k_ref[n * L:(n + 1) * L, :]
        kaug_ref[0, n // G, rows, :dh] = kn.astype(BF16)
        kaug_ref[0, n // G, rows, dh:] = jnp.where(lanes == n, 1.0, 0.0).astype(BF16)
        kmean_sc[n:n + 1, :] = jnp.sum(kn, axis=0, keepdims=True) * (1.0 / L)
        vt_ref[0, n // G, :, rows] = v_ref[n * L:(n + 1) * L, :].T.astype(BF16)

    blk = lax.broadcasted_iota(jnp.int32, (GATE_ROWS, L), 0)
    blk_f = blk.astype(F32)
    for j in range(nb):
        qt = q_ref[j * L:(j + 1) * L, :].T
        gate = jnp.dot(kmean_sc[...], qt, precision=lax.Precision.HIGHEST,
                       preferred_element_type=F32)
        gm = jnp.where(blk < j, gate, NEG)
        picked = jnp.zeros((GATE_ROWS, L), F32)
        for _ in range(MOBA_TOPK):
            mx = jnp.max(gm, axis=0, keepdims=True)
            first = jnp.min(jnp.where(gm == mx, blk_f, float(GATE_ROWS)), axis=0, keepdims=True)
            hit = blk_f == first
            picked = jnp.where(hit, 1.0, picked)
            gm = jnp.where(hit, -jnp.inf, gm)
        allowed = ((picked > 0.0) & (blk < j)) | (blk == j)
        qaug_ref[0, j, :dh, :] = (qt * (dh ** -0.5 * LOG2E)).astype(BF16)
        qaug_ref[0, j, dh:dh + GATE_ROWS, :] = jnp.where(allowed, 0.0, NEG).astype(BF16)
        qaug_ref[0, j, dh + GATE_ROWS:, :] = jnp.zeros((dh - GATE_ROWS, L), BF16)


def _moba_prep(proj, bsz, seq):
    L = MOBA_BLOCK
    dh = A_HEAD_DIM
    G = MOBA_GROUP
    nb = seq // L
    nh = bsz * A_HEADS
    cq, ck, cv = (c // dh for c in (COL_QA, COL_KA, COL_VA))
    col = lambda c: pl.BlockSpec((seq, dh), lambda b, h: (b, c + h))
    shapes = [(nh, nb // G, G * L, 2 * dh), (nh, nb // G, dh, G * L), (nh, nb, 2 * dh, L)]
    return pl.pallas_call(
        functools.partial(_moba_prep_kernel, nb=nb),
        grid=(bsz, A_HEADS),
        in_specs=[col(cq), col(ck), col(cv)],
        out_specs=[pl.BlockSpec((1,) + s[1:], lambda b, h: (b * A_HEADS + h, 0, 0, 0))
                   for s in shapes],
        out_shape=[jax.ShapeDtypeStruct(s, BF16) for s in shapes],
        scratch_shapes=[pltpu.VMEM((GATE_ROWS, dh), F32)],
        compiler_params=_params(("parallel", "parallel")),
        name="moba_prep",
    )(proj, proj, proj)


def _moba_schedule(nb):
    G = MOBA_GROUP
    return [(j, i) for j in range(nb) for i in range((j + G) // G)]


def _moba_kernel(jt_ref, it_ref, kaug_ref, vt_ref, qaug_ref, g_ref, bias_ref, o_ref,
                 s_sc, p_sc, alpha_sc, lfin_sc, lout_sc, m_sc, l_sc, acc_sc, *, sched):
    L = MOBA_BLOCK
    dh = A_HEAD_DIM
    G = MOBA_GROUP
    U = MOBA_UNROLL
    nt = len(sched)
    heads = range(MOBA_HEADS)
    hcol = lambda u: slice(u * dh, (u + 1) * dh)
    last_group = lambda j: j // G

    def scores(u, t, slot):
        s_sc[u, slot] = jnp.dot(kaug_ref[u, it_ref[t]], qaug_ref[u, jt_ref[t]],
                                preferred_element_type=F32)

    def softmax(u, t, slot):
        j = jt_ref[t]
        i = it_ref[t]
        s = s_sc[u, slot]
        parts = []
        for g in range(G):
            n = i * G + g
            plane = jnp.where(n == j, 0, jnp.where(n == j - 1, 1, 2))
            parts.append(s[g * L:(g + 1) * L, :] + bias_ref[u, plane])
        m_old = jnp.where(i == 0, NEG, m_sc[u])
        m_new = m_old
        for sp in parts:
            m_new = jnp.maximum(m_new, jnp.max(sp, axis=0, keepdims=True))
        alpha = jnp.exp2(m_old - m_new)
        probs = [jnp.exp2(sp - m_new) for sp in parts]
        l_new = alpha * l_sc[u]
        for pr in probs:
            l_new = l_new + jnp.sum(pr, axis=0, keepdims=True)
        p_sc[u, slot] = jnp.concatenate(probs, axis=0).astype(BF16)
        alpha_sc[u, slot] = alpha
        lfin_sc[u, slot] = l_new
        l_sc[u] = l_new
        m_sc[u] = m_new

    def weighted_values(u, t, slot, r):
        acc_sc[u, r] = alpha_sc[u, slot] * acc_sc[u, (r - 1) % U] + jnp.dot(
            vt_ref[u, it_ref[t]], p_sc[u, slot], preferred_element_type=F32)
        lout_sc[u, r] = lfin_sc[u, slot]

    def finish(u, j, r):
        out_t = acc_sc[u, r] / lout_sc[u, r]
        rows = pl.ds(pl.multiple_of(j * L, L), L)
        o_ref[rows, hcol(u)] = (out_t.T * _silu(g_ref[rows, hcol(u)])).astype(o_ref.dtype)

    s_sc[:, 1] = jnp.full(s_sc.shape[:1] + s_sc.shape[2:], -jnp.inf, F32)
    p_sc[:, 0] = jnp.zeros(p_sc.shape[:1] + p_sc.shape[2:], BF16)
    alpha_sc[...] = jnp.ones_like(alpha_sc)
    lfin_sc[...] = jnp.ones_like(lfin_sc)
    m_sc[...] = jnp.full_like(m_sc, NEG)
    l_sc[...] = jnp.zeros_like(l_sc)
    acc_sc[...] = jnp.zeros_like(acc_sc)

    def unrolled_steps(k, carry):
        base = U * k
        done = []
        for r in range(U):
            t = base + r
            t_pv = jnp.maximum(t - 2, 0)
            t_sm = jnp.maximum(t - 1, 0)
            for u in heads:
                weighted_values(u, t_pv, r % 2, r)
                softmax(u, t_sm, (r + 1) % 2)
                scores(u, t, r % 2)
            done.append((t - 2, t_pv, r))
        for t_real, t_pv, r in done:
            j_done = jt_ref[t_pv]

            @pl.when((t_real >= 0) & (it_ref[t_pv] == last_group(j_done)))
            def _():
                for u in heads:
                    finish(u, j_done, r)

        return carry

    assert U % 2 == 0
    n_loop = nt // U
    lax.fori_loop(0, n_loop, unrolled_steps, 0)
    for t in range(n_loop * U, nt + 2):
        r = t % U
        for u in heads:
            if 0 <= t - 2:
                weighted_values(u, t - 2, r % 2, r)
                j, i = sched[t - 2]
                if i == last_group(j):
                    finish(u, j, r)
            if 0 <= t - 1 < nt:
                softmax(u, t - 1, (r + 1) % 2)
            if t < nt:
                scores(u, t, r % 2)


def _moba_branch(proj, bias, bsz, seq):
    L = MOBA_BLOCK
    dh = A_HEAD_DIM
    G = MOBA_GROUP
    nb = seq // L
    hp = MOBA_HEADS
    assert min(MOBA_TOPK, nb - 1) == MOBA_TOPK and nb <= GATE_ROWS and nb % G == 0
    assert A_HEADS % hp == 0
    kaug, vt, qaug = _moba_prep(proj, bsz, seq)
    sched = _moba_schedule(nb)
    jt = jnp.asarray(np.array([j for j, _ in sched], np.int32))
    it = jnp.asarray(np.array([i for _, i in sched], np.int32))
    pairs = A_HEADS // hp
    cg = COL_GA // (hp * dh)
    smem = pl.BlockSpec(memory_space=pltpu.SMEM)
    per_pair = lambda a: pl.BlockSpec((hp,) + a.shape[1:], lambda b, h: (b * pairs + h, 0, 0, 0))
    return pl.pallas_call(
        functools.partial(_moba_kernel, sched=sched),
        grid=(bsz, pairs),
        in_specs=[smem, smem, per_pair(kaug), per_pair(vt), per_pair(qaug),
                  pl.BlockSpec((seq, hp * dh), lambda b, h: (b, cg + h)),
                  pl.BlockSpec((hp, 3, L, L), lambda b, h: (h, 0, 0, 0))],
        out_specs=pl.BlockSpec((seq, hp * dh), lambda b, h: (b, h)),
        out_shape=jax.ShapeDtypeStruct((bsz * seq, BRANCH_WIDTH), BF16),
        scratch_shapes=[
            pltpu.VMEM((hp, 2, G * L, L), F32),
            pltpu.VMEM((hp, 2, G * L, L), BF16),
            pltpu.VMEM((hp, 2, 1, L), F32),
            pltpu.VMEM((hp, 2, 1, L), F32),
            pltpu.VMEM((hp, MOBA_UNROLL, 1, L), F32),
            pltpu.VMEM((hp, 1, L), F32),
            pltpu.VMEM((hp, 1, L), F32),
            pltpu.VMEM((hp, MOBA_UNROLL, dh, L), F32),
        ],
        compiler_params=_params(("parallel", "parallel")),
        name="moba",
    )(jt, it, kaug, vt, qaug, proj, bias)


SWA_BLOCKS = 4


def _swa_kernel(q_ref, kp_ref, kc_ref, vp_ref, vc_ref, g0_ref, g1_ref, bias_ref, sink_ref, o_ref,
                ot_sc, x_sc, k_sc, vt_sc, s_sc, p_sc, d_sc):
    W = WINDOW
    dh = B_HEAD_DIM
    kvw = B_KV_HEADS * dh
    head = lambda g, h: slice((g * B_GROUP + h) * dh, (g * B_GROUP + h + 1) * dh)
    lane_group = lax.broadcasted_iota(jnp.int32, (2 * W, kvw), 1) // dh
    first = pl.program_id(1) == 0
    for u in range(SWA_BLOCKS):
        rows = slice(u * W, (u + 1) * W)
        if u == 0:
            k_prev, v_prev = kp_ref[...], vp_ref[...]
            plane = jnp.where(first, 1, 0)
        else:
            k_prev, v_prev = kc_ref[(u - 1) * W:u * W, :], vc_ref[(u - 1) * W:u * W, :]
            plane = 0
        qt = q_ref[rows, :].T * (dh ** -0.5 * LOG2E)
        x_sc[u] = jnp.concatenate(
            [jnp.concatenate([qt[head(g, h), :] for g in range(B_KV_HEADS)], axis=0)
             for h in range(B_GROUP)], axis=1).astype(BF16)
        kband = jnp.concatenate([k_prev, kc_ref[rows, :]], axis=0)
        vt_sc[u] = jnp.concatenate([v_prev, vc_ref[rows, :]], axis=0).T.astype(BF16)
        for g in range(B_KV_HEADS):
            k_sc[u, g] = jnp.where(lane_group == g, kband, 0.0).astype(BF16)
        for g in range(B_KV_HEADS):
            s_sc[u, g] = jnp.dot(k_sc[u, g], x_sc[u], preferred_element_type=F32)
        for g in range(B_KV_HEADS):
            s = s_sc[u, g] + bias_ref[g, plane]
            sink = sink_ref[g]
            m = jnp.maximum(jnp.max(s, axis=0, keepdims=True), sink)
            p = jnp.exp2(s - m)
            d_sc[u, g] = jnp.sum(p, axis=0, keepdims=True) + jnp.exp2(sink - m)
            p_sc[u, g] = p.astype(BF16)
        for g in range(B_KV_HEADS):
            o = jnp.dot(vt_sc[u, g * dh:(g + 1) * dh, :], p_sc[u, g],
                        preferred_element_type=F32) / d_sc[u, g]
            for h in range(B_GROUP):
                ot_sc[u, head(g, h), :] = o[:, h * W:(h + 1) * W]
        gate = jnp.concatenate([g0_ref[rows, :], g1_ref[rows, :]], axis=1)
        o_ref[rows, :] = (ot_sc[u].T * _silu(gate)).astype(o_ref.dtype)


def _swa_branch(proj, bias, sink, bsz, seq):
    W = WINDOW
    R = SWA_BLOCKS * W
    nb = seq // W
    ns = seq // R
    kvw = B_KV_HEADS * B_HEAD_DIM
    cq = COL_QB // BRANCH_WIDTH
    ck = COL_KB // kvw
    cv = COL_VB // kvw
    cg = COL_GB // (BRANCH_WIDTH // 2)
    assert COL_QB % BRANCH_WIDTH == 0 and COL_KB % kvw == 0 and COL_VB % kvw == 0
    assert COL_GB % (BRANCH_WIDTH // 2) == 0 and seq % R == 0
    prev = lambda b, n: (b * nb + jnp.maximum(SWA_BLOCKS * n - 1, 0))
    cur = lambda c, w: pl.BlockSpec((R, w), lambda b, n: (b * ns + n, c))
    return pl.pallas_call(
        _swa_kernel,
        grid=(bsz, ns),
        in_specs=[
            cur(cq, BRANCH_WIDTH),
            pl.BlockSpec((W, kvw), lambda b, n: (prev(b, n), ck)),
            cur(ck, kvw),
            pl.BlockSpec((W, kvw), lambda b, n: (prev(b, n), cv)),
            cur(cv, kvw),
            cur(cg, BRANCH_WIDTH // 2),
            cur(cg + 1, BRANCH_WIDTH // 2),
            pl.BlockSpec(bias.shape, lambda b, n: (0, 0, 0, 0)),
            pl.BlockSpec(sink.shape, lambda b, n: (0, 0, 0)),
        ],
        out_specs=cur(0, BRANCH_WIDTH),
        out_shape=jax.ShapeDtypeStruct((bsz * seq, BRANCH_WIDTH), BF16),
        scratch_shapes=[
            pltpu.VMEM((SWA_BLOCKS, BRANCH_WIDTH, W), F32),
            pltpu.VMEM((SWA_BLOCKS, kvw, B_GROUP * W), BF16),
            pltpu.VMEM((SWA_BLOCKS, B_KV_HEADS, 2 * W, kvw), BF16),
            pltpu.VMEM((SWA_BLOCKS, kvw, 2 * W), BF16),
            pltpu.VMEM((SWA_BLOCKS, B_KV_HEADS, 2 * W, B_GROUP * W), F32),
            pltpu.VMEM((SWA_BLOCKS, B_KV_HEADS, 2 * W, B_GROUP * W), BF16),
            pltpu.VMEM((SWA_BLOCKS, B_KV_HEADS, 1, B_GROUP * W), F32),
        ],
        compiler_params=_params(("parallel", "arbitrary")),
        name="swa",
    )(proj, proj, proj, proj, proj, proj, proj, bias, sink)


def _swa_sink_rows(sinks):
    s = (sinks.astype(F32) * LOG2E).reshape(B_KV_HEADS, 1, B_GROUP, 1)
    return jnp.broadcast_to(s, (B_KV_HEADS, 1, B_GROUP, WINDOW)).reshape(
        B_KV_HEADS, 1, B_GROUP * WINDOW)


SUBLANES = 8


def _scan_steps(a, b, index, length, axis):
    k = 1
    while k < length:
        keep = index >= k
        a_prev = jnp.where(keep, pltpu.roll(a, k, axis), 1.0)
        b_prev = jnp.where(keep, pltpu.roll(b, k, axis), 0.0)
        b = a * b_prev + b
        a = a * a_prev
        k *= 2
    return a, b


LRU_BLOCKS = 2


def _rglru_kernel(xc_ref, gc_ref, cw_ref, cb_ref, wr_ref, br_ref, wi_ref, bi_ref, lam_ref, o_ref,
                  xpad_sc, h_sc, a_sc, b_sc, hin_sc, *, seq):
    T = LRU_CHUNK
    C = C_BLOCK_DIM
    PAD = 8
    blocks = range(LRU_BLOCKS)
    ch = lambda u: slice(u * C, (u + 1) * C)
    for u in blocks:
        xpad_sc[u, :PAD, :] = jnp.zeros((PAD, C), F32)
        xpad_sc[u, PAD:, :] = xc_ref[:, ch(u)]
    h_sc[...] = jnp.zeros_like(h_sc)
    nlam = -lam_ref[...]
    softplus = jnp.maximum(nlam, 0.0) + jnp.log1p(jnp.exp(-jnp.abs(nlam)))
    decay = -LRU_C * softplus
    tiles = T // SUBLANES
    sub = lax.broadcasted_iota(jnp.int32, (tiles, SUBLANES, C), 1)
    tile_row = lax.broadcasted_iota(jnp.int32, (tiles, C), 0)

    def chunk(c, carry):
        t0 = pl.multiple_of(c * T, T)
        for u in blocks:
            xw = xpad_sc[u, pl.ds(t0, T + PAD), :]
            conv = cb_ref[:, ch(u)]
            for w in range(CONV_WIDTH):
                off = PAD - (CONV_WIDTH - 1) + w
                conv = conv + cw_ref[w:w + 1, ch(u)] * xw[off:off + T, :]
            cbf = conv.astype(BF16)
            r = jax.nn.sigmoid(jnp.dot(cbf, wr_ref[u], preferred_element_type=F32)
                               + br_ref[:, ch(u)])
            i = jax.nn.sigmoid(jnp.dot(cbf, wi_ref[u], preferred_element_type=F32)
                               + bi_ref[:, ch(u)])
            log_a = r * decay[:, ch(u)]
            a = jnp.exp(log_a)
            b = jnp.sqrt(jnp.maximum(-jnp.tanh(log_a) * (a * a + 1.0), 0.0)) * (i * conv)
            a, b = _scan_steps(a.reshape(tiles, SUBLANES, C), b.reshape(tiles, SUBLANES, C),
                               sub, SUBLANES, 1)
            a_sc[u] = a.reshape(T, C)
            b_sc[u] = b.reshape(T, C)
            last = pl.ds(SUBLANES - 1, tiles, stride=SUBLANES)
            a_tile, b_tile = _scan_steps(a_sc[u, last, :], b_sc[u, last, :], tile_row, tiles, 0)
            h_out = a_tile * h_sc[u] + b_tile
            hin_sc[u] = jnp.where(tile_row >= 1, pltpu.roll(h_out, 1, 0), h_sc[u])
            h_sc[u] = h_out[tiles - 1:tiles, :]
            h_in = jnp.concatenate(
                [jnp.broadcast_to(hin_sc[u, t:t + 1, :], (SUBLANES, C)) for t in range(tiles)],
                axis=0)
            h = a_sc[u] * h_in + b_sc[u]
            o_ref[pl.ds(t0, T), ch(u)] = (h * _silu(gc_ref[pl.ds(t0, T), ch(u)])).astype(
                o_ref.dtype)
        return carry

    lax.fori_loop(0, seq // T, chunk, 0)


def _rglru_branch(proj, conv_w, conv_b, w_r_bf, b_r, w_i_bf, b_i, lam, layer, bsz, seq):
    nbk = LRU_BLOCKS
    cd = C_BLOCK_DIM
    wd = nbk * cd
    assert C_BLOCKS % nbk == 0 and COL_XC % wd == 0 and COL_GC % wd == 0
    cx = COL_XC // wd
    cg = COL_GC // wd
    vspec = pl.BlockSpec((None, 1, wd), lambda b, c: (layer, 0, c))
    wspec = pl.BlockSpec((None, nbk, cd, cd), lambda b, c: (layer, c, 0, 0))
    return pl.pallas_call(
        functools.partial(_rglru_kernel, seq=seq),
        grid=(bsz, C_BLOCKS // nbk),
        in_specs=[
            pl.BlockSpec((seq, wd), lambda b, c: (b, cx + c)),
            pl.BlockSpec((seq, wd), lambda b, c: (b, cg + c)),
            pl.BlockSpec((None, CONV_WIDTH, wd), lambda b, c: (layer, 0, c)),
            vspec, wspec, vspec, wspec, vspec, vspec,
        ],
        out_specs=pl.BlockSpec((seq, wd), lambda b, c: (b, c)),
        out_shape=jax.ShapeDtypeStruct((bsz * seq, BRANCH_WIDTH), BF16),
        scratch_shapes=[pltpu.VMEM((nbk, seq + 8, cd), F32), pltpu.VMEM((nbk, 1, cd), F32),
                        pltpu.VMEM((nbk, LRU_CHUNK, cd), F32),
                        pltpu.VMEM((nbk, LRU_CHUNK, cd), F32),
                        pltpu.VMEM((nbk, LRU_CHUNK // SUBLANES, cd), F32)],
        compiler_params=_params(("parallel", "parallel")),
        name="rglru",
    )(proj, proj, conv_w, conv_b, w_r_bf, b_r, w_i_bf, b_i, lam)


def _merge_kernel(ya_ref, yb_ref, yc_ref, ma_ref, mb_ref, mc_ref, wbr_ref, o_ref):
    merged = None
    for n, (y_ref, m_ref) in enumerate(((ya_ref, ma_ref), (yb_ref, mb_ref), (yc_ref, mc_ref))):
        term = jax.nn.sigmoid(m_ref[...].astype(F32)) * jnp.dot(y_ref[...], wbr_ref[n],
                                                                preferred_element_type=F32)
        merged = term if merged is None else merged + term
    o_ref[...] = merged.astype(o_ref.dtype)


def _merge(ya, yb, yc, gates, wbr_bf, layer):
    t = ya.shape[0]
    d = wbr_bf.shape[3]
    tm, tn = MERGE_TM, MERGE_TN
    assert d % tn == 0
    cm = 0
    per = d // tn
    yspec = pl.BlockSpec((tm, BRANCH_WIDTH), lambda i, c: (i, 0))
    mspec = lambda n: pl.BlockSpec((tm, tn), lambda i, c: (i, cm + n * per + c))
    return pl.pallas_call(
        _merge_kernel,
        grid=(t // tm, per),
        in_specs=[
            yspec, yspec, yspec, mspec(0), mspec(1), mspec(2),
            pl.BlockSpec((None, 3, BRANCH_WIDTH, tn), lambda i, c: (layer, 0, 0, c)),
        ],
        out_specs=pl.BlockSpec((tm, tn), lambda i, c: (i, c)),
        out_shape=jax.ShapeDtypeStruct((t, d), BF16),
        compiler_params=_params(("parallel", "arbitrary")),
        name="merge",
    )(ya, yb, yc, gates, gates, gates, wbr_bf)


def _out_ple_kernel(mg_ref, x_ref, p_ref, wout_ref, g_ref, wpg_ref, wpp_ref, fg_ref, o_ref, *,
                    final):
    x = x_ref[...] + jnp.dot(mg_ref[...], wout_ref[...], preferred_element_type=F32)
    h = _rms(x, g_ref[...]).astype(BF16)
    gate = jax.nn.sigmoid(jnp.dot(h, wpg_ref[...], preferred_element_type=F32))
    emb = jnp.dot(p_ref[...].astype(BF16), wpp_ref[...], preferred_element_type=F32)
    y = x + gate * emb
    if final:
        y = _rms(y, fg_ref[...])
    o_ref[...] = y


def _out_ple(merged, x2d, p3d, wout_bf, g, wpg_bf, wpp_bf, final_g, layer, final):
    t, d = x2d.shape
    tm = PLE_TM
    wspec = lambda r: pl.BlockSpec((None, r, d), lambda i: (layer, 0, 0),
                                   pipeline_mode=pl.Buffered(1))
    row = lambda w: pl.BlockSpec((tm, w), lambda i: (i, 0))
    return pl.pallas_call(
        functools.partial(_out_ple_kernel, final=final),
        grid=(t // tm,),
        in_specs=[row(d), row(d), pl.BlockSpec((None, tm, PLE_DIM), lambda i: (layer, i, 0)),
                  wspec(d), pl.BlockSpec((None, 1, d), lambda i: (layer, 0, 0)), wspec(d),
                  wspec(PLE_DIM), pl.BlockSpec((1, d), lambda i: (0, 0))],
        out_specs=row(d),
        out_shape=jax.ShapeDtypeStruct((t, d), F32),
        compiler_params=_params(("parallel",)),
        name="out_ple",
    )(merged, x2d, p3d, wout_bf, g, wpg_bf, wpp_bf, final_g.reshape(1, d))


def kernel(x, p, rpe_table, norm_g, w_in, sinks, conv_w, conv_b, w_r, b_r, w_i, b_i, lam, w_br,
           w_out, ple_norm_g, w_pg, w_pp, final_norm_g):
    bsz, seq, d = x.shape
    depth = w_in.shape[0]
    assert d == D_MODEL and w_in.shape[2] == IN_WIDTH
    assert seq % MOBA_BLOCK == 0 and seq % LRU_CHUNK == 0 and (bsz * seq) % PROJ_TM == 0
    t = bsz * seq
    xf = x.reshape(t, d)
    table = rpe_table.astype(F32).T * LOG2E
    bias_a = _bias_lookup(table[:A_HEADS], _moba_bucket_planes(), 1)
    bias_b = _bias_lookup(table[A_HEADS:], _swa_bucket_planes(), B_GROUP)
    bf = lambda w: w.astype(BF16)
    vec = lambda v: v.reshape(depth, 1, v.shape[-1])
    w_in_bf, w_r_bf, w_i_bf, w_br_bf, w_out_bf, w_pg_bf, w_pp_bf = map(
        bf, (w_in, w_r, w_i, w_br, w_out, w_pg, w_pp))
    p3d = p.reshape(depth, t, PLE_DIM)
    for i in range(depth):
        proj, gates = _in_proj(xf, vec(norm_g), w_in_bf, i)
        ya = _moba_branch(proj, bias_a, bsz, seq)
        yb = _swa_branch(proj, bias_b, _swa_sink_rows(sinks[i]), bsz, seq)
        yc = _rglru_branch(proj, conv_w, vec(conv_b), w_r_bf, vec(b_r), w_i_bf, vec(b_i),
                           vec(lam), i, bsz, seq)
        merged = _merge(ya, yb, yc, gates, w_br_bf, i)
        xf = _out_ple(merged, xf, p3d, w_out_bf, vec(ple_norm_g), w_pg_bf, w_pp_bf,
                      final_norm_g, i, final=(i == depth - 1))
    return xf.reshape(bsz, seq, d)
```

```python
import functools
import math

import jax
import jax.numpy as jnp
import numpy as np
from jax import lax
from jax.experimental import pallas as pl
from jax.experimental.pallas import tpu as pltpu

F32 = jnp.float32
BF16 = jnp.bfloat16

D_MODEL = 2048
PLE_DIM = 256
BRANCH_WIDTH = 1024
A_HEADS = 8
A_HEAD_DIM = 128
MOBA_BLOCK = 256
MOBA_TOPK = 3
B_Q_HEADS = 16
B_KV_HEADS = 4
B_GROUP = B_Q_HEADS // B_KV_HEADS
B_HEAD_DIM = 64
WINDOW = 128
C_BLOCKS = 8
C_BLOCK_DIM = 128
CONV_WIDTH = 4
LRU_C = 8.0
RPE_BUCKETS = 32
RPE_MAX_DIST = 128
EPS = 1e-6
NEG = -1e30
LOG2E = math.log2(math.e)

COL_QA = 0
COL_KA = COL_QA + BRANCH_WIDTH
COL_VA = COL_KA + BRANCH_WIDTH
COL_GA = COL_VA + BRANCH_WIDTH
COL_QB = COL_GA + BRANCH_WIDTH
COL_KB = COL_QB + BRANCH_WIDTH
COL_VB = COL_KB + B_KV_HEADS * B_HEAD_DIM
COL_GB = COL_VB + B_KV_HEADS * B_HEAD_DIM
COL_XC = COL_GB + BRANCH_WIDTH
COL_GC = COL_XC + BRANCH_WIDTH
COL_MG = COL_GC + BRANCH_WIDTH
IN_WIDTH = COL_MG + 3 * D_MODEL

PROJ_TM = 2048
PROJ_TN = 512
MERGE_TM = 2048
MERGE_TN = 512
PLE_TM = 512
LRU_CHUNK = 256

VMEM_LIMIT = 62 * 1024 * 1024


def _params(sem):
    return pltpu.CompilerParams(dimension_semantics=sem, vmem_limit_bytes=VMEM_LIMIT)


def _rms(xf, g):
    return xf * lax.rsqrt(jnp.mean(xf * xf, axis=-1, keepdims=True) + EPS) * g


def _in_proj_kernel(x_ref, g_ref, w_ref, o_ref, mg_ref, h_sc):
    @pl.when(pl.program_id(1) == 0)
    def _():
        h_sc[...] = _rms(x_ref[...], g_ref[...]).astype(BF16)

    acc = jnp.dot(h_sc[...], w_ref[...], preferred_element_type=F32)
    o_ref[...] = acc
    mg_ref[...] = acc.astype(mg_ref.dtype)


def _in_proj(x2d, g, w_bf, layer, w_layer):
    t, d = x2d.shape
    n = w_bf.shape[2]
    assert COL_MG % PROJ_TN == 0
    main_tiles = COL_MG // PROJ_TN
    return pl.pallas_call(
        _in_proj_kernel,
        grid=(t // PROJ_TM, n // PROJ_TN),
        in_specs=[
            pl.BlockSpec((PROJ_TM, d), lambda i, j: (i, 0)),
            pl.BlockSpec((None, 1, d), lambda i, j: (layer, 0, 0)),
            pl.BlockSpec((None, d, PROJ_TN), lambda i, j: (w_layer, 0, j)),
        ],
        out_specs=[
            pl.BlockSpec((PROJ_TM, PROJ_TN), lambda i, j: (i, jnp.minimum(j, main_tiles))),
            pl.BlockSpec((PROJ_TM, PROJ_TN), lambda i, j: (i, jnp.maximum(j - main_tiles, 0))),
        ],
        out_shape=[jax.ShapeDtypeStruct((t, COL_MG + PROJ_TN), F32),
                   jax.ShapeDtypeStruct((t, n - COL_MG), BF16)],
        scratch_shapes=[pltpu.VMEM((PROJ_TM, d), BF16)],
        compiler_params=_params(("parallel", "arbitrary")),
        name="in_proj",
    )(x2d, g, w_bf)


MASKED_BUCKET = RPE_BUCKETS


def _t5_bucket_np(dist):
    n = np.maximum(dist, 0)
    max_exact = RPE_BUCKETS // 2
    ratio = np.log(np.maximum(n, 1).astype(np.float32) / np.float32(max_exact)) / np.float32(
        math.log(RPE_MAX_DIST / max_exact))
    large = max_exact + (ratio * np.float32(RPE_BUCKETS - max_exact)).astype(np.int32)
    large = np.minimum(large, RPE_BUCKETS - 1)
    return np.where(n < max_exact, n, large).astype(np.int32)


def _moba_bucket_planes():
    L = MOBA_BLOCK
    ki = np.arange(L)[:, None]
    qi = np.arange(L)[None, :]
    d_own = qi - ki
    d_far = 2 * L + qi - ki
    assert d_far.min() >= RPE_MAX_DIST
    own = np.where(d_own >= 0, _t5_bucket_np(d_own), MASKED_BUCKET)
    return np.stack([own, _t5_bucket_np(L + qi - ki), _t5_bucket_np(d_far)]).astype(np.int32)


def _swa_bucket_planes():
    W = WINDOW
    kk = np.arange(2 * W)[:, None]
    qi = np.arange(W)[None, :]
    dist = qi - kk + W
    ok = (dist >= 0) & (dist < W)
    b = _t5_bucket_np(dist)
    return np.stack([np.where(ok, b, MASKED_BUCKET),
                     np.where(ok & (kk >= W), b, MASKED_BUCKET)]).astype(np.int32)


def _bias_lookup_kernel(tab_ref, idx_ref, o_ref):
    h = pl.program_id(0)
    for p in range(idx_ref.shape[0]):
        idx = idx_ref[p]
        acc = jnp.full(idx.shape, NEG, F32)
        for b in range(RPE_BUCKETS):
            acc = jnp.where(idx == b, tab_ref[h * RPE_BUCKETS + b], acc)
        o_ref[0, p] = acc


def _bias_lookup(table_hb, planes, heads_per_row):
    nh = table_hb.shape[0]
    npl, r, c = planes.shape
    return pl.pallas_call(
        _bias_lookup_kernel,
        grid=(nh,),
        in_specs=[
            pl.BlockSpec(memory_space=pltpu.SMEM),
            pl.BlockSpec((npl, r, c), lambda h: (0, 0, 0)),
        ],
        out_specs=pl.BlockSpec((1, npl, r, c),
                               lambda h: (h // heads_per_row, 0, 0, h % heads_per_row)),
        out_shape=jax.ShapeDtypeStruct((nh // heads_per_row, npl, r, c * heads_per_row), F32),
        compiler_params=_params(("parallel",)),
        name="bias_lookup",
    )(table_hb.reshape(-1), jnp.asarray(planes))


GATE_ROWS = 16


def _silu(x):
    return x * jax.nn.sigmoid(x)


MOBA_GROUP = 2
MOBA_UNROLL = 2
SOFTMAX_ROWS = 128
MOBA_HEADS = 2


def _moba_prep_kernel(q_ref, k_ref, v_ref, kaug_ref, vt_ref, qaug_ref, kmean_sc, *, nb):
    L = MOBA_BLOCK
    dh = A_HEAD_DIM
    G = MOBA_GROUP

    lanes = lax.broadcasted_iota(jnp.int32, (L, dh), 1)
    kmean_sc[...] = jnp.zeros_like(kmean_sc)
    for n in range(nb):
        rows = slice((n % G) * L, (n % G + 1) * L)
        kn = k_ref[n * L:(n + 1) * L, :]
        kaug_ref[0, n // G, rows, :dh] = kn.astype(BF16)
        kaug_ref[0, n // G, rows, dh:] = jnp.where(lanes == n, 1.0, 0.0).astype(BF16)
        kmean_sc[n:n + 1, :] = jnp.sum(kn, axis=0, keepdims=True) * (1.0 / L)
        vt_ref[0, n // G, :, rows] = v_ref[n * L:(n + 1) * L, :].T.astype(BF16)

    blk = lax.broadcasted_iota(jnp.int32, (GATE_ROWS, L), 0)
    blk_f = blk.astype(F32)
    for j in range(nb):
        qt = q_ref[j * L:(j + 1) * L, :].T
        gate = jnp.dot(kmean_sc[...], qt, precision=lax.Precision.HIGHEST,
                       preferred_element_type=F32)
        gm = jnp.where(blk < j, gate, NEG)
        picked = jnp.zeros((GATE_ROWS, L), F32)
        for _ in range(MOBA_TOPK):
            mx = jnp.max(gm, axis=0, keepdims=True)
            first = jnp.min(jnp.where(gm == mx, blk_f, float(GATE_ROWS)), axis=0, keepdims=True)
            hit = blk_f == first
            picked = jnp.where(hit, 1.0, picked)
            gm = jnp.where(hit, -jnp.inf, gm)
        allowed = ((picked > 0.0) & (blk < j)) | (blk == j)
        qaug_ref[0, j, :dh, :] = (qt * (dh ** -0.5 * LOG2E)).astype(BF16)
        qaug_ref[0, j, dh:dh + GATE_ROWS, :] = jnp.where(allowed, 0.0, NEG).astype(BF16)
        qaug_ref[0, j, dh + GATE_ROWS:, :] = jnp.zeros((dh - GATE_ROWS, L), BF16)


def _moba_prep(proj, bsz, seq):
    L = MOBA_BLOCK
    dh = A_HEAD_DIM
    G = MOBA_GROUP
    nb = seq // L
    nh = bsz * A_HEADS
    cq, ck, cv = (c // dh for c in (COL_QA, COL_KA, COL_VA))
    col = lambda c: pl.BlockSpec((seq, dh), lambda b, h: (b, c + h))
    shapes = [(nh, nb // G, G * L, 2 * dh), (nh, nb // G, dh, G * L), (nh, nb, 2 * dh, L)]
    return pl.pallas_call(
        functools.partial(_moba_prep_kernel, nb=nb),
        grid=(bsz, A_HEADS),
        in_specs=[col(cq), col(ck), col(cv)],
        out_specs=[pl.BlockSpec((1,) + s[1:], lambda b, h: (b * A_HEADS + h, 0, 0, 0))
                   for s in shapes],
        out_shape=[jax.ShapeDtypeStruct(s, BF16) for s in shapes],
        scratch_shapes=[pltpu.VMEM((GATE_ROWS, dh), F32)],
        compiler_params=_params(("parallel", "parallel")),
        name="moba_prep",
    )(proj, proj, proj)


def _moba_schedule(nb):
    G = MOBA_GROUP
    return [(j, i) for j in range(nb) for i in range((j + G) // G)]


def _moba_kernel(jt_ref, it_ref, kaug_ref, vt_ref, qaug_ref, g_ref, bias_ref, o_ref,
                 s_sc, p_sc, alpha_sc, lfin_sc, lout_sc, m_sc, l_sc, acc_sc, *, sched):
    L = MOBA_BLOCK
    dh = A_HEAD_DIM
    G = MOBA_GROUP
    U = MOBA_UNROLL
    nt = len(sched)
    heads = range(MOBA_HEADS)
    hcol = lambda u: slice(u * dh, (u + 1) * dh)
    last_group = lambda j: j // G

    def scores(u, t, slot):
        s_sc[u, slot] = jnp.dot(kaug_ref[u, it_ref[t]], qaug_ref[u, jt_ref[t]],
                                preferred_element_type=F32)

    def softmax(u, t, slot):
        j = jt_ref[t]
        i = it_ref[t]
        m_old = jnp.where(i == 0, NEG, m_sc[u])
        m_new = m_old
        for g in range(G):
            n = i * G + g
            plane = jnp.where(n == j, 0, jnp.where(n == j - 1, 1, 2))
            rows = slice(g * L, (g + 1) * L)
            sp = s_sc[u, slot, rows, :] + bias_ref[u, plane]
            s_sc[u, slot, rows, :] = sp
            m_new = jnp.maximum(m_new, jnp.max(sp, axis=0, keepdims=True))
        alpha = jnp.exp2(m_old - m_new)
        l_new = alpha * l_sc[u]
        for c in range(G * L // SOFTMAX_ROWS):
            rows = slice(c * SOFTMAX_ROWS, (c + 1) * SOFTMAX_ROWS)
            pr = jnp.exp2(s_sc[u, slot, rows, :] - m_new)
            l_new = l_new + jnp.sum(pr, axis=0, keepdims=True)
            p_sc[u, slot, rows, :] = pr.astype(BF16)
        alpha_sc[u, slot] = alpha
        lfin_sc[u, slot] = l_new
        l_sc[u] = l_new
        m_sc[u] = m_new

    def weighted_values(u, t, slot, r):
        acc_sc[u, r] = alpha_sc[u, slot] * acc_sc[u, (r - 1) % U] + jnp.dot(
            vt_ref[u, it_ref[t]], p_sc[u, slot], preferred_element_type=F32)
        lout_sc[u, r] = lfin_sc[u, slot]

    def finish(u, j, r):
        out_t = acc_sc[u, r] / lout_sc[u, r]
        rows = pl.ds(pl.multiple_of(j * L, L), L)
        o_ref[rows, hcol(u)] = (out_t.T * _silu(g_ref[rows, hcol(u)])).astype(o_ref.dtype)

    s_sc[:, 1] = jnp.full(s_sc.shape[:1] + s_sc.shape[2:], -jnp.inf, F32)
    p_sc[:, 0] = jnp.zeros(p_sc.shape[:1] + p_sc.shape[2:], BF16)
    alpha_sc[...] = jnp.ones_like(alpha_sc)
    lfin_sc[...] = jnp.ones_like(lfin_sc)
    m_sc[...] = jnp.full_like(m_sc, NEG)
    l_sc[...] = jnp.zeros_like(l_sc)
    acc_sc[...] = jnp.zeros_like(acc_sc)

    def unrolled_steps(k, carry):
        base = U * k
        done = []
        for r in range(U):
            t = base + r
            t_pv = jnp.maximum(t - 2, 0)
            t_sm = jnp.maximum(t - 1, 0)
            for u in heads:
                weighted_values(u, t_pv, r % 2, r)
                softmax(u, t_sm, (r + 1) % 2)
                scores(u, t, r % 2)
            done.append((t - 2, t_pv, r))
        for t_real, t_pv, r in done:
            j_done = jt_ref[t_pv]

            @pl.when((t_real >= 0) & (it_ref[t_pv] == last_group(j_done)))
            def _():
                for u in heads:
                    finish(u, j_done, r)

        return carry

    assert U % 2 == 0
    n_loop = nt // U
    lax.fori_loop(0, n_loop, unrolled_steps, 0)
    for t in range(n_loop * U, nt + 2):
        r = t % U
        for u in heads:
            if 0 <= t - 2:
                weighted_values(u, t - 2, r % 2, r)
                j, i = sched[t - 2]
                if i == last_group(j):
                    finish(u, j, r)
            if 0 <= t - 1 < nt:
                softmax(u, t - 1, (r + 1) % 2)
            if t < nt:
                scores(u, t, r % 2)


def _moba_branch(proj, bias, bsz, seq):
    L = MOBA_BLOCK
    dh = A_HEAD_DIM
    G = MOBA_GROUP
    nb = seq // L
    hp = MOBA_HEADS
    assert min(MOBA_TOPK, nb - 1) == MOBA_TOPK and nb <= GATE_ROWS and nb % G == 0
    assert A_HEADS % hp == 0
    kaug, vt, qaug = _moba_prep(proj, bsz, seq)
    sched = _moba_schedule(nb)
    jt = jnp.asarray(np.array([j for j, _ in sched], np.int32))
    it = jnp.asarray(np.array([i for _, i in sched], np.int32))
    pairs = A_HEADS // hp
    cg = COL_GA // (hp * dh)
    smem = pl.BlockSpec(memory_space=pltpu.SMEM)
    per_pair = lambda a: pl.BlockSpec((hp,) + a.shape[1:], lambda b, h: (b * pairs + h, 0, 0, 0))
    return pl.pallas_call(
        functools.partial(_moba_kernel, sched=sched),
        grid=(bsz, pairs),
        in_specs=[smem, smem, per_pair(kaug), per_pair(vt), per_pair(qaug),
                  pl.BlockSpec((seq, hp * dh), lambda b, h: (b, cg + h)),
                  pl.BlockSpec((hp, 3, L, L), lambda b, h: (h, 0, 0, 0))],
        out_specs=pl.BlockSpec((seq, hp * dh), lambda b, h: (b, h)),
        out_shape=jax.ShapeDtypeStruct((bsz * seq, BRANCH_WIDTH), BF16),
        scratch_shapes=[
            pltpu.VMEM((hp, 2, G * L, L), F32),
            pltpu.VMEM((hp, 2, G * L, L), BF16),
            pltpu.VMEM((hp, 2, 1, L), F32),
            pltpu.VMEM((hp, 2, 1, L), F32),
            pltpu.VMEM((hp, MOBA_UNROLL, 1, L), F32),
            pltpu.VMEM((hp, 1, L), F32),
            pltpu.VMEM((hp, 1, L), F32),
            pltpu.VMEM((hp, MOBA_UNROLL, dh, L), F32),
        ],
        compiler_params=_params(("parallel", "parallel")),
        name="moba",
    )(jt, it, kaug, vt, qaug, proj, bias)


SWA_BLOCKS = 4


def _swa_kernel(q_ref, kp_ref, kc_ref, vp_ref, vc_ref, g0_ref, g1_ref, bias_ref, sink_ref, o_ref,
                ot_sc, x_sc, k_sc, vt_sc, s_sc, p_sc, d_sc):
    W = WINDOW
    dh = B_HEAD_DIM
    kvw = B_KV_HEADS * dh
    head = lambda g, h: slice((g * B_GROUP + h) * dh, (g * B_GROUP + h + 1) * dh)
    lane_group = lax.broadcasted_iota(jnp.int32, (2 * W, kvw), 1) // dh
    first = pl.program_id(1) == 0
    for u in range(SWA_BLOCKS):
        rows = slice(u * W, (u + 1) * W)
        if u == 0:
            k_prev, v_prev = kp_ref[...], vp_ref[...]
            plane = jnp.where(first, 1, 0)
        else:
            k_prev, v_prev = kc_ref[(u - 1) * W:u * W, :], vc_ref[(u - 1) * W:u * W, :]
            plane = 0
        qt = q_ref[rows, :].T * (dh ** -0.5 * LOG2E)
        x_sc[u] = jnp.concatenate(
            [jnp.concatenate([qt[head(g, h), :] for g in range(B_KV_HEADS)], axis=0)
             for h in range(B_GROUP)], axis=1).astype(BF16)
        kband = jnp.concatenate([k_prev, kc_ref[rows, :]], axis=0)
        vt_sc[u] = jnp.concatenate([v_prev, vc_ref[rows, :]], axis=0).T.astype(BF16)
        for g in range(B_KV_HEADS):
            k_sc[u, g] = jnp.where(lane_group == g, kband, 0.0).astype(BF16)
        for g in range(B_KV_HEADS):
            s_sc[u, g] = jnp.dot(k_sc[u, g], x_sc[u], preferred_element_type=F32)
        for g in range(B_KV_HEADS):
            s = s_sc[u, g] + bias_ref[g, plane]
            sink = sink_ref[g]
            m = jnp.maximum(jnp.max(s, axis=0, keepdims=True), sink)
            p = jnp.exp2(s - m)
            d_sc[u, g] = jnp.sum(p, axis=0, keepdims=True) + jnp.exp2(sink - m)
            p_sc[u, g] = p.astype(BF16)
        for g in range(B_KV_HEADS):
            o = jnp.dot(vt_sc[u, g * dh:(g + 1) * dh, :], p_sc[u, g],
                        preferred_element_type=F32) / d_sc[u, g]
            for h in range(B_GROUP):
                ot_sc[u, head(g, h), :] = o[:, h * W:(h + 1) * W]
        gate = jnp.concatenate([g0_ref[rows, :], g1_ref[rows, :]], axis=1)
        o_ref[rows, :] = (ot_sc[u].T * _silu(gate)).astype(o_ref.dtype)


def _swa_branch(proj, bias, sink, bsz, seq):
    W = WINDOW
    R = SWA_BLOCKS * W
    nb = seq // W
    ns = seq // R
    kvw = B_KV_HEADS * B_HEAD_DIM
    cq = COL_QB // BRANCH_WIDTH
    ck = COL_KB // kvw
    cv = COL_VB // kvw
    cg = COL_GB // (BRANCH_WIDTH // 2)
    assert COL_QB % BRANCH_WIDTH == 0 and COL_KB % kvw == 0 and COL_VB % kvw == 0
    assert COL_GB % (BRANCH_WIDTH // 2) == 0 and seq % R == 0
    prev = lambda b, n: (b * nb + jnp.maximum(SWA_BLOCKS * n - 1, 0))
    cur = lambda c, w: pl.BlockSpec((R, w), lambda b, n: (b * ns + n, c))
    return pl.pallas_call(
        _swa_kernel,
        grid=(bsz, ns),
        in_specs=[
            cur(cq, BRANCH_WIDTH),
            pl.BlockSpec((W, kvw), lambda b, n: (prev(b, n), ck)),
            cur(ck, kvw),
            pl.BlockSpec((W, kvw), lambda b, n: (prev(b, n), cv)),
            cur(cv, kvw),
            cur(cg, BRANCH_WIDTH // 2),
            cur(cg + 1, BRANCH_WIDTH // 2),
            pl.BlockSpec(bias.shape, lambda b, n: (0, 0, 0, 0)),
            pl.BlockSpec(sink.shape, lambda b, n: (0, 0, 0)),
        ],
        out_specs=cur(0, BRANCH_WIDTH),
        out_shape=jax.ShapeDtypeStruct((bsz * seq, BRANCH_WIDTH), BF16),
        scratch_shapes=[
            pltpu.VMEM((SWA_BLOCKS, BRANCH_WIDTH, W), F32),
            pltpu.VMEM((SWA_BLOCKS, kvw, B_GROUP * W), BF16),
            pltpu.VMEM((SWA_BLOCKS, B_KV_HEADS, 2 * W, kvw), BF16),
            pltpu.VMEM((SWA_BLOCKS, kvw, 2 * W), BF16),
            pltpu.VMEM((SWA_BLOCKS, B_KV_HEADS, 2 * W, B_GROUP * W), F32),
            pltpu.VMEM((SWA_BLOCKS, B_KV_HEADS, 2 * W, B_GROUP * W), BF16),
            pltpu.VMEM((SWA_BLOCKS, B_KV_HEADS, 1, B_GROUP * W), F32),
        ],
        compiler_params=_params(("parallel", "arbitrary")),
        name="swa",
    )(proj, proj, proj, proj, proj, proj, proj, bias, sink)


def _swa_sink_rows(sinks):
    s = (sinks.astype(F32) * LOG2E).reshape(B_KV_HEADS, 1, B_GROUP, 1)
    return jnp.broadcast_to(s, (B_KV_HEADS, 1, B_GROUP, WINDOW)).reshape(
        B_KV_HEADS, 1, B_GROUP * WINDOW)


SUBLANES = 8


def _scan_steps(a, b, index, length, axis):
    k = 1
    while k < length:
        keep = index >= k
        a_prev = jnp.where(keep, pltpu.roll(a, k, axis), 1.0)
        b_prev = jnp.where(keep, pltpu.roll(b, k, axis), 0.0)
        b = a * b_prev + b
        a = a * a_prev
        k *= 2
    return a, b


LRU_BLOCKS = 2


def _rglru_kernel(xc_ref, gc_ref, cw_ref, cb_ref, wr_ref, br_ref, wi_ref, bi_ref, lam_ref, o_ref,
                  xpad_sc, h_sc, a_sc, b_sc, hin_sc, *, seq):
    T = LRU_CHUNK
    C = C_BLOCK_DIM
    PAD = 8
    blocks = range(LRU_BLOCKS)
    ch = lambda u: slice(u * C, (u + 1) * C)
    for u in blocks:
        xpad_sc[u, :PAD, :] = jnp.zeros((PAD, C), F32)
        xpad_sc[u, PAD:, :] = xc_ref[:, ch(u)]
    h_sc[...] = jnp.zeros_like(h_sc)
    nlam = -lam_ref[...]
    softplus = jnp.maximum(nlam, 0.0) + jnp.log1p(jnp.exp(-jnp.abs(nlam)))
    decay = -LRU_C * softplus
    tiles = T // SUBLANES
    sub = lax.broadcasted_iota(jnp.int32, (tiles, SUBLANES, C), 1)
    tile_row = lax.broadcasted_iota(jnp.int32, (tiles, C), 0)

    def chunk(c, carry):
        t0 = pl.multiple_of(c * T, T)
        for u in blocks:
            xw = xpad_sc[u, pl.ds(t0, T + PAD), :]
            conv = cb_ref[:, ch(u)]
            for w in range(CONV_WIDTH):
                off = PAD - (CONV_WIDTH - 1) + w
                conv = conv + cw_ref[w:w + 1, ch(u)] * xw[off:off + T, :]
            cbf = conv.astype(BF16)
            r = jax.nn.sigmoid(jnp.dot(cbf, wr_ref[u], preferred_element_type=F32)
                               + br_ref[:, ch(u)])
            i = jax.nn.sigmoid(jnp.dot(cbf, wi_ref[u], preferred_element_type=F32)
                               + bi_ref[:, ch(u)])
            log_a = r * decay[:, ch(u)]
            a = jnp.exp(log_a)
            b = jnp.sqrt(jnp.maximum(-jnp.tanh(log_a) * (a * a + 1.0), 0.0)) * (i * conv)
            a, b = _scan_steps(a.reshape(tiles, SUBLANES, C), b.reshape(tiles, SUBLANES, C),
                               sub, SUBLANES, 1)
            a_sc[u] = a.reshape(T, C)
            b_sc[u] = b.reshape(T, C)
            last = pl.ds(SUBLANES - 1, tiles, stride=SUBLANES)
            a_tile, b_tile = _scan_steps(a_sc[u, last, :], b_sc[u, last, :], tile_row, tiles, 0)
            h_out = a_tile * h_sc[u] + b_tile
            hin_sc[u] = jnp.where(tile_row >= 1, pltpu.roll(h_out, 1, 0), h_sc[u])
            h_sc[u] = h_out[tiles - 1:tiles, :]
            h_in = jnp.concatenate(
                [jnp.broadcast_to(hin_sc[u, t:t + 1, :], (SUBLANES, C)) for t in range(tiles)],
                axis=0)
            h = a_sc[u] * h_in + b_sc[u]
            o_ref[pl.ds(t0, T), ch(u)] = (h * _silu(gc_ref[pl.ds(t0, T), ch(u)])).astype(
                o_ref.dtype)
        return carry

    lax.fori_loop(0, seq // T, chunk, 0)


def _rglru_branch(proj, conv_w, conv_b, w_r_bf, b_r, w_i_bf, b_i, lam, layer, bsz, seq):
    nbk = LRU_BLOCKS
    cd = C_BLOCK_DIM
    wd = nbk * cd
    assert C_BLOCKS % nbk == 0 and COL_XC % wd == 0 and COL_GC % wd == 0
    cx = COL_XC // wd
    cg = COL_GC // wd
    vspec = pl.BlockSpec((None, 1, wd), lambda b, c: (layer, 0, c))
    wspec = pl.BlockSpec((None, nbk, cd, cd), lambda b, c: (layer, c, 0, 0))
    return pl.pallas_call(
        functools.partial(_rglru_kernel, seq=seq),
        grid=(bsz, C_BLOCKS // nbk),
        in_specs=[
            pl.BlockSpec((seq, wd), lambda b, c: (b, cx + c)),
            pl.BlockSpec((seq, wd), lambda b, c: (b, cg + c)),
            pl.BlockSpec((None, CONV_WIDTH, wd), lambda b, c: (layer, 0, c)),
            vspec, wspec, vspec, wspec, vspec, vspec,
        ],
        out_specs=pl.BlockSpec((seq, wd), lambda b, c: (b, c)),
        out_shape=jax.ShapeDtypeStruct((bsz * seq, BRANCH_WIDTH), BF16),
        scratch_shapes=[pltpu.VMEM((nbk, seq + 8, cd), F32), pltpu.VMEM((nbk, 1, cd), F32),
                        pltpu.VMEM((nbk, LRU_CHUNK, cd), F32),
                        pltpu.VMEM((nbk, LRU_CHUNK, cd), F32),
                        pltpu.VMEM((nbk, LRU_CHUNK // SUBLANES, cd), F32)],
        compiler_params=_params(("parallel", "parallel")),
        name="rglru",
    )(proj, proj, conv_w, conv_b, w_r_bf, b_r, w_i_bf, b_i, lam)


def _merge_kernel(ya_ref, yb_ref, yc_ref, ma_ref, mb_ref, mc_ref, wbr_ref, o_ref):
    merged = None
    for n, (y_ref, m_ref) in enumerate(((ya_ref, ma_ref), (yb_ref, mb_ref), (yc_ref, mc_ref))):
        term = jax.nn.sigmoid(m_ref[...].astype(F32)) * jnp.dot(y_ref[...], wbr_ref[n],
                                                                preferred_element_type=F32)
        merged = term if merged is None else merged + term
    o_ref[...] = merged.astype(o_ref.dtype)


def _merge(ya, yb, yc, gates, wbr_bf, layer):
    t = ya.shape[0]
    d = wbr_bf.shape[3]
    tm, tn = MERGE_TM, MERGE_TN
    assert d % tn == 0
    cm = 0
    per = d // tn
    yspec = pl.BlockSpec((tm, BRANCH_WIDTH), lambda i, c: (i, 0))
    mspec = lambda n: pl.BlockSpec((tm, tn), lambda i, c: (i, cm + n * per + c))
    return pl.pallas_call(
        _merge_kernel,
        grid=(t // tm, per),
        in_specs=[
            yspec, yspec, yspec, mspec(0), mspec(1), mspec(2),
            pl.BlockSpec((None, 3, BRANCH_WIDTH, tn), lambda i, c: (layer, 0, 0, c)),
        ],
        out_specs=pl.BlockSpec((tm, tn), lambda i, c: (i, c)),
        out_shape=jax.ShapeDtypeStruct((t, d), BF16),
        compiler_params=_params(("parallel", "arbitrary")),
        name="merge",
    )(ya, yb, yc, gates, gates, gates, wbr_bf)


CAST_COLS = 1024


def _out_ple_kernel(mg_ref, x_ref, p_ref, wout_ref, g_ref, wpg_ref, wpp_ref, fg_ref, *rest,
                    final, cast):
    if cast:
        wsrc_ref, o_ref, wdst_ref = rest
        wdst_ref[...] = wsrc_ref[...].astype(wdst_ref.dtype)
    else:
        (o_ref,) = rest
    x = x_ref[...] + jnp.dot(mg_ref[...], wout_ref[...], preferred_element_type=F32)
    h = _rms(x, g_ref[...]).astype(BF16)
    gate = jax.nn.sigmoid(jnp.dot(h, wpg_ref[...], preferred_element_type=F32))
    emb = jnp.dot(p_ref[...].astype(BF16), wpp_ref[...], preferred_element_type=F32)
    y = x + gate * emb
    if final:
        y = _rms(y, fg_ref[...])
    o_ref[...] = y


def _out_ple(merged, x2d, p3d, wout_bf, g, wpg_bf, wpp_bf, final_g, layer, final, w_next=None):
    t, d = x2d.shape
    tm = PLE_TM
    steps = t // tm
    in_extra, out_extra, args_extra = [], [], []
    out_shape = jax.ShapeDtypeStruct((t, d), F32)
    if w_next is not None:
        depth, wd, wn = w_next.shape
        rows = wd * wn // CAST_COLS
        assert wd * wn % CAST_COLS == 0 and rows % (steps * 16) == 0
        in_extra = [pl.BlockSpec((None, rows // steps, CAST_COLS), lambda i: (layer + 1, i, 0))]
        out_extra = [pl.BlockSpec((rows // steps, CAST_COLS), lambda i: (i, 0))]
        args_extra = [w_next.reshape(depth, rows, CAST_COLS)]
        out_shape = [out_shape, jax.ShapeDtypeStruct((rows, CAST_COLS), BF16)]
    wspec = lambda r: pl.BlockSpec((None, r, d), lambda i: (layer, 0, 0),
                                   pipeline_mode=pl.Buffered(1))
    row = lambda w: pl.BlockSpec((tm, w), lambda i: (i, 0))
    out = pl.pallas_call(
        functools.partial(_out_ple_kernel, final=final, cast=w_next is not None),
        grid=(steps,),
        in_specs=[row(d), row(d), pl.BlockSpec((None, tm, PLE_DIM), lambda i: (layer, i, 0)),
                  wspec(d), pl.BlockSpec((None, 1, d), lambda i: (layer, 0, 0)), wspec(d),
                  wspec(PLE_DIM), pl.BlockSpec((1, d), lambda i: (0, 0))] + in_extra,
        out_specs=[row(d)] + out_extra if out_extra else row(d),
        out_shape=out_shape,
        compiler_params=_params(("parallel",)),
        name="out_ple",
    )(merged, x2d, p3d, wout_bf, g, wpg_bf, wpp_bf, final_g.reshape(1, d), *args_extra)
    if w_next is None:
        return out, None
    return out[0], out[1].reshape(1, wd, wn)


def kernel(x, p, rpe_table, norm_g, w_in, sinks, conv_w, conv_b, w_r, b_r, w_i, b_i, lam, w_br,
           w_out, ple_norm_g, w_pg, w_pp, final_norm_g):
    bsz, seq, d = x.shape
    depth = w_in.shape[0]
    assert d == D_MODEL and w_in.shape[2] == IN_WIDTH
    assert seq % MOBA_BLOCK == 0 and seq % LRU_CHUNK == 0 and (bsz * seq) % PROJ_TM == 0
    t = bsz * seq
    xf = x.reshape(t, d)
    table = rpe_table.astype(F32).T * LOG2E
    bias_a = _bias_lookup(table[:A_HEADS], _moba_bucket_planes(), 1)
    bias_b = _bias_lookup(table[A_HEADS:], _swa_bucket_planes(), B_GROUP)
    bf = lambda w: w.astype(BF16)
    vec = lambda v: v.reshape(depth, 1, v.shape[-1])
    w_r_bf, w_i_bf, w_br_bf, w_out_bf, w_pg_bf, w_pp_bf = map(
        bf, (w_r, w_i, w_br, w_out, w_pg, w_pp))
    w_in_cur = bf(w_in[:1])
    p3d = p.reshape(depth, t, PLE_DIM)
    for i in range(depth):
        proj, gates = _in_proj(xf, vec(norm_g), w_in_cur, i, 0)
        ya = _moba_branch(proj, bias_a, bsz, seq)
        yb = _swa_branch(proj, bias_b, _swa_sink_rows(sinks[i]), bsz, seq)
        yc = _rglru_branch(proj, conv_w, vec(conv_b), w_r_bf, vec(b_r), w_i_bf, vec(b_i),
                           vec(lam), i, bsz, seq)
        merged = _merge(ya, yb, yc, gates, w_br_bf, i)
        xf, w_in_cur = _out_ple(merged, xf, p3d, w_out_bf, vec(ple_norm_g), w_pg_bf, w_pp_bf,
                                final_norm_g, i, final=(i == depth - 1),
                                w_next=w_in if i + 1 < depth else None)
    return xf.reshape(bsz, seq, d)
```

```python
import functools
import math

import jax
import jax.numpy as jnp
import numpy as np
from jax import lax
from jax.experimental import pallas as pl
from jax.experimental.pallas import tpu as pltpu

F32 = jnp.float32
BF16 = jnp.bfloat16

D_MODEL = 2048
PLE_DIM = 256
BRANCH_WIDTH = 1024
A_HEADS = 8
A_HEAD_DIM = 128
MOBA_BLOCK = 256
MOBA_TOPK = 3
B_Q_HEADS = 16
B_KV_HEADS = 4
B_GROUP = B_Q_HEADS // B_KV_HEADS
B_HEAD_DIM = 64
WINDOW = 128
C_BLOCKS = 8
C_BLOCK_DIM = 128
CONV_WIDTH = 4
LRU_C = 8.0
RPE_BUCKETS = 32
RPE_MAX_DIST = 128
EPS = 1e-6
NEG = -1e30
LOG2E = math.log2(math.e)

COL_QA = 0
COL_KA = COL_QA + BRANCH_WIDTH
COL_VA = COL_KA + BRANCH_WIDTH
COL_GA = COL_VA + BRANCH_WIDTH
COL_QB = COL_GA + BRANCH_WIDTH
COL_KB = COL_QB + BRANCH_WIDTH
COL_VB = COL_KB + B_KV_HEADS * B_HEAD_DIM
COL_GB = COL_VB + B_KV_HEADS * B_HEAD_DIM
COL_XC = COL_GB + BRANCH_WIDTH
COL_GC = COL_XC + BRANCH_WIDTH
COL_MG = COL_GC + BRANCH_WIDTH
IN_WIDTH = COL_MG + 3 * D_MODEL

PROJ_TM = 2048
PROJ_TN = 512
MERGE_TM = 2048
MERGE_TN = 512
PLE_TM = 512
LRU_CHUNK = 256

VMEM_LIMIT = 62 * 1024 * 1024


def _params(sem):
    return pltpu.CompilerParams(dimension_semantics=sem, vmem_limit_bytes=VMEM_LIMIT)


def _rms(xf, g):
    return xf * lax.rsqrt(jnp.mean(xf * xf, axis=-1, keepdims=True) + EPS) * g


def _in_proj_kernel(x_ref, g_ref, w_ref, o_ref, mg_ref, h_sc):
    @pl.when(pl.program_id(1) == 0)
    def _():
        h_sc[...] = _rms(x_ref[...], g_ref[...]).astype(BF16)

    acc = jnp.dot(h_sc[...], w_ref[...], preferred_element_type=F32)
    o_ref[...] = acc
    mg_ref[...] = acc.astype(mg_ref.dtype)


def _in_proj(x2d, g, w_bf, layer, w_layer):
    t, d = x2d.shape
    n = w_bf.shape[2]
    assert COL_MG % PROJ_TN == 0
    main_tiles = COL_MG // PROJ_TN
    return pl.pallas_call(
        _in_proj_kernel,
        grid=(t // PROJ_TM, n // PROJ_TN),
        in_specs=[
            pl.BlockSpec((PROJ_TM, d), lambda i, j: (i, 0)),
            pl.BlockSpec((None, 1, d), lambda i, j: (layer, 0, 0)),
            pl.BlockSpec((None, d, PROJ_TN), lambda i, j: (w_layer, 0, j)),
        ],
        out_specs=[
            pl.BlockSpec((PROJ_TM, PROJ_TN), lambda i, j: (i, jnp.minimum(j, main_tiles))),
            pl.BlockSpec((PROJ_TM, PROJ_TN), lambda i, j: (i, jnp.maximum(j - main_tiles, 0))),
        ],
        out_shape=[jax.ShapeDtypeStruct((t, COL_MG + PROJ_TN), F32),
                   jax.ShapeDtypeStruct((t, n - COL_MG), BF16)],
        scratch_shapes=[pltpu.VMEM((PROJ_TM, d), BF16)],
        compiler_params=_params(("parallel", "arbitrary")),
        name="in_proj",
    )(x2d, g, w_bf)


MASKED_BUCKET = RPE_BUCKETS


def _t5_bucket_np(dist):
    n = np.maximum(dist, 0)
    max_exact = RPE_BUCKETS // 2
    ratio = np.log(np.maximum(n, 1).astype(np.float32) / np.float32(max_exact)) / np.float32(
        math.log(RPE_MAX_DIST / max_exact))
    large = max_exact + (ratio * np.float32(RPE_BUCKETS - max_exact)).astype(np.int32)
    large = np.minimum(large, RPE_BUCKETS - 1)
    return np.where(n < max_exact, n, large).astype(np.int32)


def _moba_bucket_planes():
    L = MOBA_BLOCK
    ki = np.arange(L)[:, None]
    qi = np.arange(L)[None, :]
    d_own = qi - ki
    d_far = 2 * L + qi - ki
    assert d_far.min() >= RPE_MAX_DIST
    own = np.where(d_own >= 0, _t5_bucket_np(d_own), MASKED_BUCKET)
    return np.stack([own, _t5_bucket_np(L + qi - ki), _t5_bucket_np(d_far)]).astype(np.int32)


def _swa_bucket_planes():
    W = WINDOW
    kk = np.arange(2 * W)[:, None]
    qi = np.arange(W)[None, :]
    dist = qi - kk + W
    ok = (dist >= 0) & (dist < W)
    b = _t5_bucket_np(dist)
    return np.stack([np.where(ok, b, MASKED_BUCKET),
                     np.where(ok & (kk >= W), b, MASKED_BUCKET)]).astype(np.int32)


def _bias_lookup_kernel(tab_ref, idx_ref, o_ref):
    h = pl.program_id(0)
    for p in range(idx_ref.shape[0]):
        idx = idx_ref[p]
        acc = jnp.full(idx.shape, NEG, F32)
        for b in range(RPE_BUCKETS):
            acc = jnp.where(idx == b, tab_ref[h * RPE_BUCKETS + b], acc)
        o_ref[0, p] = acc


def _bias_lookup(table_hb, planes, heads_per_row):
    nh = table_hb.shape[0]
    npl, r, c = planes.shape
    return pl.pallas_call(
        _bias_lookup_kernel,
        grid=(nh,),
        in_specs=[
            pl.BlockSpec(memory_space=pltpu.SMEM),
            pl.BlockSpec((npl, r, c), lambda h: (0, 0, 0)),
        ],
        out_specs=pl.BlockSpec((1, npl, r, c),
                               lambda h: (h // heads_per_row, 0, 0, h % heads_per_row)),
        out_shape=jax.ShapeDtypeStruct((nh // heads_per_row, npl, r, c * heads_per_row), F32),
        compiler_params=_params(("parallel",)),
        name="bias_lookup",
    )(table_hb.reshape(-1), jnp.asarray(planes))


GATE_ROWS = 16


def _silu(x):
    return x * jax.nn.sigmoid(x)


MOBA_GROUP = 2
MOBA_UNROLL = 2
MOBA_HEADS = 2


def _moba_prep_kernel(q_ref, k_ref, v_ref, kaug_ref, vt_ref, qaug_ref, kmean_sc, *, nb):
    L = MOBA_BLOCK
    dh = A_HEAD_DIM
    G = MOBA_GROUP

    lanes = lax.broadcasted_iota(jnp.int32, (L, dh), 1)
    kmean_sc[...] = jnp.zeros_like(kmean_sc)
    for n in range(nb):
        rows = slice((n % G) * L, (n % G + 1) * L)
        kn = k_ref[n * L:(n + 1) * L, :]
        kaug_ref[0, n // G, rows, :dh] = kn.astype(BF16)
        kaug_ref[0, n // G, rows, dh:] = jnp.where(lanes == n, 1.0, 0.0).astype(BF16)
        kmean_sc[n:n + 1, :] = jnp.sum(kn, axis=0, keepdims=True) * (1.0 / L)
        vt_ref[0, n // G, :, rows] = v_ref[n * L:(n + 1) * L, :].T.astype(BF16)

    blk = lax.broadcasted_iota(jnp.int32, (GATE_ROWS, L), 0)
    blk_f = blk.astype(F32)
    for j in range(nb):
        qt = q_ref[j * L:(j + 1) * L, :].T
        gate = jnp.dot(kmean_sc[...], qt, precision=lax.Precision.HIGHEST,
                       preferred_element_type=F32)
        gm = jnp.where(blk < j, gate, NEG)
        picked = jnp.zeros((GATE_ROWS, L), F32)
        for _ in range(MOBA_TOPK):
            mx = jnp.max(gm, axis=0, keepdims=True)
            first = jnp.min(jnp.where(gm == mx, blk_f, float(GATE_ROWS)), axis=0, keepdims=True)
            hit = blk_f == first
            picked = jnp.where(hit, 1.0, picked)
            gm = jnp.where(hit, -jnp.inf, gm)
        allowed = ((picked > 0.0) & (blk < j)) | (blk == j)
        qaug_ref[0, j, :dh, :] = (qt * (dh ** -0.5 * LOG2E)).astype(BF16)
        qaug_ref[0, j, dh:dh + GATE_ROWS, :] = jnp.where(allowed, 0.0, NEG).astype(BF16)
        qaug_ref[0, j, dh + GATE_ROWS:, :] = jnp.zeros((dh - GATE_ROWS, L), BF16)


def _moba_prep(proj, bsz, seq):
    L = MOBA_BLOCK
    dh = A_HEAD_DIM
    G = MOBA_GROUP
    nb = seq // L
    nh = bsz * A_HEADS
    cq, ck, cv = (c // dh for c in (COL_QA, COL_KA, COL_VA))
    col = lambda c: pl.BlockSpec((seq, dh), lambda b, h: (b, c + h))
    shapes = [(nh, nb // G, G * L, 2 * dh), (nh, nb // G, dh, G * L), (nh, nb, 2 * dh, L)]
    return pl.pallas_call(
        functools.partial(_moba_prep_kernel, nb=nb),
        grid=(bsz, A_HEADS),
        in_specs=[col(cq), col(ck), col(cv)],
        out_specs=[pl.BlockSpec((1,) + s[1:], lambda b, h: (b * A_HEADS + h, 0, 0, 0))
                   for s in shapes],
        out_shape=[jax.ShapeDtypeStruct(s, BF16) for s in shapes],
        scratch_shapes=[pltpu.VMEM((GATE_ROWS, dh), F32)],
        compiler_params=_params(("parallel", "parallel")),
        name="moba_prep",
    )(proj, proj, proj)


def _moba_schedule(nb):
    G = MOBA_GROUP
    return [(j, i) for j in range(nb) for i in range((j + G) // G)]


def _moba_kernel(jt_ref, it_ref, kaug_ref, vt_ref, qaug_ref, g_ref, bias_ref, o_ref,
                 s_sc, p_sc, alpha_sc, lfin_sc, lout_sc, m_sc, l_sc, acc_sc, *, sched):
    L = MOBA_BLOCK
    dh = A_HEAD_DIM
    G = MOBA_GROUP
    U = MOBA_UNROLL
    nt = len(sched)
    heads = range(MOBA_HEADS)
    hcol = lambda u: slice(u * dh, (u + 1) * dh)
    last_group = lambda j: j // G

    def scores(u, t, slot):
        s_sc[u, slot] = jnp.dot(kaug_ref[u, it_ref[t]], qaug_ref[u, jt_ref[t]],
                                preferred_element_type=F32)

    def softmax(u, t, slot):
        j = jt_ref[t]
        i = it_ref[t]
        s = s_sc[u, slot]
        parts = []
        for g in range(G):
            n = i * G + g
            plane = jnp.where(n == j, 0, jnp.where(n == j - 1, 1, 2))
            parts.append(s[g * L:(g + 1) * L, :] + bias_ref[u, plane])
        m_old = jnp.where(i == 0, NEG, m_sc[u])
        m_new = m_old
        for sp in parts:
            m_new = jnp.maximum(m_new, jnp.max(sp, axis=0, keepdims=True))
        alpha = jnp.exp2(m_old - m_new)
        probs = [jnp.exp2(sp - m_new) for sp in parts]
        l_new = alpha * l_sc[u]
        for pr in probs:
            l_new = l_new + jnp.sum(pr, axis=0, keepdims=True)
        p_sc[u, slot] = jnp.concatenate(probs, axis=0).astype(BF16)
        alpha_sc[u, slot] = alpha
        lfin_sc[u, slot] = l_new
        l_sc[u] = l_new
        m_sc[u] = m_new

    def weighted_values(u, t, slot, r):
        acc_sc[u, r] = alpha_sc[u, slot] * acc_sc[u, (r - 1) % U] + jnp.dot(
            vt_ref[u, it_ref[t]], p_sc[u, slot], preferred_element_type=F32)
        lout_sc[u, r] = lfin_sc[u, slot]

    def finish(u, j, r):
        out_t = acc_sc[u, r] / lout_sc[u, r]
        rows = pl.ds(pl.multiple_of(j * L, L), L)
        o_ref[rows, hcol(u)] = (out_t.T * _silu(g_ref[rows, hcol(u)])).astype(o_ref.dtype)

    s_sc[:, 1] = jnp.full(s_sc.shape[:1] + s_sc.shape[2:], -jnp.inf, F32)
    p_sc[:, 0] = jnp.zeros(p_sc.shape[:1] + p_sc.shape[2:], BF16)
    alpha_sc[...] = jnp.ones_like(alpha_sc)
    lfin_sc[...] = jnp.ones_like(lfin_sc)
    m_sc[...] = jnp.full_like(m_sc, NEG)
    l_sc[...] = jnp.zeros_like(l_sc)
    acc_sc[...] = jnp.zeros_like(acc_sc)

    def unrolled_steps(k, carry):
        base = U * k
        done = []
        for r in range(U):
            t = base + r
            t_pv = jnp.maximum(t - 2, 0)
            t_sm = jnp.maximum(t - 1, 0)
            for u in heads:
                weighted_values(u, t_pv, r % 2, r)
                softmax(u, t_sm, (r + 1) % 2)
                scores(u, t, r % 2)
            done.append((t - 2, t_pv, r))
        for t_real, t_pv, r in done:
            j_done = jt_ref[t_pv]

            @pl.when((t_real >= 0) & (it_ref[t_pv] == last_group(j_done)))
            def _():
                for u in heads:
                    finish(u, j_done, r)

        return carry

    assert U % 2 == 0
    n_loop = nt // U
    lax.fori_loop(0, n_loop, unrolled_steps, 0)
    for t in range(n_loop * U, nt + 2):
        r = t % U
        for u in heads:
            if 0 <= t - 2:
                weighted_values(u, t - 2, r % 2, r)
                j, i = sched[t - 2]
                if i == last_group(j):
                    finish(u, j, r)
            if 0 <= t - 1 < nt:
                softmax(u, t - 1, (r + 1) % 2)
            if t < nt:
                scores(u, t, r % 2)


def _moba_branch(proj, bias, bsz, seq):
    L = MOBA_BLOCK
    dh = A_HEAD_DIM
    G = MOBA_GROUP
    nb = seq // L
    hp = MOBA_HEADS
    assert min(MOBA_TOPK, nb - 1) == MOBA_TOPK and nb <= GATE_ROWS and nb % G == 0
    assert A_HEADS % hp == 0
    kaug, vt, qaug = _moba_prep(proj, bsz, seq)
    sched = _moba_schedule(nb)
    jt = jnp.asarray(np.array([j for j, _ in sched], np.int32))
    it = jnp.asarray(np.array([i for _, i in sched], np.int32))
    pairs = A_HEADS // hp
    cg = COL_GA // (hp * dh)
    smem = pl.BlockSpec(memory_space=pltpu.SMEM)
    per_pair = lambda a: pl.BlockSpec((hp,) + a.shape[1:], lambda b, h: (b * pairs + h, 0, 0, 0))
    return pl.pallas_call(
        functools.partial(_moba_kernel, sched=sched),
        grid=(bsz, pairs),
        in_specs=[smem, smem, per_pair(kaug), per_pair(vt), per_pair(qaug),
                  pl.BlockSpec((seq, hp * dh), lambda b, h: (b, cg + h)),
                  pl.BlockSpec((hp, 3, L, L), lambda b, h: (h, 0, 0, 0))],
        out_specs=pl.BlockSpec((seq, hp * dh), lambda b, h: (b, h)),
        out_shape=jax.ShapeDtypeStruct((bsz * seq, BRANCH_WIDTH), BF16),
        scratch_shapes=[
            pltpu.VMEM((hp, 2, G * L, L), F32),
            pltpu.VMEM((hp, 2, G * L, L), BF16),
            pltpu.VMEM((hp, 2, 1, L), F32),
            pltpu.VMEM((hp, 2, 1, L), F32),
            pltpu.VMEM((hp, MOBA_UNROLL, 1, L), F32),
            pltpu.VMEM((hp, 1, L), F32),
            pltpu.VMEM((hp, 1, L), F32),
            pltpu.VMEM((hp, MOBA_UNROLL, dh, L), F32),
        ],
        compiler_params=_params(("parallel", "parallel")),
        name="moba",
    )(jt, it, kaug, vt, qaug, proj, bias)


SWA_BLOCKS = 4


def _swa_kernel(q_ref, kp_ref, kc_ref, vp_ref, vc_ref, g0_ref, g1_ref, bias_ref, sink_ref, o_ref,
                ot_sc, x_sc, k_sc, vt_sc, s_sc, p_sc, d_sc):
    W = WINDOW
    dh = B_HEAD_DIM
    kvw = B_KV_HEADS * dh
    head = lambda g, h: slice((g * B_GROUP + h) * dh, (g * B_GROUP + h + 1) * dh)
    lane_group = lax.broadcasted_iota(jnp.int32, (2 * W, kvw), 1) // dh
    first = pl.program_id(1) == 0
    for u in range(SWA_BLOCKS):
        rows = slice(u * W, (u + 1) * W)
        if u == 0:
            k_prev, v_prev = kp_ref[...], vp_ref[...]
            plane = jnp.where(first, 1, 0)
        else:
            k_prev, v_prev = kc_ref[(u - 1) * W:u * W, :], vc_ref[(u - 1) * W:u * W, :]
            plane = 0
        qt = q_ref[rows, :].T * (dh ** -0.5 * LOG2E)
        x_sc[u] = jnp.concatenate(
            [jnp.concatenate([qt[head(g, h), :] for g in range(B_KV_HEADS)], axis=0)
             for h in range(B_GROUP)], axis=1).astype(BF16)
        kband = jnp.concatenate([k_prev, kc_ref[rows, :]], axis=0)
        vt_sc[u] = jnp.concatenate([v_prev, vc_ref[rows, :]], axis=0).T.astype(BF16)
        for g in range(B_KV_HEADS):
            k_sc[u, g] = jnp.where(lane_group == g, kband, 0.0).astype(BF16)
        for g in range(B_KV_HEADS):
            s_sc[u, g] = jnp.dot(k_sc[u, g], x_sc[u], preferred_element_type=F32)
        for g in range(B_KV_HEADS):
            s = s_sc[u, g] + bias_ref[g, plane]
            sink = sink_ref[g]
            m = jnp.maximum(jnp.max(s, axis=0, keepdims=True), sink)
            p = jnp.exp2(s - m)
            d_sc[u, g] = jnp.sum(p, axis=0, keepdims=True) + jnp.exp2(sink - m)
            p_sc[u, g] = p.astype(BF16)
        for g in range(B_KV_HEADS):
            o = jnp.dot(vt_sc[u, g * dh:(g + 1) * dh, :], p_sc[u, g],
                        preferred_element_type=F32) / d_sc[u, g]
            for h in range(B_GROUP):
                ot_sc[u, head(g, h), :] = o[:, h * W:(h + 1) * W]
        gate = jnp.concatenate([g0_ref[rows, :], g1_ref[rows, :]], axis=1)
        o_ref[rows, :] = (ot_sc[u].T * _silu(gate)).astype(o_ref.dtype)


def _swa_branch(proj, bias, sink, bsz, seq):
    W = WINDOW
    R = SWA_BLOCKS * W
    nb = seq // W
    ns = seq // R
    kvw = B_KV_HEADS * B_HEAD_DIM
    cq = COL_QB // BRANCH_WIDTH
    ck = COL_KB // kvw
    cv = COL_VB // kvw
    cg = COL_GB // (BRANCH_WIDTH // 2)
    assert COL_QB % BRANCH_WIDTH == 0 and COL_KB % kvw == 0 and COL_VB % kvw == 0
    assert COL_GB % (BRANCH_WIDTH // 2) == 0 and seq % R == 0
    prev = lambda b, n: (b * nb + jnp.maximum(SWA_BLOCKS * n - 1, 0))
    cur = lambda c, w: pl.BlockSpec((R, w), lambda b, n: (b * ns + n, c))
    return pl.pallas_call(
        _swa_kernel,
        grid=(bsz, ns),
        in_specs=[
            cur(cq, BRANCH_WIDTH),
            pl.BlockSpec((W, kvw), lambda b, n: (prev(b, n), ck)),
            cur(ck, kvw),
            pl.BlockSpec((W, kvw), lambda b, n: (prev(b, n), cv)),
            cur(cv, kvw),
            cur(cg, BRANCH_WIDTH // 2),
            cur(cg + 1, BRANCH_WIDTH // 2),
            pl.BlockSpec(bias.shape, lambda b, n: (0, 0, 0, 0)),
            pl.BlockSpec(sink.shape, lambda b, n: (0, 0, 0)),
        ],
        out_specs=cur(0, BRANCH_WIDTH),
        out_shape=jax.ShapeDtypeStruct((bsz * seq, BRANCH_WIDTH), BF16),
        scratch_shapes=[
            pltpu.VMEM((SWA_BLOCKS, BRANCH_WIDTH, W), F32),
            pltpu.VMEM((SWA_BLOCKS, kvw, B_GROUP * W), BF16),
            pltpu.VMEM((SWA_BLOCKS, B_KV_HEADS, 2 * W, kvw), BF16),
            pltpu.VMEM((SWA_BLOCKS, kvw, 2 * W), BF16),
            pltpu.VMEM((SWA_BLOCKS, B_KV_HEADS, 2 * W, B_GROUP * W), F32),
            pltpu.VMEM((SWA_BLOCKS, B_KV_HEADS, 2 * W, B_GROUP * W), BF16),
            pltpu.VMEM((SWA_BLOCKS, B_KV_HEADS, 1, B_GROUP * W), F32),
        ],
        compiler_params=_params(("parallel", "arbitrary")),
        name="swa",
    )(proj, proj, proj, proj, proj, proj, proj, bias, sink)


def _swa_sink_rows(sinks):
    s = (sinks.astype(F32) * LOG2E).reshape(B_KV_HEADS, 1, B_GROUP, 1)
    return jnp.broadcast_to(s, (B_KV_HEADS, 1, B_GROUP, WINDOW)).reshape(
        B_KV_HEADS, 1, B_GROUP * WINDOW)


SUBLANES = 8


def _scan_steps(a, b, index, length, axis):
    k = 1
    while k < length:
        keep = index >= k
        a_prev = jnp.where(keep, pltpu.roll(a, k, axis), 1.0)
        b_prev = jnp.where(keep, pltpu.roll(b, k, axis), 0.0)
        b = a * b_prev + b
        a = a * a_prev
        k *= 2
    return a, b


LRU_BLOCKS = 2


def _rglru_kernel(xc_ref, gc_ref, cw_ref, cb_ref, wr_ref, br_ref, wi_ref, bi_ref, lam_ref, o_ref,
                  xpad_sc, h_sc, a_sc, b_sc, hin_sc, *, seq):
    T = LRU_CHUNK
    C = C_BLOCK_DIM
    PAD = 8
    blocks = range(LRU_BLOCKS)
    ch = lambda u: slice(u * C, (u + 1) * C)
    for u in blocks:
        xpad_sc[u, :PAD, :] = jnp.zeros((PAD, C), F32)
        xpad_sc[u, PAD:, :] = xc_ref[:, ch(u)]
    h_sc[...] = jnp.zeros_like(h_sc)
    nlam = -lam_ref[...]
    softplus = jnp.maximum(nlam, 0.0) + jnp.log1p(jnp.exp(-jnp.abs(nlam)))
    decay = -LRU_C * softplus
    tiles = T // SUBLANES
    sub = lax.broadcasted_iota(jnp.int32, (tiles, SUBLANES, C), 1)
    tile_row = lax.broadcasted_iota(jnp.int32, (tiles, C), 0)

    def chunk(c, carry):
        t0 = pl.multiple_of(c * T, T)
        for u in blocks:
            xw = xpad_sc[u, pl.ds(t0, T + PAD), :]
            conv = cb_ref[:, ch(u)]
            for w in range(CONV_WIDTH):
                off = PAD - (CONV_WIDTH - 1) + w
                conv = conv + cw_ref[w:w + 1, ch(u)] * xw[off:off + T, :]
            cbf = conv.astype(BF16)
            r = jax.nn.sigmoid(jnp.dot(cbf, wr_ref[u], preferred_element_type=F32)
                               + br_ref[:, ch(u)])
            i = jax.nn.sigmoid(jnp.dot(cbf, wi_ref[u], preferred_element_type=F32)
                               + bi_ref[:, ch(u)])
            log_a = r * decay[:, ch(u)]
            a = jnp.exp(log_a)
            b = jnp.sqrt(jnp.maximum(-jnp.tanh(log_a) * (a * a + 1.0), 0.0)) * (i * conv)
            a, b = _scan_steps(a.reshape(tiles, SUBLANES, C), b.reshape(tiles, SUBLANES, C),
                               sub, SUBLANES, 1)
            a_sc[u] = a.reshape(T, C)
            b_sc[u] = b.reshape(T, C)
            last = pl.ds(SUBLANES - 1, tiles, stride=SUBLANES)
            a_tile, b_tile = _scan_steps(a_sc[u, last, :], b_sc[u, last, :], tile_row, tiles, 0)
            h_out = a_tile * h_sc[u] + b_tile
            hin_sc[u] = jnp.where(tile_row >= 1, pltpu.roll(h_out, 1, 0), h_sc[u])
            h_sc[u] = h_out[tiles - 1:tiles, :]
            h_in = jnp.concatenate(
                [jnp.broadcast_to(hin_sc[u, t:t + 1, :], (SUBLANES, C)) for t in range(tiles)],
                axis=0)
            h = a_sc[u] * h_in + b_sc[u]
            o_ref[pl.ds(t0, T), ch(u)] = (h * _silu(gc_ref[pl.ds(t0, T), ch(u)])).astype(
                o_ref.dtype)
        return carry

    lax.fori_loop(0, seq // T, chunk, 0)


def _rglru_branch(proj, conv_w, conv_b, w_r_bf, b_r, w_i_bf, b_i, lam, layer, bsz, seq):
    nbk = LRU_BLOCKS
    cd = C_BLOCK_DIM
    wd = nbk * cd
    assert C_BLOCKS % nbk == 0 and COL_XC % wd == 0 and COL_GC % wd == 0
    cx = COL_XC // wd
    cg = COL_GC // wd
    vspec = pl.BlockSpec((None, 1, wd), lambda b, c: (layer, 0, c))
    wspec = pl.BlockSpec((None, nbk, cd, cd), lambda b, c: (layer, c, 0, 0))
    return pl.pallas_call(
        functools.partial(_rglru_kernel, seq=seq),
        grid=(bsz, C_BLOCKS // nbk),
        in_specs=[
            pl.BlockSpec((seq, wd), lambda b, c: (b, cx + c)),
            pl.BlockSpec((seq, wd), lambda b, c: (b, cg + c)),
            pl.BlockSpec((None, CONV_WIDTH, wd), lambda b, c: (layer, 0, c)),
            vspec, wspec, vspec, wspec, vspec, vspec,
        ],
        out_specs=pl.BlockSpec((seq, wd), lambda b, c: (b, c)),
        out_shape=jax.ShapeDtypeStruct((bsz * seq, BRANCH_WIDTH), BF16),
        scratch_shapes=[pltpu.VMEM((nbk, seq + 8, cd), F32), pltpu.VMEM((nbk, 1, cd), F32),
                        pltpu.VMEM((nbk, LRU_CHUNK, cd), F32),
                        pltpu.VMEM((nbk, LRU_CHUNK, cd), F32),
                        pltpu.VMEM((nbk, LRU_CHUNK // SUBLANES, cd), F32)],
        compiler_params=_params(("parallel", "parallel")),
        name="rglru",
    )(proj, proj, conv_w, conv_b, w_r_bf, b_r, w_i_bf, b_i, lam)


def _merge_kernel(ya_ref, yb_ref, yc_ref, ma_ref, mb_ref, mc_ref, wbr_ref, o_ref):
    merged = None
    for n, (y_ref, m_ref) in enumerate(((ya_ref, ma_ref), (yb_ref, mb_ref), (yc_ref, mc_ref))):
        term = jax.nn.sigmoid(m_ref[...].astype(F32)) * jnp.dot(y_ref[...], wbr_ref[n],
                                                                preferred_element_type=F32)
        merged = term if merged is None else merged + term
    o_ref[...] = merged.astype(o_ref.dtype)


def _merge(ya, yb, yc, gates, wbr_bf, layer):
    t = ya.shape[0]
    d = wbr_bf.shape[3]
    tm, tn = MERGE_TM, MERGE_TN
    assert d % tn == 0
    cm = 0
    per = d // tn
    yspec = pl.BlockSpec((tm, BRANCH_WIDTH), lambda i, c: (i, 0))
    mspec = lambda n: pl.BlockSpec((tm, tn), lambda i, c: (i, cm + n * per + c))
    return pl.pallas_call(
        _merge_kernel,
        grid=(t // tm, per),
        in_specs=[
            yspec, yspec, yspec, mspec(0), mspec(1), mspec(2),
            pl.BlockSpec((None, 3, BRANCH_WIDTH, tn), lambda i, c: (layer, 0, 0, c)),
        ],
        out_specs=pl.BlockSpec((tm, tn), lambda i, c: (i, c)),
        out_shape=jax.ShapeDtypeStruct((t, d), BF16),
        compiler_params=_params(("parallel", "arbitrary")),
        name="merge",
    )(ya, yb, yc, gates, gates, gates, wbr_bf)


BF16_SUBLANES = 16


def _out_ple_kernel(mg_ref, x_ref, p_ref, wout_ref, g_ref, wpg_ref, wpp_ref, fg_ref, *rest,
                    final, cast):
    if cast:
        wsrc_ref, o_ref, wdst_ref = rest
        wdst_ref[...] = wsrc_ref[...].astype(wdst_ref.dtype)
    else:
        (o_ref,) = rest
    x = x_ref[...] + jnp.dot(mg_ref[...], wout_ref[...], preferred_element_type=F32)
    h = _rms(x, g_ref[...]).astype(BF16)
    gate = jax.nn.sigmoid(jnp.dot(h, wpg_ref[...], preferred_element_type=F32))
    emb = jnp.dot(p_ref[...].astype(BF16), wpp_ref[...], preferred_element_type=F32)
    y = x + gate * emb
    if final:
        y = _rms(y, fg_ref[...])
    o_ref[...] = y


def _out_ple(merged, x2d, p3d, wout_bf, g, wpg_bf, wpp_bf, final_g, layer, final, w_next=None):
    t, d = x2d.shape
    tm = PLE_TM
    steps = t // tm
    in_extra, out_extra, args_extra = [], [], []
    out_shape = jax.ShapeDtypeStruct((t, d), F32)
    if w_next is not None:
        _, wd, wn = w_next.shape
        assert wd % (steps * BF16_SUBLANES) == 0
        in_extra = [pl.BlockSpec((None, wd // steps, wn), lambda i: (layer + 1, i, 0))]
        out_extra = [pl.BlockSpec((wd // steps, wn), lambda i: (i, 0))]
        args_extra = [w_next]
        out_shape = [out_shape, jax.ShapeDtypeStruct((wd, wn), BF16)]
    wspec = lambda r: pl.BlockSpec((None, r, d), lambda i: (layer, 0, 0),
                                   pipeline_mode=pl.Buffered(1))
    row = lambda w: pl.BlockSpec((tm, w), lambda i: (i, 0))
    out = pl.pallas_call(
        functools.partial(_out_ple_kernel, final=final, cast=w_next is not None),
        grid=(steps,),
        in_specs=[row(d), row(d), pl.BlockSpec((None, tm, PLE_DIM), lambda i: (layer, i, 0)),
                  wspec(d), pl.BlockSpec((None, 1, d), lambda i: (layer, 0, 0)), wspec(d),
                  wspec(PLE_DIM), pl.BlockSpec((1, d), lambda i: (0, 0))] + in_extra,
        out_specs=[row(d)] + out_extra if out_extra else row(d),
        out_shape=out_shape,
        compiler_params=_params(("parallel",)),
        name="out_ple",
    )(merged, x2d, p3d, wout_bf, g, wpg_bf, wpp_bf, final_g.reshape(1, d), *args_extra)
    if w_next is None:
        return out, None
    return out[0], out[1].reshape(1, wd, wn)


def kernel(x, p, rpe_table, norm_g, w_in, sinks, conv_w, conv_b, w_r, b_r, w_i, b_i, lam, w_br,
           w_out, ple_norm_g, w_pg, w_pp, final_norm_g):
    bsz, seq, d = x.shape
    depth = w_in.shape[0]
    assert d == D_MODEL and w_in.shape[2] == IN_WIDTH
    assert seq % MOBA_BLOCK == 0 and seq % LRU_CHUNK == 0 and (bsz * seq) % PROJ_TM == 0
    t = bsz * seq
    xf = x.reshape(t, d)
    table = rpe_table.astype(F32).T * LOG2E
    bias_a = _bias_lookup(table[:A_HEADS], _moba_bucket_planes(), 1)
    bias_b = _bias_lookup(table[A_HEADS:], _swa_bucket_planes(), B_GROUP)
    bf = lambda w: w.astype(BF16)
    vec = lambda v: v.reshape(depth, 1, v.shape[-1])
    w_r_bf, w_i_bf, w_br_bf, w_out_bf, w_pg_bf, w_pp_bf = map(
        bf, (w_r, w_i, w_br, w_out, w_pg, w_pp))
    w_in_cur = bf(w_in[:1])
    p3d = p.reshape(depth, t, PLE_DIM)
    for i in range(depth):
        proj, gates = _in_proj(xf, vec(norm_g), w_in_cur, i, 0)
        ya = _moba_branch(proj, bias_a, bsz, seq)
        yb = _swa_branch(proj, bias_b, _swa_sink_rows(sinks[i]), bsz, seq)
        yc = _rglru_branch(proj, conv_w, vec(conv_b), w_r_bf, vec(b_r), w_i_bf, vec(b_i),
                           vec(lam), i, bsz, seq)
        merged = _merge(ya, yb, yc, gates, w_br_bf, i)
        xf, w_in_cur = _out_ple(merged, xf, p3d, w_out_bf, vec(ple_norm_g), w_pg_bf, w_pp_bf,
                                final_norm_g, i, final=(i == depth - 1),
                                w_next=w_in if i + 1 < depth else None)
    return xf.reshape(bsz, seq, d)
```

```python
import functools
import math

import jax
import jax.numpy as jnp
import numpy as np
from jax import lax
from jax.experimental import pallas as pl
from jax.experimental.pallas import tpu as pltpu

F32 = jnp.float32
BF16 = jnp.bfloat16

D_MODEL = 2048
PLE_DIM = 256
BRANCH_WIDTH = 1024
A_HEADS = 8
A_HEAD_DIM = 128
MOBA_BLOCK = 256
MOBA_TOPK = 3
B_Q_HEADS = 16
B_KV_HEADS = 4
B_GROUP = B_Q_HEADS // B_KV_HEADS
B_HEAD_DIM = 64
WINDOW = 128
C_BLOCKS = 8
C_BLOCK_DIM = 128
CONV_WIDTH = 4
LRU_C = 8.0
RPE_BUCKETS = 32
RPE_MAX_DIST = 128
EPS = 1e-6
NEG = -1e30
LOG2E = math.log2(math.e)

COL_QA = 0
COL_KA = COL_QA + BRANCH_WIDTH
COL_VA = COL_KA + BRANCH_WIDTH
COL_GA = COL_VA + BRANCH_WIDTH
COL_QB = COL_GA + BRANCH_WIDTH
COL_KB = COL_QB + BRANCH_WIDTH
COL_VB = COL_KB + B_KV_HEADS * B_HEAD_DIM
COL_GB = COL_VB + B_KV_HEADS * B_HEAD_DIM
COL_XC = COL_GB + BRANCH_WIDTH
COL_GC = COL_XC + BRANCH_WIDTH
COL_MG = COL_GC + BRANCH_WIDTH
IN_WIDTH = COL_MG + 3 * D_MODEL

PROJ_TM = 2048
PROJ_TN = 512
MERGE_TM = 2048
MERGE_TN = 512
PLE_TM = 512
LRU_CHUNK = 256

VMEM_LIMIT = 62 * 1024 * 1024


def _params(sem):
    return pltpu.CompilerParams(dimension_semantics=sem, vmem_limit_bytes=VMEM_LIMIT)


def _rms(xf, g):
    return xf * lax.rsqrt(jnp.mean(xf * xf, axis=-1, keepdims=True) + EPS) * g


def _in_proj_kernel(x_ref, g_ref, w_ref, o_ref, mg_ref, h_sc):
    @pl.when(pl.program_id(1) == 0)
    def _():
        h_sc[...] = _rms(x_ref[...], g_ref[...]).astype(BF16)

    acc = jnp.dot(h_sc[...], w_ref[...], preferred_element_type=F32)
    o_ref[...] = acc
    mg_ref[...] = acc.astype(mg_ref.dtype)


def _in_proj(x2d, g, w_bf, layer, w_layer):
    t, d = x2d.shape
    n = w_bf.shape[2]
    assert COL_MG % PROJ_TN == 0
    main_tiles = COL_MG // PROJ_TN
    return pl.pallas_call(
        _in_proj_kernel,
        grid=(t // PROJ_TM, n // PROJ_TN),
        in_specs=[
            pl.BlockSpec((PROJ_TM, d), lambda i, j: (i, 0)),
            pl.BlockSpec((None, 1, d), lambda i, j: (layer, 0, 0)),
            pl.BlockSpec((None, d, PROJ_TN), lambda i, j: (w_layer, 0, j)),
        ],
        out_specs=[
            pl.BlockSpec((PROJ_TM, PROJ_TN), lambda i, j: (i, jnp.minimum(j, main_tiles))),
            pl.BlockSpec((PROJ_TM, PROJ_TN), lambda i, j: (i, jnp.maximum(j - main_tiles, 0))),
        ],
        out_shape=[jax.ShapeDtypeStruct((t, COL_MG + PROJ_TN), F32),
                   jax.ShapeDtypeStruct((t, n - COL_MG), BF16)],
        scratch_shapes=[pltpu.VMEM((PROJ_TM, d), BF16)],
        compiler_params=_params(("parallel", "arbitrary")),
        name="in_proj",
    )(x2d, g, w_bf)


MASKED_BUCKET = RPE_BUCKETS


def _t5_bucket_np(dist):
    n = np.maximum(dist, 0)
    max_exact = RPE_BUCKETS // 2
    ratio = np.log(np.maximum(n, 1).astype(np.float32) / np.float32(max_exact)) / np.float32(
        math.log(RPE_MAX_DIST / max_exact))
    large = max_exact + (ratio * np.float32(RPE_BUCKETS - max_exact)).astype(np.int32)
    large = np.minimum(large, RPE_BUCKETS - 1)
    return np.where(n < max_exact, n, large).astype(np.int32)


def _moba_bucket_planes():
    L = MOBA_BLOCK
    ki = np.arange(L)[:, None]
    qi = np.arange(L)[None, :]
    d_own = qi - ki
    d_far = 2 * L + qi - ki
    assert d_far.min() >= RPE_MAX_DIST
    own = np.where(d_own >= 0, _t5_bucket_np(d_own), MASKED_BUCKET)
    return np.stack([own, _t5_bucket_np(L + qi - ki), _t5_bucket_np(d_far)]).astype(np.int32)


def _swa_bucket_planes():
    W = WINDOW
    kk = np.arange(2 * W)[:, None]
    qi = np.arange(W)[None, :]
    dist = qi - kk + W
    ok = (dist >= 0) & (dist < W)
    b = _t5_bucket_np(dist)
    return np.stack([np.where(ok, b, MASKED_BUCKET),
                     np.where(ok & (kk >= W), b, MASKED_BUCKET)]).astype(np.int32)


def _bias_lookup_kernel(tab_ref, idx_ref, o_ref):
    h = pl.program_id(0)
    for p in range(idx_ref.shape[0]):
        idx = idx_ref[p]
        acc = jnp.full(idx.shape, NEG, F32)
        for b in range(RPE_BUCKETS):
            acc = jnp.where(idx == b, tab_ref[h * RPE_BUCKETS + b], acc)
        o_ref[0, p] = acc


def _bias_lookup(table_hb, planes, heads_per_row):
    nh = table_hb.shape[0]
    npl, r, c = planes.shape
    return pl.pallas_call(
        _bias_lookup_kernel,
        grid=(nh,),
        in_specs=[
            pl.BlockSpec(memory_space=pltpu.SMEM),
            pl.BlockSpec((npl, r, c), lambda h: (0, 0, 0)),
        ],
        out_specs=pl.BlockSpec((1, npl, r, c),
                               lambda h: (h // heads_per_row, 0, 0, h % heads_per_row)),
        out_shape=jax.ShapeDtypeStruct((nh // heads_per_row, npl, r, c * heads_per_row), F32),
        compiler_params=_params(("parallel",)),
        name="bias_lookup",
    )(table_hb.reshape(-1), jnp.asarray(planes))


GATE_ROWS = 16


def _silu(x):
    return x * jax.nn.sigmoid(x)


MOBA_GROUP = 2
MOBA_UNROLL = 2
MOBA_HEADS = 2


def _moba_prep_kernel(q_ref, k_ref, v_ref, kb_ref, vt_ref, qt_ref, mask_ref, kmean_sc, *, nb):
    L = MOBA_BLOCK
    dh = A_HEAD_DIM
    G = MOBA_GROUP

    kmean_sc[...] = jnp.zeros_like(kmean_sc)
    for n in range(nb):
        rows = slice((n % G) * L, (n % G + 1) * L)
        kn = k_ref[n * L:(n + 1) * L, :]
        kb_ref[0, n // G, rows, :] = kn.astype(BF16)
        kmean_sc[n:n + 1, :] = jnp.sum(kn, axis=0, keepdims=True) * (1.0 / L)
        vt_ref[0, n // G, :, rows] = v_ref[n * L:(n + 1) * L, :].T.astype(BF16)

    blk = lax.broadcasted_iota(jnp.int32, (GATE_ROWS, L), 0)
    blk_f = blk.astype(F32)
    for j in range(nb):
        qt = q_ref[j * L:(j + 1) * L, :].T
        gate = jnp.dot(kmean_sc[...], qt, precision=lax.Precision.HIGHEST,
                       preferred_element_type=F32)
        gm = jnp.where(blk < j, gate, NEG)
        picked = jnp.zeros((GATE_ROWS, L), F32)
        for _ in range(MOBA_TOPK):
            mx = jnp.max(gm, axis=0, keepdims=True)
            first = jnp.min(jnp.where(gm == mx, blk_f, float(GATE_ROWS)), axis=0, keepdims=True)
            hit = blk_f == first
            picked = jnp.where(hit, 1.0, picked)
            gm = jnp.where(hit, -jnp.inf, gm)
        allowed = ((picked > 0.0) & (blk < j)) | (blk == j)
        qt_ref[0, j] = (qt * (dh ** -0.5 * LOG2E)).astype(BF16)
        mask_ref[0, j] = jnp.where(allowed, 0.0, NEG).astype(BF16)


def _moba_prep(proj, bsz, seq):
    L = MOBA_BLOCK
    dh = A_HEAD_DIM
    G = MOBA_GROUP
    nb = seq // L
    nh = bsz * A_HEADS
    cq, ck, cv = (c // dh for c in (COL_QA, COL_KA, COL_VA))
    col = lambda c: pl.BlockSpec((seq, dh), lambda b, h: (b, c + h))
    shapes = [(nh, nb // G, G * L, dh), (nh, nb // G, dh, G * L), (nh, nb, dh, L),
              (nh, nb, GATE_ROWS, L)]
    return pl.pallas_call(
        functools.partial(_moba_prep_kernel, nb=nb),
        grid=(bsz, A_HEADS),
        in_specs=[col(cq), col(ck), col(cv)],
        out_specs=[pl.BlockSpec((1,) + s[1:], lambda b, h: (b * A_HEADS + h, 0, 0, 0))
                   for s in shapes],
        out_shape=[jax.ShapeDtypeStruct(s, BF16) for s in shapes],
        scratch_shapes=[pltpu.VMEM((GATE_ROWS, dh), F32)],
        compiler_params=_params(("parallel", "parallel")),
        name="moba_prep",
    )(proj, proj, proj)


def _moba_schedule(nb):
    G = MOBA_GROUP
    return [(j, i) for j in range(nb) for i in range((j + G) // G)]


def _moba_kernel(jt_ref, it_ref, kb_ref, vt_ref, qt_ref, mask_ref, g_ref, bias_ref, o_ref,
                 kaug_sc, qaug_sc, s_sc, p_sc, alpha_sc, lfin_sc, lout_sc, m_sc, l_sc, acc_sc, *,
                 sched):
    L = MOBA_BLOCK
    dh = A_HEAD_DIM
    G = MOBA_GROUP
    U = MOBA_UNROLL
    nt = len(sched)
    heads = range(MOBA_HEADS)
    hcol = lambda u: slice(u * dh, (u + 1) * dh)
    last_group = lambda j: j // G

    @pl.when((pl.program_id(0) == 0) & (pl.program_id(1) == 0))
    def _():
        lanes = lax.broadcasted_iota(jnp.int32, (L, dh), 1)
        for u in heads:
            for n in range(kaug_sc.shape[1] * G):
                rows = slice((n % G) * L, (n % G + 1) * L)
                kaug_sc[u, n // G, rows, dh:] = jnp.where(lanes == n, 1.0, 0.0).astype(BF16)
            qaug_sc[u, :, dh + GATE_ROWS:, :] = jnp.zeros(
                (qaug_sc.shape[1], dh - GATE_ROWS, L), BF16)

    for u in heads:
        kaug_sc[u, :, :, :dh] = kb_ref[u]
        qaug_sc[u, :, :dh, :] = qt_ref[u]
        qaug_sc[u, :, dh:dh + GATE_ROWS, :] = mask_ref[u]

    def scores(u, t, slot):
        s_sc[u, slot] = jnp.dot(kaug_sc[u, it_ref[t]], qaug_sc[u, jt_ref[t]],
                                preferred_element_type=F32)

    def softmax(u, t, slot):
        j = jt_ref[t]
        i = it_ref[t]
        s = s_sc[u, slot]
        parts = []
        for g in range(G):
            n = i * G + g
            plane = jnp.where(n == j, 0, jnp.where(n == j - 1, 1, 2))
            parts.append(s[g * L:(g + 1) * L, :] + bias_ref[u, plane])
        m_old = jnp.where(i == 0, NEG, m_sc[u])
        m_new = m_old
        for sp in parts:
            m_new = jnp.maximum(m_new, jnp.max(sp, axis=0, keepdims=True))
        alpha = jnp.exp2(m_old - m_new)
        probs = [jnp.exp2(sp - m_new) for sp in parts]
        l_new = alpha * l_sc[u]
        for pr in probs:
            l_new = l_new + jnp.sum(pr, axis=0, keepdims=True)
        p_sc[u, slot] = jnp.concatenate(probs, axis=0).astype(BF16)
        alpha_sc[u, slot] = alpha
        lfin_sc[u, slot] = l_new
        l_sc[u] = l_new
        m_sc[u] = m_new

    def weighted_values(u, t, slot, r):
        acc_sc[u, r] = alpha_sc[u, slot] * acc_sc[u, (r - 1) % U] + jnp.dot(
            vt_ref[u, it_ref[t]], p_sc[u, slot], preferred_element_type=F32)
        lout_sc[u, r] = lfin_sc[u, slot]

    def finish(u, j, r):
        out_t = acc_sc[u, r] / lout_sc[u, r]
        rows = pl.ds(pl.multiple_of(j * L, L), L)
        o_ref[rows, hcol(u)] = (out_t.T * _silu(g_ref[rows, hcol(u)])).astype(o_ref.dtype)

    s_sc[:, 1] = jnp.full(s_sc.shape[:1] + s_sc.shape[2:], -jnp.inf, F32)
    p_sc[:, 0] = jnp.zeros(p_sc.shape[:1] + p_sc.shape[2:], BF16)
    alpha_sc[...] = jnp.ones_like(alpha_sc)
    lfin_sc[...] = jnp.ones_like(lfin_sc)
    m_sc[...] = jnp.full_like(m_sc, NEG)
    l_sc[...] = jnp.zeros_like(l_sc)
    acc_sc[...] = jnp.zeros_like(acc_sc)

    def unrolled_steps(k, carry):
        base = U * k
        done = []
        for r in range(U):
            t = base + r
            t_pv = jnp.maximum(t - 2, 0)
            t_sm = jnp.maximum(t - 1, 0)
            for u in heads:
                weighted_values(u, t_pv, r % 2, r)
                softmax(u, t_sm, (r + 1) % 2)
                scores(u, t, r % 2)
            done.append((t - 2, t_pv, r))
        for t_real, t_pv, r in done:
            j_done = jt_ref[t_pv]

            @pl.when((t_real >= 0) & (it_ref[t_pv] == last_group(j_done)))
            def _():
                for u in heads:
                    finish(u, j_done, r)

        return carry

    assert U % 2 == 0
    n_loop = nt // U
    lax.fori_loop(0, n_loop, unrolled_steps, 0)
    for t in range(n_loop * U, nt + 2):
        r = t % U
        for u in heads:
            if 0 <= t - 2:
                weighted_values(u, t - 2, r % 2, r)
                j, i = sched[t - 2]
                if i == last_group(j):
                    finish(u, j, r)
            if 0 <= t - 1 < nt:
                softmax(u, t - 1, (r + 1) % 2)
            if t < nt:
                scores(u, t, r % 2)


def _moba_branch(proj, bias, bsz, seq):
    L = MOBA_BLOCK
    dh = A_HEAD_DIM
    G = MOBA_GROUP
    nb = seq // L
    hp = MOBA_HEADS
    assert min(MOBA_TOPK, nb - 1) == MOBA_TOPK and nb <= GATE_ROWS and nb % G == 0
    assert A_HEADS % hp == 0
    kb, vt, qt, mask = _moba_prep(proj, bsz, seq)
    sched = _moba_schedule(nb)
    jt = jnp.asarray(np.array([j for j, _ in sched], np.int32))
    it = jnp.asarray(np.array([i for _, i in sched], np.int32))
    pairs = A_HEADS // hp
    cg = COL_GA // (hp * dh)
    smem = pl.BlockSpec(memory_space=pltpu.SMEM)
    per_pair = lambda a: pl.BlockSpec((hp,) + a.shape[1:], lambda b, h: (b * pairs + h, 0, 0, 0))
    return pl.pallas_call(
        functools.partial(_moba_kernel, sched=sched),
        grid=(bsz, pairs),
        in_specs=[smem, smem, per_pair(kb), per_pair(vt), per_pair(qt), per_pair(mask),
                  pl.BlockSpec((seq, hp * dh), lambda b, h: (b, cg + h)),
                  pl.BlockSpec((hp, 3, L, L), lambda b, h: (h, 0, 0, 0))],
        out_specs=pl.BlockSpec((seq, hp * dh), lambda b, h: (b, h)),
        out_shape=jax.ShapeDtypeStruct((bsz * seq, BRANCH_WIDTH), BF16),
        scratch_shapes=[
            pltpu.VMEM((hp, nb // G, G * L, 2 * dh), BF16),
            pltpu.VMEM((hp, nb, 2 * dh, L), BF16),
            pltpu.VMEM((hp, 2, G * L, L), F32),
            pltpu.VMEM((hp, 2, G * L, L), BF16),
            pltpu.VMEM((hp, 2, 1, L), F32),
            pltpu.VMEM((hp, 2, 1, L), F32),
            pltpu.VMEM((hp, MOBA_UNROLL, 1, L), F32),
            pltpu.VMEM((hp, 1, L), F32),
            pltpu.VMEM((hp, 1, L), F32),
            pltpu.VMEM((hp, MOBA_UNROLL, dh, L), F32),
        ],
        compiler_params=_params(("arbitrary", "arbitrary")),
        name="moba",
    )(jt, it, kb, vt, qt, mask, proj, bias)


SWA_BLOCKS = 4


def _swa_kernel(q_ref, kp_ref, kc_ref, vp_ref, vc_ref, g0_ref, g1_ref, bias_ref, sink_ref, o_ref,
                ot_sc, x_sc, k_sc, vt_sc, s_sc, p_sc, d_sc):
    W = WINDOW
    dh = B_HEAD_DIM
    kvw = B_KV_HEADS * dh
    head = lambda g, h: slice((g * B_GROUP + h) * dh, (g * B_GROUP + h + 1) * dh)
    lane_group = lax.broadcasted_iota(jnp.int32, (2 * W, kvw), 1) // dh
    first = pl.program_id(1) == 0
    for u in range(SWA_BLOCKS):
        rows = slice(u * W, (u + 1) * W)
        if u == 0:
            k_prev, v_prev = kp_ref[...], vp_ref[...]
            plane = jnp.where(first, 1, 0)
        else:
            k_prev, v_prev = kc_ref[(u - 1) * W:u * W, :], vc_ref[(u - 1) * W:u * W, :]
            plane = 0
        qt = q_ref[rows, :].T * (dh ** -0.5 * LOG2E)
        x_sc[u] = jnp.concatenate(
            [jnp.concatenate([qt[head(g, h), :] for g in range(B_KV_HEADS)], axis=0)
             for h in range(B_GROUP)], axis=1).astype(BF16)
        kband = jnp.concatenate([k_prev, kc_ref[rows, :]], axis=0)
        vt_sc[u] = jnp.concatenate([v_prev, vc_ref[rows, :]], axis=0).T.astype(BF16)
        for g in range(B_KV_HEADS):
            k_sc[u, g] = jnp.where(lane_group == g, kband, 0.0).astype(BF16)
        for g in range(B_KV_HEADS):
            s_sc[u, g] = jnp.dot(k_sc[u, g], x_sc[u], preferred_element_type=F32)
        for g in range(B_KV_HEADS):
            s = s_sc[u, g] + bias_ref[g, plane]
            sink = sink_ref[g]
            m = jnp.maximum(jnp.max(s, axis=0, keepdims=True), sink)
            p = jnp.exp2(s - m)
            d_sc[u, g] = jnp.sum(p, axis=0, keepdims=True) + jnp.exp2(sink - m)
            p_sc[u, g] = p.astype(BF16)
        for g in range(B_KV_HEADS):
            o = jnp.dot(vt_sc[u, g * dh:(g + 1) * dh, :], p_sc[u, g],
                        preferred_element_type=F32) / d_sc[u, g]
            for h in range(B_GROUP):
                ot_sc[u, head(g, h), :] = o[:, h * W:(h + 1) * W]
        gate = jnp.concatenate([g0_ref[rows, :], g1_ref[rows, :]], axis=1)
        o_ref[rows, :] = (ot_sc[u].T * _silu(gate)).astype(o_ref.dtype)


def _swa_branch(proj, bias, sink, bsz, seq):
    W = WINDOW
    R = SWA_BLOCKS * W
    nb = seq // W
    ns = seq // R
    kvw = B_KV_HEADS * B_HEAD_DIM
    cq = COL_QB // BRANCH_WIDTH
    ck = COL_KB // kvw
    cv = COL_VB // kvw
    cg = COL_GB // (BRANCH_WIDTH // 2)
    assert COL_QB % BRANCH_WIDTH == 0 and COL_KB % kvw == 0 and COL_VB % kvw == 0
    assert COL_GB % (BRANCH_WIDTH // 2) == 0 and seq % R == 0
    prev = lambda b, n: (b * nb + jnp.maximum(SWA_BLOCKS * n - 1, 0))
    cur = lambda c, w: pl.BlockSpec((R, w), lambda b, n: (b * ns + n, c))
    return pl.pallas_call(
        _swa_kernel,
        grid=(bsz, ns),
        in_specs=[
            cur(cq, BRANCH_WIDTH),
            pl.BlockSpec((W, kvw), lambda b, n: (prev(b, n), ck)),
            cur(ck, kvw),
            pl.BlockSpec((W, kvw), lambda b, n: (prev(b, n), cv)),
            cur(cv, kvw),
            cur(cg, BRANCH_WIDTH // 2),
            cur(cg + 1, BRANCH_WIDTH // 2),
            pl.BlockSpec(bias.shape, lambda b, n: (0, 0, 0, 0)),
            pl.BlockSpec(sink.shape, lambda b, n: (0, 0, 0)),
        ],
        out_specs=cur(0, BRANCH_WIDTH),
        out_shape=jax.ShapeDtypeStruct((bsz * seq, BRANCH_WIDTH), BF16),
        scratch_shapes=[
            pltpu.VMEM((SWA_BLOCKS, BRANCH_WIDTH, W), F32),
            pltpu.VMEM((SWA_BLOCKS, kvw, B_GROUP * W), BF16),
            pltpu.VMEM((SWA_BLOCKS, B_KV_HEADS, 2 * W, kvw), BF16),
            pltpu.VMEM((SWA_BLOCKS, kvw, 2 * W), BF16),
            pltpu.VMEM((SWA_BLOCKS, B_KV_HEADS, 2 * W, B_GROUP * W), F32),
            pltpu.VMEM((SWA_BLOCKS, B_KV_HEADS, 2 * W, B_GROUP * W), BF16),
            pltpu.VMEM((SWA_BLOCKS, B_KV_HEADS, 1, B_GROUP * W), F32),
        ],
        compiler_params=_params(("parallel", "arbitrary")),
        name="swa",
    )(proj, proj, proj, proj, proj, proj, proj, bias, sink)


def _swa_sink_rows(sinks):
    s = (sinks.astype(F32) * LOG2E).reshape(B_KV_HEADS, 1, B_GROUP, 1)
    return jnp.broadcast_to(s, (B_KV_HEADS, 1, B_GROUP, WINDOW)).reshape(
        B_KV_HEADS, 1, B_GROUP * WINDOW)


SUBLANES = 8


def _scan_steps(a, b, index, length, axis):
    k = 1
    while k < length:
        keep = index >= k
        a_prev = jnp.where(keep, pltpu.roll(a, k, axis), 1.0)
        b_prev = jnp.where(keep, pltpu.roll(b, k, axis), 0.0)
        b = a * b_prev + b
        a = a * a_prev
        k *= 2
    return a, b


LRU_BLOCKS = 2


def _rglru_kernel(xc_ref, gc_ref, cw_ref, cb_ref, wr_ref, br_ref, wi_ref, bi_ref, lam_ref, o_ref,
                  xpad_sc, h_sc, a_sc, b_sc, hin_sc, *, seq):
    T = LRU_CHUNK
    C = C_BLOCK_DIM
    PAD = 8
    blocks = range(LRU_BLOCKS)
    ch = lambda u: slice(u * C, (u + 1) * C)
    for u in blocks:
        xpad_sc[u, :PAD, :] = jnp.zeros((PAD, C), F32)
        xpad_sc[u, PAD:, :] = xc_ref[:, ch(u)]
    h_sc[...] = jnp.zeros_like(h_sc)
    nlam = -lam_ref[...]
    softplus = jnp.maximum(nlam, 0.0) + jnp.log1p(jnp.exp(-jnp.abs(nlam)))
    decay = -LRU_C * softplus
    tiles = T // SUBLANES
    sub = lax.broadcasted_iota(jnp.int32, (tiles, SUBLANES, C), 1)
    tile_row = lax.broadcasted_iota(jnp.int32, (tiles, C), 0)

    def chunk(c, carry):
        t0 = pl.multiple_of(c * T, T)
        for u in blocks:
            xw = xpad_sc[u, pl.ds(t0, T + PAD), :]
            conv = cb_ref[:, ch(u)]
            for w in range(CONV_WIDTH):
                off = PAD - (CONV_WIDTH - 1) + w
                conv = conv + cw_ref[w:w + 1, ch(u)] * xw[off:off + T, :]
            cbf = conv.astype(BF16)
            r = jax.nn.sigmoid(jnp.dot(cbf, wr_ref[u], preferred_element_type=F32)
                               + br_ref[:, ch(u)])
            i = jax.nn.sigmoid(jnp.dot(cbf, wi_ref[u], preferred_element_type=F32)
                               + bi_ref[:, ch(u)])
            log_a = r * decay[:, ch(u)]
            a = jnp.exp(log_a)
            u2 = -jnp.tanh(log_a) * (a * a + 1.0)
            b = jnp.where(u2 > 0.0, u2 * lax.rsqrt(u2), 0.0) * (i * conv)
            a, b = _scan_steps(a.reshape(tiles, SUBLANES, C), b.reshape(tiles, SUBLANES, C),
                               sub, SUBLANES, 1)
            a_sc[u] = a.reshape(T, C)
            b_sc[u] = b.reshape(T, C)
            last = pl.ds(SUBLANES - 1, tiles, stride=SUBLANES)
            a_tile, b_tile = _scan_steps(a_sc[u, last, :], b_sc[u, last, :], tile_row, tiles, 0)
            h_out = a_tile * h_sc[u] + b_tile
            hin_sc[u] = jnp.where(tile_row >= 1, pltpu.roll(h_out, 1, 0), h_sc[u])
            h_sc[u] = h_out[tiles - 1:tiles, :]
            h_in = jnp.concatenate(
                [jnp.broadcast_to(hin_sc[u, t:t + 1, :], (SUBLANES, C)) for t in range(tiles)],
                axis=0)
            h = a_sc[u] * h_in + b_sc[u]
            o_ref[pl.ds(t0, T), ch(u)] = (h * _silu(gc_ref[pl.ds(t0, T), ch(u)])).astype(
                o_ref.dtype)
        return carry

    lax.fori_loop(0, seq // T, chunk, 0)


def _rglru_branch(proj, conv_w, conv_b, w_r_bf, b_r, w_i_bf, b_i, lam, layer, bsz, seq):
    nbk = LRU_BLOCKS
    cd = C_BLOCK_DIM
    wd = nbk * cd
    assert C_BLOCKS % nbk == 0 and COL_XC % wd == 0 and COL_GC % wd == 0
    cx = COL_XC // wd
    cg = COL_GC // wd
    vspec = pl.BlockSpec((None, 1, wd), lambda b, c: (layer, 0, c))
    wspec = pl.BlockSpec((None, nbk, cd, cd), lambda b, c: (layer, c, 0, 0))
    return pl.pallas_call(
        functools.partial(_rglru_kernel, seq=seq),
        grid=(bsz, C_BLOCKS // nbk),
        in_specs=[
            pl.BlockSpec((seq, wd), lambda b, c: (b, cx + c)),
            pl.BlockSpec((seq, wd), lambda b, c: (b, cg + c)),
            pl.BlockSpec((None, CONV_WIDTH, wd), lambda b, c: (layer, 0, c)),
            vspec, wspec, vspec, wspec, vspec, vspec,
        ],
        out_specs=pl.BlockSpec((seq, wd), lambda b, c: (b, c)),
        out_shape=jax.ShapeDtypeStruct((bsz * seq, BRANCH_WIDTH), BF16),
        scratch_shapes=[pltpu.VMEM((nbk, seq + 8, cd), F32), pltpu.VMEM((nbk, 1, cd), F32),
                        pltpu.VMEM((nbk, LRU_CHUNK, cd), F32),
                        pltpu.VMEM((nbk, LRU_CHUNK, cd), F32),
                        pltpu.VMEM((nbk, LRU_CHUNK // SUBLANES, cd), F32)],
        compiler_params=_params(("parallel", "parallel")),
        name="rglru",
    )(proj, proj, conv_w, conv_b, w_r_bf, b_r, w_i_bf, b_i, lam)


def _merge_kernel(ya_ref, yb_ref, yc_ref, ma_ref, mb_ref, mc_ref, wbr_ref, o_ref):
    merged = None
    for n, (y_ref, m_ref) in enumerate(((ya_ref, ma_ref), (yb_ref, mb_ref), (yc_ref, mc_ref))):
        term = jax.nn.sigmoid(m_ref[...].astype(F32)) * jnp.dot(y_ref[...], wbr_ref[n],
                                                                preferred_element_type=F32)
        merged = term if merged is None else merged + term
    o_ref[...] = merged.astype(o_ref.dtype)


def _merge(ya, yb, yc, gates, wbr_bf, layer):
    t = ya.shape[0]
    d = wbr_bf.shape[3]
    tm, tn = MERGE_TM, MERGE_TN
    assert d % tn == 0
    cm = 0
    per = d // tn
    yspec = pl.BlockSpec((tm, BRANCH_WIDTH), lambda i, c: (i, 0))
    mspec = lambda n: pl.BlockSpec((tm, tn), lambda i, c: (i, cm + n * per + c))
    return pl.pallas_call(
        _merge_kernel,
        grid=(t // tm, per),
        in_specs=[
            yspec, yspec, yspec, mspec(0), mspec(1), mspec(2),
            pl.BlockSpec((None, 3, BRANCH_WIDTH, tn), lambda i, c: (layer, 0, 0, c)),
        ],
        out_specs=pl.BlockSpec((tm, tn), lambda i, c: (i, c)),
        out_shape=jax.ShapeDtypeStruct((t, d), BF16),
        compiler_params=_params(("parallel", "arbitrary")),
        name="merge",
    )(ya, yb, yc, gates, gates, gates, wbr_bf)


BF16_SUBLANES = 16


def _out_ple_kernel(mg_ref, x_ref, p_ref, wout_ref, g_ref, wpg_ref, wpp_ref, fg_ref, *rest,
                    final, cast):
    if cast:
        wsrc_ref, o_ref, wdst_ref = rest
        wdst_ref[...] = wsrc_ref[...].astype(wdst_ref.dtype)
    else:
        (o_ref,) = rest
    x = x_ref[...] + jnp.dot(mg_ref[...], wout_ref[...], preferred_element_type=F32)
    h = _rms(x, g_ref[...]).astype(BF16)
    gate = jax.nn.sigmoid(jnp.dot(h, wpg_ref[...], preferred_element_type=F32))
    emb = jnp.dot(p_ref[...].astype(BF16), wpp_ref[...], preferred_element_type=F32)
    y = x + gate * emb
    if final:
        y = _rms(y, fg_ref[...])
    o_ref[...] = y


def _out_ple(merged, x2d, p3d, wout_bf, g, wpg_bf, wpp_bf, final_g, layer, final, w_next=None):
    t, d = x2d.shape
    tm = PLE_TM
    steps = t // tm
    in_extra, out_extra, args_extra = [], [], []
    out_shape = jax.ShapeDtypeStruct((t, d), F32)
    if w_next is not None:
        _, wd, wn = w_next.shape
        assert wd % (steps * BF16_SUBLANES) == 0
        in_extra = [pl.BlockSpec((None, wd // steps, wn), lambda i: (layer + 1, i, 0))]
        out_extra = [pl.BlockSpec((wd // steps, wn), lambda i: (i, 0))]
        args_extra = [w_next]
        out_shape = [out_shape, jax.ShapeDtypeStruct((wd, wn), BF16)]
    wspec = lambda r: pl.BlockSpec((None, r, d), lambda i: (layer, 0, 0),
                                   pipeline_mode=pl.Buffered(1))
    row = lambda w: pl.BlockSpec((tm, w), lambda i: (i, 0))
    out = pl.pallas_call(
        functools.partial(_out_ple_kernel, final=final, cast=w_next is not None),
        grid=(steps,),
        in_specs=[row(d), row(d), pl.BlockSpec((None, tm, PLE_DIM), lambda i: (layer, i, 0)),
                  wspec(d), pl.BlockSpec((None, 1, d), lambda i: (layer, 0, 0)), wspec(d),
                  wspec(PLE_DIM), pl.BlockSpec((1, d), lambda i: (0, 0))] + in_extra,
        out_specs=[row(d)] + out_extra if out_extra else row(d),
        out_shape=out_shape,
        compiler_params=_params(("parallel",)),
        name="out_ple",
    )(merged, x2d, p3d, wout_bf, g, wpg_bf, wpp_bf, final_g.reshape(1, d), *args_extra)
    if w_next is None:
        return out, None
    return out[0], out[1].reshape(1, wd, wn)


def kernel(x, p, rpe_table, norm_g, w_in, sinks, conv_w, conv_b, w_r, b_r, w_i, b_i, lam, w_br,
           w_out, ple_norm_g, w_pg, w_pp, final_norm_g):
    bsz, seq, d = x.shape
    depth = w_in.shape[0]
    assert d == D_MODEL and w_in.shape[2] == IN_WIDTH
    assert seq % MOBA_BLOCK == 0 and seq % LRU_CHUNK == 0 and (bsz * seq) % PROJ_TM == 0
    t = bsz * seq
    xf = x.reshape(t, d)
    table = rpe_table.astype(F32).T * LOG2E
    bias_a = _bias_lookup(table[:A_HEADS], _moba_bucket_planes(), 1)
    bias_b = _bias_lookup(table[A_HEADS:], _swa_bucket_planes(), B_GROUP)
    bf = lambda w: w.astype(BF16)
    vec = lambda v: v.reshape(depth, 1, v.shape[-1])
    w_r_bf, w_i_bf, w_br_bf, w_out_bf, w_pg_bf, w_pp_bf = map(
        bf, (w_r, w_i, w_br, w_out, w_pg, w_pp))
    w_in_cur = bf(w_in[:1])
    p3d = p.reshape(depth, t, PLE_DIM)
    for i in range(depth):
        proj, gates = _in_proj(xf, vec(norm_g), w_in_cur, i, 0)
        ya = _moba_branch(proj, bias_a, bsz, seq)
        yb = _swa_branch(proj, bias_b, _swa_sink_rows(sinks[i]), bsz, seq)
        yc = _rglru_branch(proj, conv_w, vec(conv_b), w_r_bf, vec(b_r), w_i_bf, vec(b_i),
                           vec(lam), i, bsz, seq)
        merged = _merge(ya, yb, yc, gates, w_br_bf, i)
        xf, w_in_cur = _out_ple(merged, xf, p3d, w_out_bf, vec(ple_norm_g), w_pg_bf, w_pp_bf,
                                final_norm_g, i, final=(i == depth - 1),
                                w_next=w_in if i + 1 < depth else None)
    return xf.reshape(bsz, seq, d)
```

```python
import functools
import math

import jax
import jax.numpy as jnp
import numpy as np
from jax import lax
from jax.experimental import pallas as pl
from jax.experimental.pallas import tpu as pltpu

F32 = jnp.float32
BF16 = jnp.bfloat16

D_MODEL = 2048
PLE_DIM = 256
BRANCH_WIDTH = 1024
A_HEADS = 8
A_HEAD_DIM = 128
MOBA_BLOCK = 256
MOBA_TOPK = 3
B_Q_HEADS = 16
B_KV_HEADS = 4
B_GROUP = B_Q_HEADS // B_KV_HEADS
B_HEAD_DIM = 64
WINDOW = 128
C_BLOCKS = 8
C_BLOCK_DIM = 128
CONV_WIDTH = 4
LRU_C = 8.0
RPE_BUCKETS = 32
RPE_MAX_DIST = 128
EPS = 1e-6
NEG = -1e30
LOG2E = math.log2(math.e)

COL_QA = 0
COL_KA = COL_QA + BRANCH_WIDTH
COL_VA = COL_KA + BRANCH_WIDTH
COL_GA = COL_VA + BRANCH_WIDTH
COL_QB = COL_GA + BRANCH_WIDTH
COL_KB = COL_QB + BRANCH_WIDTH
COL_VB = COL_KB + B_KV_HEADS * B_HEAD_DIM
COL_GB = COL_VB + B_KV_HEADS * B_HEAD_DIM
COL_XC = COL_GB + BRANCH_WIDTH
COL_GC = COL_XC + BRANCH_WIDTH
COL_MG = COL_GC + BRANCH_WIDTH
IN_WIDTH = COL_MG + 3 * D_MODEL

PROJ_TM = 2048
PROJ_TN = 512
MERGE_TM = 2048
MERGE_TN = 512
MERGE_ROWS = 256
PLE_TM = 512
LRU_CHUNK = 256

VMEM_LIMIT = 62 * 1024 * 1024


def _params(sem):
    return pltpu.CompilerParams(dimension_semantics=sem, vmem_limit_bytes=VMEM_LIMIT)


def _rms(xf, g):
    return xf * lax.rsqrt(jnp.mean(xf * xf, axis=-1, keepdims=True) + EPS) * g


def _in_proj_kernel(x_ref, g_ref, w_ref, o_ref, mg_ref, h_sc):
    @pl.when(pl.program_id(1) == 0)
    def _():
        h_sc[...] = _rms(x_ref[...], g_ref[...]).astype(BF16)

    acc = jnp.dot(h_sc[...], w_ref[...], preferred_element_type=F32)
    o_ref[...] = acc
    mg_ref[...] = acc.astype(mg_ref.dtype)


def _in_proj(x2d, g, w_bf, layer, w_layer):
    t, d = x2d.shape
    n = w_bf.shape[2]
    assert COL_MG % PROJ_TN == 0
    main_tiles = COL_MG // PROJ_TN
    return pl.pallas_call(
        _in_proj_kernel,
        grid=(t // PROJ_TM, n // PROJ_TN),
        in_specs=[
            pl.BlockSpec((PROJ_TM, d), lambda i, j: (i, 0)),
            pl.BlockSpec((None, 1, d), lambda i, j: (layer, 0, 0)),
            pl.BlockSpec((None, d, PROJ_TN), lambda i, j: (w_layer, 0, j)),
        ],
        out_specs=[
            pl.BlockSpec((PROJ_TM, PROJ_TN), lambda i, j: (i, jnp.minimum(j, main_tiles))),
            pl.BlockSpec((PROJ_TM, PROJ_TN), lambda i, j: (i, jnp.maximum(j - main_tiles, 0))),
        ],
        out_shape=[jax.ShapeDtypeStruct((t, COL_MG + PROJ_TN), F32),
                   jax.ShapeDtypeStruct((t, n - COL_MG), BF16)],
        scratch_shapes=[pltpu.VMEM((PROJ_TM, d), BF16)],
        compiler_params=_params(("parallel", "arbitrary")),
        name="in_proj",
    )(x2d, g, w_bf)


MASKED_BUCKET = RPE_BUCKETS


def _t5_bucket_np(dist):
    n = np.maximum(dist, 0)
    max_exact = RPE_BUCKETS // 2
    ratio = np.log(np.maximum(n, 1).astype(np.float32) / np.float32(max_exact)) / np.float32(
        math.log(RPE_MAX_DIST / max_exact))
    large = max_exact + (ratio * np.float32(RPE_BUCKETS - max_exact)).astype(np.int32)
    large = np.minimum(large, RPE_BUCKETS - 1)
    return np.where(n < max_exact, n, large).astype(np.int32)


def _moba_bucket_planes():
    L = MOBA_BLOCK
    ki = np.arange(L)[:, None]
    qi = np.arange(L)[None, :]
    d_own = qi - ki
    d_far = 2 * L + qi - ki
    assert d_far.min() >= RPE_MAX_DIST
    own = np.where(d_own >= 0, _t5_bucket_np(d_own), MASKED_BUCKET)
    return np.stack([own, _t5_bucket_np(L + qi - ki), _t5_bucket_np(d_far)]).astype(np.int32)


def _swa_bucket_planes():
    W = WINDOW
    kk = np.arange(2 * W)[:, None]
    qi = np.arange(W)[None, :]
    dist = qi - kk + W
    ok = (dist >= 0) & (dist < W)
    b = _t5_bucket_np(dist)
    return np.stack([np.where(ok, b, MASKED_BUCKET),
                     np.where(ok & (kk >= W), b, MASKED_BUCKET)]).astype(np.int32)


def _bias_lookup_kernel(tab_ref, idx_ref, o_ref):
    h = pl.program_id(0)
    for p in range(idx_ref.shape[0]):
        idx = idx_ref[p]
        acc = jnp.full(idx.shape, NEG, F32)
        for b in range(RPE_BUCKETS):
            acc = jnp.where(idx == b, tab_ref[h * RPE_BUCKETS + b], acc)
        o_ref[0, p] = acc


def _bias_lookup(table_hb, planes, heads_per_row):
    nh = table_hb.shape[0]
    npl, r, c = planes.shape
    return pl.pallas_call(
        _bias_lookup_kernel,
        grid=(nh,),
        in_specs=[
            pl.BlockSpec(memory_space=pltpu.SMEM),
            pl.BlockSpec((npl, r, c), lambda h: (0, 0, 0)),
        ],
        out_specs=pl.BlockSpec((1, npl, r, c),
                               lambda h: (h // heads_per_row, 0, 0, h % heads_per_row)),
        out_shape=jax.ShapeDtypeStruct((nh // heads_per_row, npl, r, c * heads_per_row), F32),
        compiler_params=_params(("parallel",)),
        name="bias_lookup",
    )(table_hb.reshape(-1), jnp.asarray(planes))


GATE_ROWS = 16


_sigmoid = jax.nn.sigmoid


def _silu(x):
    return x * _sigmoid(x)


MOBA_GROUP = 2
MOBA_UNROLL = 2
MOBA_HEADS = 2


def _moba_prep_kernel(q_ref, k_ref, v_ref, kb_ref, vt_ref, qt_ref, mask_ref, kmean_sc, *, nb):
    L = MOBA_BLOCK
    dh = A_HEAD_DIM
    G = MOBA_GROUP

    kmean_sc[...] = jnp.zeros_like(kmean_sc)
    for n in range(nb):
        rows = slice((n % G) * L, (n % G + 1) * L)
        kn = k_ref[n * L:(n + 1) * L, :]
        kb_ref[0, n // G, rows, :] = kn.astype(BF16)
        kmean_sc[n:n + 1, :] = jnp.sum(kn, axis=0, keepdims=True) * (1.0 / L)
        vt_ref[0, n // G, :, rows] = v_ref[n * L:(n + 1) * L, :].T.astype(BF16)

    blk = lax.broadcasted_iota(jnp.int32, (GATE_ROWS, L), 0)
    blk_f = blk.astype(F32)
    for j in range(nb):
        qt = q_ref[j * L:(j + 1) * L, :].T
        gate = jnp.dot(kmean_sc[...], qt, precision=lax.Precision.HIGHEST,
                       preferred_element_type=F32)
        gm = jnp.where(blk < j, gate, NEG)
        picked = jnp.zeros((GATE_ROWS, L), F32)
        for _ in range(MOBA_TOPK):
            mx = jnp.max(gm, axis=0, keepdims=True)
            first = jnp.min(jnp.where(gm == mx, blk_f, float(GATE_ROWS)), axis=0, keepdims=True)
            hit = blk_f == first
            picked = jnp.where(hit, 1.0, picked)
            gm = jnp.where(hit, -jnp.inf, gm)
        allowed = ((picked > 0.0) & (blk < j)) | (blk == j)
        qt_ref[0, j] = (qt * (dh ** -0.5 * LOG2E)).astype(BF16)
        mask_ref[0, j] = jnp.where(allowed, 0.0, NEG).astype(BF16)


def _moba_prep(proj, bsz, seq):
    L = MOBA_BLOCK
    dh = A_HEAD_DIM
    G = MOBA_GROUP
    nb = seq // L
    nh = bsz * A_HEADS
    cq, ck, cv = (c // dh for c in (COL_QA, COL_KA, COL_VA))
    col = lambda c: pl.BlockSpec((seq, dh), lambda b, h: (b, c + h))
    shapes = [(nh, nb // G, G * L, dh), (nh, nb // G, dh, G * L), (nh, nb, dh, L),
              (nh, nb, GATE_ROWS, L)]
    return pl.pallas_call(
        functools.partial(_moba_prep_kernel, nb=nb),
        grid=(bsz, A_HEADS),
        in_specs=[col(cq), col(ck), col(cv)],
        out_specs=[pl.BlockSpec((1,) + s[1:], lambda b, h: (b * A_HEADS + h, 0, 0, 0))
                   for s in shapes],
        out_shape=[jax.ShapeDtypeStruct(s, BF16) for s in shapes],
        scratch_shapes=[pltpu.VMEM((GATE_ROWS, dh), F32)],
        compiler_params=_params(("parallel", "parallel")),
        name="moba_prep",
    )(proj, proj, proj)


def _moba_schedule(nb):
    G = MOBA_GROUP
    return [(j, i) for j in range(nb) for i in range((j + G) // G)]


def _moba_kernel(jt_ref, it_ref, kb_ref, vt_ref, qt_ref, mask_ref, g_ref, bias_ref, o_ref,
                 kaug_sc, qaug_sc, s_sc, p_sc, alpha_sc, lfin_sc, lout_sc, m_sc, l_sc, acc_sc, *,
                 sched):
    L = MOBA_BLOCK
    dh = A_HEAD_DIM
    G = MOBA_GROUP
    U = MOBA_UNROLL
    nt = len(sched)
    heads = range(MOBA_HEADS)
    hcol = lambda u: slice(u * dh, (u + 1) * dh)
    last_group = lambda j: j // G

    @pl.when((pl.program_id(0) == 0) & (pl.program_id(1) == 0))
    def _():
        lanes = lax.broadcasted_iota(jnp.int32, (L, dh), 1)
        for u in heads:
            for n in range(kaug_sc.shape[1] * G):
                rows = slice((n % G) * L, (n % G + 1) * L)
                kaug_sc[u, n // G, rows, dh:] = jnp.where(lanes == n, 1.0, 0.0).astype(BF16)
            qaug_sc[u, :, dh + GATE_ROWS:, :] = jnp.zeros(
                (qaug_sc.shape[1], dh - GATE_ROWS, L), BF16)

    for u in heads:
        kaug_sc[u, :, :, :dh] = kb_ref[u]
        qaug_sc[u, :, :dh, :] = qt_ref[u]
        qaug_sc[u, :, dh:dh + GATE_ROWS, :] = mask_ref[u]

    def scores(u, t, slot):
        s_sc[u, slot] = jnp.dot(kaug_sc[u, it_ref[t]], qaug_sc[u, jt_ref[t]],
                                preferred_element_type=F32)

    def softmax(u, t, slot):
        j = jt_ref[t]
        i = it_ref[t]
        s = s_sc[u, slot]
        parts = []
        for g in range(G):
            n = i * G + g
            plane = jnp.where(n == j, 0, jnp.where(n == j - 1, 1, 2))
            parts.append(s[g * L:(g + 1) * L, :] + bias_ref[u, plane])
        m_old = jnp.where(i == 0, NEG, m_sc[u])
        m_new = m_old
        for sp in parts:
            m_new = jnp.maximum(m_new, jnp.max(sp, axis=0, keepdims=True))
        alpha = jnp.exp2(m_old - m_new)
        probs = [jnp.exp2(sp - m_new) for sp in parts]
        l_new = alpha * l_sc[u]
        for pr in probs:
            l_new = l_new + jnp.sum(pr, axis=0, keepdims=True)
        p_sc[u, slot] = jnp.concatenate(probs, axis=0).astype(BF16)
        alpha_sc[u, slot] = alpha
        lfin_sc[u, slot] = l_new
        l_sc[u] = l_new
        m_sc[u] = m_new

    def weighted_values(u, t, slot, r):
        acc_sc[u, r] = alpha_sc[u, slot] * acc_sc[u, (r - 1) % U] + jnp.dot(
            vt_ref[u, it_ref[t]], p_sc[u, slot], preferred_element_type=F32)
        lout_sc[u, r] = lfin_sc[u, slot]

    def finish(u, j, r):
        out_t = acc_sc[u, r] / lout_sc[u, r]
        rows = pl.ds(pl.multiple_of(j * L, L), L)
        o_ref[rows, hcol(u)] = (out_t.T * _silu(g_ref[rows, hcol(u)])).astype(o_ref.dtype)

    s_sc[:, 1] = jnp.full(s_sc.shape[:1] + s_sc.shape[2:], -jnp.inf, F32)
    p_sc[:, 0] = jnp.zeros(p_sc.shape[:1] + p_sc.shape[2:], BF16)
    alpha_sc[...] = jnp.ones_like(alpha_sc)
    lfin_sc[...] = jnp.ones_like(lfin_sc)
    m_sc[...] = jnp.full_like(m_sc, NEG)
    l_sc[...] = jnp.zeros_like(l_sc)
    acc_sc[...] = jnp.zeros_like(acc_sc)

    def unrolled_steps(k, carry):
        base = U * k
        done = []
        for r in range(U):
            t = base + r
            t_pv = jnp.maximum(t - 2, 0)
            t_sm = jnp.maximum(t - 1, 0)
            for u in heads:
                weighted_values(u, t_pv, r % 2, r)
                softmax(u, t_sm, (r + 1) % 2)
                scores(u, t, r % 2)
            done.append((t - 2, t_pv, r))
        for t_real, t_pv, r in done:
            j_done = jt_ref[t_pv]

            @pl.when((t_real >= 0) & (it_ref[t_pv] == last_group(j_done)))
            def _():
                for u in heads:
                    finish(u, j_done, r)

        return carry

    assert U % 2 == 0
    n_loop = nt // U
    lax.fori_loop(0, n_loop, unrolled_steps, 0)
    for t in range(n_loop * U, nt + 2):
        r = t % U
        for u in heads:
            if 0 <= t - 2:
                weighted_values(u, t - 2, r % 2, r)
                j, i = sched[t - 2]
                if i == last_group(j):
                    finish(u, j, r)
            if 0 <= t - 1 < nt:
                softmax(u, t - 1, (r + 1) % 2)
            if t < nt:
                scores(u, t, r % 2)


def _moba_branch(proj, bias, bsz, seq):
    L = MOBA_BLOCK
    dh = A_HEAD_DIM
    G = MOBA_GROUP
    nb = seq // L
    hp = MOBA_HEADS
    assert min(MOBA_TOPK, nb - 1) == MOBA_TOPK and nb <= GATE_ROWS and nb % G == 0
    assert A_HEADS % hp == 0
    kb, vt, qt, mask = _moba_prep(proj, bsz, seq)
    sched = _moba_schedule(nb)
    jt = jnp.asarray(np.array([j for j, _ in sched], np.int32))
    it = jnp.asarray(np.array([i for _, i in sched], np.int32))
    pairs = A_HEADS // hp
    cg = COL_GA // (hp * dh)
    smem = pl.BlockSpec(memory_space=pltpu.SMEM)
    per_pair = lambda a: pl.BlockSpec((hp,) + a.shape[1:], lambda b, h: (b * pairs + h, 0, 0, 0))
    return pl.pallas_call(
        functools.partial(_moba_kernel, sched=sched),
        grid=(bsz, pairs),
        in_specs=[smem, smem, per_pair(kb), per_pair(vt), per_pair(qt), per_pair(mask),
                  pl.BlockSpec((seq, hp * dh), lambda b, h: (b, cg + h)),
                  pl.BlockSpec((hp, 3, L, L), lambda b, h: (h, 0, 0, 0))],
        out_specs=pl.BlockSpec((seq, hp * dh), lambda b, h: (b, h)),
        out_shape=jax.ShapeDtypeStruct((bsz * seq, BRANCH_WIDTH), BF16),
        scratch_shapes=[
            pltpu.VMEM((hp, nb // G, G * L, 2 * dh), BF16),
            pltpu.VMEM((hp, nb, 2 * dh, L), BF16),
            pltpu.VMEM((hp, 2, G * L, L), F32),
            pltpu.VMEM((hp, 2, G * L, L), BF16),
            pltpu.VMEM((hp, 2, 1, L), F32),
            pltpu.VMEM((hp, 2, 1, L), F32),
            pltpu.VMEM((hp, MOBA_UNROLL, 1, L), F32),
            pltpu.VMEM((hp, 1, L), F32),
            pltpu.VMEM((hp, 1, L), F32),
            pltpu.VMEM((hp, MOBA_UNROLL, dh, L), F32),
        ],
        compiler_params=_params(("arbitrary", "arbitrary")),
        name="moba",
    )(jt, it, kb, vt, qt, mask, proj, bias)


SWA_BLOCKS = 4


def _swa_kernel(q_ref, kp_ref, kc_ref, vp_ref, vc_ref, g0_ref, g1_ref, bias_ref, sink_ref, o_ref,
                ot_sc, x_sc, k_sc, vt_sc, s_sc, p_sc, d_sc):
    W = WINDOW
    dh = B_HEAD_DIM
    kvw = B_KV_HEADS * dh
    head = lambda g, h: slice((g * B_GROUP + h) * dh, (g * B_GROUP + h + 1) * dh)
    lane_group = lax.broadcasted_iota(jnp.int32, (2 * W, kvw), 1) // dh
    first = pl.program_id(1) == 0
    for u in range(SWA_BLOCKS):
        rows = slice(u * W, (u + 1) * W)
        if u == 0:
            k_prev, v_prev = kp_ref[...], vp_ref[...]
            plane = jnp.where(first, 1, 0)
        else:
            k_prev, v_prev = kc_ref[(u - 1) * W:u * W, :], vc_ref[(u - 1) * W:u * W, :]
            plane = 0
        qt = q_ref[rows, :].T * (dh ** -0.5 * LOG2E)
        x_sc[u] = jnp.concatenate(
            [jnp.concatenate([qt[head(g, h), :] for g in range(B_KV_HEADS)], axis=0)
             for h in range(B_GROUP)], axis=1).astype(BF16)
        kband = jnp.concatenate([k_prev, kc_ref[rows, :]], axis=0)
        vt_sc[u] = jnp.concatenate([v_prev, vc_ref[rows, :]], axis=0).T.astype(BF16)
        for g in range(B_KV_HEADS):
            k_sc[u, g] = jnp.where(lane_group == g, kband, 0.0).astype(BF16)
        for g in range(B_KV_HEADS):
            s_sc[u, g] = jnp.dot(k_sc[u, g], x_sc[u], preferred_element_type=F32)
        for g in range(B_KV_HEADS):
            s = s_sc[u, g] + bias_ref[g, plane]
            sink = sink_ref[g]
            m = jnp.maximum(jnp.max(s, axis=0, keepdims=True), sink)
            p = jnp.exp2(s - m)
            d_sc[u, g] = jnp.sum(p, axis=0, keepdims=True) + jnp.exp2(sink - m)
            p_sc[u, g] = p.astype(BF16)
        for g in range(B_KV_HEADS):
            o = jnp.dot(vt_sc[u, g * dh:(g + 1) * dh, :], p_sc[u, g],
                        preferred_element_type=F32) / d_sc[u, g]
            for h in range(B_GROUP):
                ot_sc[u, head(g, h), :] = o[:, h * W:(h + 1) * W]
        gate = jnp.concatenate([g0_ref[rows, :], g1_ref[rows, :]], axis=1)
        o_ref[rows, :] = (ot_sc[u].T * _silu(gate)).astype(o_ref.dtype)


def _swa_branch(proj, bias, sink, bsz, seq):
    W = WINDOW
    R = SWA_BLOCKS * W
    nb = seq // W
    ns = seq // R
    kvw = B_KV_HEADS * B_HEAD_DIM
    cq = COL_QB // BRANCH_WIDTH
    ck = COL_KB // kvw
    cv = COL_VB // kvw
    cg = COL_GB // (BRANCH_WIDTH // 2)
    assert COL_QB % BRANCH_WIDTH == 0 and COL_KB % kvw == 0 and COL_VB % kvw == 0
    assert COL_GB % (BRANCH_WIDTH // 2) == 0 and seq % R == 0
    prev = lambda b, n: (b * nb + jnp.maximum(SWA_BLOCKS * n - 1, 0))
    cur = lambda c, w: pl.BlockSpec((R, w), lambda b, n: (b * ns + n, c))
    return pl.pallas_call(
        _swa_kernel,
        grid=(bsz, ns),
        in_specs=[
            cur(cq, BRANCH_WIDTH),
            pl.BlockSpec((W, kvw), lambda b, n: (prev(b, n), ck)),
            cur(ck, kvw),
            pl.BlockSpec((W, kvw), lambda b, n: (prev(b, n), cv)),
            cur(cv, kvw),
            cur(cg, BRANCH_WIDTH // 2),
            cur(cg + 1, BRANCH_WIDTH // 2),
            pl.BlockSpec(bias.shape, lambda b, n: (0, 0, 0, 0)),
            pl.BlockSpec(sink.shape, lambda b, n: (0, 0, 0)),
        ],
        out_specs=cur(0, BRANCH_WIDTH),
        out_shape=jax.ShapeDtypeStruct((bsz * seq, BRANCH_WIDTH), BF16),
        scratch_shapes=[
            pltpu.VMEM((SWA_BLOCKS, BRANCH_WIDTH, W), F32),
            pltpu.VMEM((SWA_BLOCKS, kvw, B_GROUP * W), BF16),
            pltpu.VMEM((SWA_BLOCKS, B_KV_HEADS, 2 * W, kvw), BF16),
            pltpu.VMEM((SWA_BLOCKS, kvw, 2 * W), BF16),
            pltpu.VMEM((SWA_BLOCKS, B_KV_HEADS, 2 * W, B_GROUP * W), F32),
            pltpu.VMEM((SWA_BLOCKS, B_KV_HEADS, 2 * W, B_GROUP * W), BF16),
            pltpu.VMEM((SWA_BLOCKS, B_KV_HEADS, 1, B_GROUP * W), F32),
        ],
        compiler_params=_params(("parallel", "arbitrary")),
        name="swa",
    )(proj, proj, proj, proj, proj, proj, proj, bias, sink)


def _swa_sink_rows(sinks):
    s = (sinks.astype(F32) * LOG2E).reshape(B_KV_HEADS, 1, B_GROUP, 1)
    return jnp.broadcast_to(s, (B_KV_HEADS, 1, B_GROUP, WINDOW)).reshape(
        B_KV_HEADS, 1, B_GROUP * WINDOW)


SUBLANES = 8


def _scan_steps(a, b, index, length, axis):
    k = 1
    while k < length:
        keep = index >= k
        a_prev = jnp.where(keep, pltpu.roll(a, k, axis), 1.0)
        b_prev = jnp.where(keep, pltpu.roll(b, k, axis), 0.0)
        b = a * b_prev + b
        a = a * a_prev
        k *= 2
    return a, b


LRU_BLOCKS = 2


def _rglru_kernel(xc_ref, gc_ref, cw_ref, cb_ref, wr_ref, br_ref, wi_ref, bi_ref, lam_ref, o_ref,
                  xpad_sc, h_sc, a_sc, b_sc, hin_sc, *, seq):
    T = LRU_CHUNK
    C = C_BLOCK_DIM
    PAD = 8
    blocks = range(LRU_BLOCKS)
    ch = lambda u: slice(u * C, (u + 1) * C)
    for u in blocks:
        xpad_sc[u, :PAD, :] = jnp.zeros((PAD, C), F32)
        xpad_sc[u, PAD:, :] = xc_ref[:, ch(u)]
    h_sc[...] = jnp.zeros_like(h_sc)
    nlam = -lam_ref[...]
    softplus = jnp.maximum(nlam, 0.0) + jnp.log1p(jnp.exp(-jnp.abs(nlam)))
    decay = -LRU_C * softplus
    tiles = T // SUBLANES
    sub = lax.broadcasted_iota(jnp.int32, (tiles, SUBLANES, C), 1)
    tile_row = lax.broadcasted_iota(jnp.int32, (tiles, C), 0)

    def chunk(c, carry):
        t0 = pl.multiple_of(c * T, T)
        for u in blocks:
            conv = cb_ref[:, ch(u)]
            for w in range(CONV_WIDTH):
                off = PAD - (CONV_WIDTH - 1) + w
                conv = conv + cw_ref[w:w + 1, ch(u)] * xpad_sc[u, pl.ds(t0 + off, T), :]
            cbf = conv.astype(BF16)
            r = _sigmoid(jnp.dot(cbf, wr_ref[u], preferred_element_type=F32) + br_ref[:, ch(u)])
            i = _sigmoid(jnp.dot(cbf, wi_ref[u], preferred_element_type=F32) + bi_ref[:, ch(u)])
            log_a = r * decay[:, ch(u)]
            a = jnp.exp(log_a)
            u2 = -jnp.tanh(log_a) * (a * a + 1.0)
            b = jnp.where(u2 > 0.0, u2 * lax.rsqrt(u2), 0.0) * (i * conv)
            a, b = _scan_steps(a.reshape(tiles, SUBLANES, C), b.reshape(tiles, SUBLANES, C),
                               sub, SUBLANES, 1)
            a_sc[u] = a.reshape(T, C)
            b_sc[u] = b.reshape(T, C)
            last = pl.ds(SUBLANES - 1, tiles, stride=SUBLANES)
            a_tile, b_tile = _scan_steps(a_sc[u, last, :], b_sc[u, last, :], tile_row, tiles, 0)
            h_out = a_tile * h_sc[u] + b_tile
            hin_sc[u] = jnp.where(tile_row >= 1, pltpu.roll(h_out, 1, 0), h_sc[u])
            h_sc[u] = h_out[tiles - 1:tiles, :]
            h_in = jnp.concatenate(
                [jnp.broadcast_to(hin_sc[u, t:t + 1, :], (SUBLANES, C)) for t in range(tiles)],
                axis=0)
            h = a_sc[u] * h_in + b_sc[u]
            o_ref[pl.ds(t0, T), ch(u)] = (h * _silu(gc_ref[pl.ds(t0, T), ch(u)])).astype(
                o_ref.dtype)
        return carry

    lax.fori_loop(0, seq // T, chunk, 0)


def _rglru_branch(proj, conv_w, conv_b, w_r_bf, b_r, w_i_bf, b_i, lam, layer, bsz, seq):
    nbk = LRU_BLOCKS
    cd = C_BLOCK_DIM
    wd = nbk * cd
    assert C_BLOCKS % nbk == 0 and COL_XC % wd == 0 and COL_GC % wd == 0
    cx = COL_XC // wd
    cg = COL_GC // wd
    vspec = pl.BlockSpec((None, 1, wd), lambda b, c: (layer, 0, c))
    wspec = pl.BlockSpec((None, nbk, cd, cd), lambda b, c: (layer, c, 0, 0))
    return pl.pallas_call(
        functools.partial(_rglru_kernel, seq=seq),
        grid=(bsz, C_BLOCKS // nbk),
        in_specs=[
            pl.BlockSpec((seq, wd), lambda b, c: (b, cx + c)),
            pl.BlockSpec((seq, wd), lambda b, c: (b, cg + c)),
            pl.BlockSpec((None, CONV_WIDTH, wd), lambda b, c: (layer, 0, c)),
            vspec, wspec, vspec, wspec, vspec, vspec,
        ],
        out_specs=pl.BlockSpec((seq, wd), lambda b, c: (b, c)),
        out_shape=jax.ShapeDtypeStruct((bsz * seq, BRANCH_WIDTH), BF16),
        scratch_shapes=[pltpu.VMEM((nbk, seq + 8, cd), F32), pltpu.VMEM((nbk, 1, cd), F32),
                        pltpu.VMEM((nbk, LRU_CHUNK, cd), F32),
                        pltpu.VMEM((nbk, LRU_CHUNK, cd), F32),
                        pltpu.VMEM((nbk, LRU_CHUNK // SUBLANES, cd), F32)],
        compiler_params=_params(("parallel", "parallel")),
        name="rglru",
    )(proj, proj, conv_w, conv_b, w_r_bf, b_r, w_i_bf, b_i, lam)


def _merge_kernel(ya_ref, yb_ref, yc_ref, ma_ref, mb_ref, mc_ref, wbr_ref, o_ref):
    for c in range(o_ref.shape[0] // MERGE_ROWS):
        rows = slice(c * MERGE_ROWS, (c + 1) * MERGE_ROWS)
        merged = None
        for n, (y_ref, m_ref) in enumerate(((ya_ref, ma_ref), (yb_ref, mb_ref), (yc_ref, mc_ref))):
            term = _sigmoid(m_ref[rows, :].astype(F32)) * jnp.dot(
                y_ref[rows, :], wbr_ref[n], preferred_element_type=F32)
            merged = term if merged is None else merged + term
        o_ref[rows, :] = merged.astype(o_ref.dtype)


def _merge(ya, yb, yc, gates, wbr_bf, layer):
    t = ya.shape[0]
    d = wbr_bf.shape[3]
    tm, tn = MERGE_TM, MERGE_TN
    assert d % tn == 0
    cm = 0
    per = d // tn
    yspec = pl.BlockSpec((tm, BRANCH_WIDTH), lambda i, c: (i, 0))
    mspec = lambda n: pl.BlockSpec((tm, tn), lambda i, c: (i, cm + n * per + c))
    return pl.pallas_call(
        _merge_kernel,
        grid=(t // tm, per),
        in_specs=[
            yspec, yspec, yspec, mspec(0), mspec(1), mspec(2),
            pl.BlockSpec((None, 3, BRANCH_WIDTH, tn), lambda i, c: (layer, 0, 0, c)),
        ],
        out_specs=pl.BlockSpec((tm, tn), lambda i, c: (i, c)),
        out_shape=jax.ShapeDtypeStruct((t, d), BF16),
        compiler_params=_params(("parallel", "arbitrary")),
        name="merge",
    )(ya, yb, yc, gates, gates, gates, wbr_bf)


BF16_SUBLANES = 16


def _out_ple_kernel(mg_ref, x_ref, p_ref, wout_ref, g_ref, wpg_ref, wpp_ref, fg_ref, *rest,
                    final, cast):
    if cast:
        wsrc_ref, o_ref, wdst_ref = rest
        wdst_ref[...] = wsrc_ref[...].astype(wdst_ref.dtype)
    else:
        (o_ref,) = rest
    x = x_ref[...] + jnp.dot(mg_ref[...], wout_ref[...], preferred_element_type=F32)
    h = _rms(x, g_ref[...]).astype(BF16)
    gate = _sigmoid(jnp.dot(h, wpg_ref[...], preferred_element_type=F32))
    emb = jnp.dot(p_ref[...].astype(BF16), wpp_ref[...], preferred_element_type=F32)
    y = x + gate * emb
    if final:
        y = _rms(y, fg_ref[...])
    o_ref[...] = y


def _out_ple(merged, x2d, p3d, wout_bf, g, wpg_bf, wpp_bf, final_g, layer, final, w_next=None):
    t, d = x2d.shape
    tm = PLE_TM
    steps = t // tm
    in_extra, out_extra, args_extra = [], [], []
    out_shape = jax.ShapeDtypeStruct((t, d), F32)
    if w_next is not None:
        _, wd, wn = w_next.shape
        assert wd % (steps * BF16_SUBLANES) == 0
        in_extra = [pl.BlockSpec((None, wd // steps, wn), lambda i: (layer + 1, i, 0))]
        out_extra = [pl.BlockSpec((wd // steps, wn), lambda i: (i, 0))]
        args_extra = [w_next]
        out_shape = [out_shape, jax.ShapeDtypeStruct((wd, wn), BF16)]
    wspec = lambda r: pl.BlockSpec((None, r, d), lambda i: (layer, 0, 0),
                                   pipeline_mode=pl.Buffered(1))
    row = lambda w: pl.BlockSpec((tm, w), lambda i: (i, 0))
    out = pl.pallas_call(
        functools.partial(_out_ple_kernel, final=final, cast=w_next is not None),
        grid=(steps,),
        in_specs=[row(d), row(d), pl.BlockSpec((None, tm, PLE_DIM), lambda i: (layer, i, 0)),
                  wspec(d), pl.BlockSpec((None, 1, d), lambda i: (layer, 0, 0)), wspec(d),
                  wspec(PLE_DIM), pl.BlockSpec((1, d), lambda i: (0, 0))] + in_extra,
        out_specs=[row(d)] + out_extra if out_extra else row(d),
        out_shape=out_shape,
        compiler_params=_params(("parallel",)),
        name="out_ple",
    )(merged, x2d, p3d, wout_bf, g, wpg_bf, wpp_bf, final_g.reshape(1, d), *args_extra)
    if w_next is None:
        return out, None
    return out[0], out[1].reshape(1, wd, wn)


def kernel(x, p, rpe_table, norm_g, w_in, sinks, conv_w, conv_b, w_r, b_r, w_i, b_i, lam, w_br,
           w_out, ple_norm_g, w_pg, w_pp, final_norm_g):
    bsz, seq, d = x.shape
    depth = w_in.shape[0]
    assert d == D_MODEL and w_in.shape[2] == IN_WIDTH
    assert seq % MOBA_BLOCK == 0 and seq % LRU_CHUNK == 0 and (bsz * seq) % PROJ_TM == 0
    t = bsz * seq
    xf = x.reshape(t, d)
    table = rpe_table.astype(F32).T * LOG2E
    bias_a = _bias_lookup(table[:A_HEADS], _moba_bucket_planes(), 1)
    bias_b = _bias_lookup(table[A_HEADS:], _swa_bucket_planes(), B_GROUP)
    bf = lambda w: w.astype(BF16)
    vec = lambda v: v.reshape(depth, 1, v.shape[-1])
    w_r_bf, w_i_bf, w_br_bf, w_out_bf, w_pg_bf, w_pp_bf = map(
        bf, (w_r, w_i, w_br, w_out, w_pg, w_pp))
    w_in_cur = bf(w_in[:1])
    p3d = p.reshape(depth, t, PLE_DIM)
    for i in range(depth):
        proj, gates = _in_proj(xf, vec(norm_g), w_in_cur, i, 0)
        ya = _moba_branch(proj, bias_a, bsz, seq)
        yb = _swa_branch(proj, bias_b, _swa_sink_rows(sinks[i]), bsz, seq)
        yc = _rglru_branch(proj, conv_w, vec(conv_b), w_r_bf, vec(b_r), w_i_bf, vec(b_i),
                           vec(lam), i, bsz, seq)
        merged = _merge(ya, yb, yc, gates, w_br_bf, i)
        xf, w_in_cur = _out_ple(merged, xf, p3d, w_out_bf, vec(ple_norm_g), w_pg_bf, w_pp_bf,
                                final_norm_g, i, final=(i == depth - 1),
                                w_next=w_in if i + 1 < depth else None)
    return xf.reshape(bsz, seq, d)
```

```python
import functools
import math

import jax
import jax.numpy as jnp
import numpy as np
from jax import lax
from jax.experimental import pallas as pl
from jax.experimental.pallas import tpu as pltpu

F32 = jnp.float32
BF16 = jnp.bfloat16

D_MODEL = 2048
PLE_DIM = 256
BRANCH_WIDTH = 1024
A_HEADS = 8
A_HEAD_DIM = 128
MOBA_BLOCK = 256
MOBA_TOPK = 3
B_Q_HEADS = 16
B_KV_HEADS = 4
B_GROUP = B_Q_HEADS // B_KV_HEADS
B_HEAD_DIM = 64
WINDOW = 128
C_BLOCKS = 8
C_BLOCK_DIM = 128
CONV_WIDTH = 4
LRU_C = 8.0
RPE_BUCKETS = 32
RPE_MAX_DIST = 128
EPS = 1e-6
NEG = -1e30
LOG2E = math.log2(math.e)

COL_QA = 0
COL_KA = COL_QA + BRANCH_WIDTH
COL_VA = COL_KA + BRANCH_WIDTH
COL_GA = COL_VA + BRANCH_WIDTH
COL_QB = COL_GA + BRANCH_WIDTH
COL_KB = COL_QB + BRANCH_WIDTH
COL_VB = COL_KB + B_KV_HEADS * B_HEAD_DIM
COL_GB = COL_VB + B_KV_HEADS * B_HEAD_DIM
COL_XC = COL_GB + BRANCH_WIDTH
COL_GC = COL_XC + BRANCH_WIDTH
COL_MG = COL_GC + BRANCH_WIDTH
IN_WIDTH = COL_MG + 3 * D_MODEL

PROJ_TM = 2048
PROJ_TN = 512
MERGE_TM = 2048
MERGE_TN = 512
MERGE_ROWS = 128
PLE_TM = 512
LRU_CHUNK = 256

VMEM_LIMIT = 62 * 1024 * 1024


def _params(sem):
    return pltpu.CompilerParams(dimension_semantics=sem, vmem_limit_bytes=VMEM_LIMIT)


def _rms(xf, g):
    return xf * lax.rsqrt(jnp.mean(xf * xf, axis=-1, keepdims=True) + EPS) * g


def _in_proj_kernel(x_ref, g_ref, w_ref, o_ref, mg_ref, h_sc):
    @pl.when(pl.program_id(1) == 0)
    def _():
        h_sc[...] = _rms(x_ref[...], g_ref[...]).astype(BF16)

    acc = jnp.dot(h_sc[...], w_ref[...], preferred_element_type=F32)
    o_ref[...] = acc
    mg_ref[...] = acc.astype(mg_ref.dtype)


def _in_proj(x2d, g, w_bf, layer, w_layer):
    t, d = x2d.shape
    n = w_bf.shape[2]
    assert COL_MG % PROJ_TN == 0
    main_tiles = COL_MG // PROJ_TN
    return pl.pallas_call(
        _in_proj_kernel,
        grid=(t // PROJ_TM, n // PROJ_TN),
        in_specs=[
            pl.BlockSpec((PROJ_TM, d), lambda i, j: (i, 0)),
            pl.BlockSpec((None, 1, d), lambda i, j: (layer, 0, 0)),
            pl.BlockSpec((None, d, PROJ_TN), lambda i, j: (w_layer, 0, j)),
        ],
        out_specs=[
            pl.BlockSpec((PROJ_TM, PROJ_TN), lambda i, j: (i, jnp.minimum(j, main_tiles))),
            pl.BlockSpec((PROJ_TM, PROJ_TN), lambda i, j: (i, jnp.maximum(j - main_tiles, 0))),
        ],
        out_shape=[jax.ShapeDtypeStruct((t, COL_MG + PROJ_TN), F32),
                   jax.ShapeDtypeStruct((t, n - COL_MG), BF16)],
        scratch_shapes=[pltpu.VMEM((PROJ_TM, d), BF16)],
        compiler_params=_params(("parallel", "arbitrary")),
        name="in_proj",
    )(x2d, g, w_bf)


MASKED_BUCKET = RPE_BUCKETS


def _t5_bucket_np(dist):
    n = np.maximum(dist, 0)
    max_exact = RPE_BUCKETS // 2
    ratio = np.log(np.maximum(n, 1).astype(np.float32) / np.float32(max_exact)) / np.float32(
        math.log(RPE_MAX_DIST / max_exact))
    large = max_exact + (ratio * np.float32(RPE_BUCKETS - max_exact)).astype(np.int32)
    large = np.minimum(large, RPE_BUCKETS - 1)
    return np.where(n < max_exact, n, large).astype(np.int32)


def _moba_bucket_planes():
    L = MOBA_BLOCK
    ki = np.arange(L)[:, None]
    qi = np.arange(L)[None, :]
    d_own = qi - ki
    d_far = 2 * L + qi - ki
    assert d_far.min() >= RPE_MAX_DIST
    own = np.where(d_own >= 0, _t5_bucket_np(d_own), MASKED_BUCKET)
    return np.stack([own, _t5_bucket_np(L + qi - ki), _t5_bucket_np(d_far)]).astype(np.int32)


def _swa_bucket_planes():
    W = WINDOW
    kk = np.arange(2 * W)[:, None]
    qi = np.arange(W)[None, :]
    dist = qi - kk + W
    ok = (dist >= 0) & (dist < W)
    b = _t5_bucket_np(dist)
    return np.stack([np.where(ok, b, MASKED_BUCKET),
                     np.where(ok & (kk >= W), b, MASKED_BUCKET)]).astype(np.int32)


def _bias_lookup_kernel(tab_ref, idx_ref, o_ref):
    h = pl.program_id(0)
    for p in range(idx_ref.shape[0]):
        idx = idx_ref[p]
        acc = jnp.full(idx.shape, NEG, F32)
        for b in range(RPE_BUCKETS):
            acc = jnp.where(idx == b, tab_ref[h * RPE_BUCKETS + b], acc)
        o_ref[0, p] = acc


def _bias_lookup(table_hb, planes, heads_per_row):
    nh = table_hb.shape[0]
    npl, r, c = planes.shape
    return pl.pallas_call(
        _bias_lookup_kernel,
        grid=(nh,),
        in_specs=[
            pl.BlockSpec(memory_space=pltpu.SMEM),
            pl.BlockSpec((npl, r, c), lambda h: (0, 0, 0)),
        ],
        out_specs=pl.BlockSpec((1, npl, r, c),
                               lambda h: (h // heads_per_row, 0, 0, h % heads_per_row)),
        out_shape=jax.ShapeDtypeStruct((nh // heads_per_row, npl, r, c * heads_per_row), F32),
        compiler_params=_params(("parallel",)),
        name="bias_lookup",
    )(table_hb.reshape(-1), jnp.asarray(planes))


GATE_ROWS = 16


_sigmoid = jax.nn.sigmoid


def _silu(x):
    return x * _sigmoid(x)


MOBA_GROUP = 2
MOBA_UNROLL = 2
MOBA_SLOTS = 2
MOBA_HEADS = 2


def _moba_prep_kernel(q_ref, k_ref, v_ref, kb_ref, vt_ref, qt_ref, mask_ref, kmean_sc, *, nb):
    L = MOBA_BLOCK
    dh = A_HEAD_DIM
    G = MOBA_GROUP

    kmean_sc[...] = jnp.zeros_like(kmean_sc)
    for n in range(nb):
        rows = slice((n % G) * L, (n % G + 1) * L)
        kn = k_ref[n * L:(n + 1) * L, :]
        kb_ref[0, n // G, rows, :] = kn.astype(BF16)
        kmean_sc[n:n + 1, :] = jnp.sum(kn, axis=0, keepdims=True) * (1.0 / L)
        vt_ref[0, n // G, :, rows] = v_ref[n * L:(n + 1) * L, :].T.astype(BF16)

    blk = lax.broadcasted_iota(jnp.int32, (GATE_ROWS, L), 0)
    blk_f = blk.astype(F32)
    for j in range(nb):
        qt = q_ref[j * L:(j + 1) * L, :].T
        gate = jnp.dot(kmean_sc[...], qt, precision=lax.Precision.HIGHEST,
                       preferred_element_type=F32)
        gm = jnp.where(blk < j, gate, NEG)
        picked = jnp.zeros((GATE_ROWS, L), F32)
        for _ in range(MOBA_TOPK):
            mx = jnp.max(gm, axis=0, keepdims=True)
            first = jnp.min(jnp.where(gm == mx, blk_f, float(GATE_ROWS)), axis=0, keepdims=True)
            hit = blk_f == first
            picked = jnp.where(hit, 1.0, picked)
            gm = jnp.where(hit, -jnp.inf, gm)
        allowed = ((picked > 0.0) & (blk < j)) | (blk == j)
        qt_ref[0, j] = (qt * (dh ** -0.5 * LOG2E)).astype(BF16)
        mask_ref[0, j] = jnp.where(allowed, 0.0, NEG).astype(BF16)


def _moba_prep(proj, bsz, seq):
    L = MOBA_BLOCK
    dh = A_HEAD_DIM
    G = MOBA_GROUP
    nb = seq // L
    nh = bsz * A_HEADS
    cq, ck, cv = (c // dh for c in (COL_QA, COL_KA, COL_VA))
    col = lambda c: pl.BlockSpec((seq, dh), lambda b, h: (b, c + h))
    shapes = [(nh, nb // G, G * L, dh), (nh, nb // G, dh, G * L), (nh, nb, dh, L),
              (nh, nb, GATE_ROWS, L)]
    return pl.pallas_call(
        functools.partial(_moba_prep_kernel, nb=nb),
        grid=(bsz, A_HEADS),
        in_specs=[col(cq), col(ck), col(cv)],
        out_specs=[pl.BlockSpec((1,) + s[1:], lambda b, h: (b * A_HEADS + h, 0, 0, 0))
                   for s in shapes],
        out_shape=[jax.ShapeDtypeStruct(s, BF16) for s in shapes],
        scratch_shapes=[pltpu.VMEM((GATE_ROWS, dh), F32)],
        compiler_params=_params(("parallel", "parallel")),
        name="moba_prep",
    )(proj, proj, proj)


def _moba_schedule(nb):
    G = MOBA_GROUP
    return [(j, i) for j in range(nb) for i in range((j + G) // G)]


def _moba_kernel(jt_ref, it_ref, kb_ref, vt_ref, qt_ref, mask_ref, g_ref, bias_ref, o_ref,
                 kaug_sc, qaug_sc, s_sc, p_sc, alpha_sc, lfin_sc, lout_sc, m_sc, l_sc, acc_sc, *,
                 sched):
    L = MOBA_BLOCK
    dh = A_HEAD_DIM
    G = MOBA_GROUP
    U = MOBA_UNROLL
    NS = MOBA_SLOTS
    nt = len(sched)
    heads = range(MOBA_HEADS)
    hcol = lambda u: slice(u * dh, (u + 1) * dh)
    last_group = lambda j: j // G

    @pl.when((pl.program_id(0) == 0) & (pl.program_id(1) == 0))
    def _():
        lanes = lax.broadcasted_iota(jnp.int32, (L, dh), 1)
        for u in heads:
            for n in range(kaug_sc.shape[1] * G):
                rows = slice((n % G) * L, (n % G + 1) * L)
                kaug_sc[u, n // G, rows, dh:] = jnp.where(lanes == n, 1.0, 0.0).astype(BF16)
            qaug_sc[u, :, dh + GATE_ROWS:, :] = jnp.zeros(
                (qaug_sc.shape[1], dh - GATE_ROWS, L), BF16)

    for u in heads:
        kaug_sc[u, :, :, :dh] = kb_ref[u]
        qaug_sc[u, :, :dh, :] = qt_ref[u]
        qaug_sc[u, :, dh:dh + GATE_ROWS, :] = mask_ref[u]

    def scores(u, t, slot):
        s_sc[u, slot] = jnp.dot(kaug_sc[u, it_ref[t]], qaug_sc[u, jt_ref[t]],
                                preferred_element_type=F32)

    def softmax(u, t, slot):
        j = jt_ref[t]
        i = it_ref[t]
        s = s_sc[u, slot]
        parts = []
        for g in range(G):
            n = i * G + g
            plane = jnp.where(n == j, 0, jnp.where(n == j - 1, 1, 2))
            parts.append(s[g * L:(g + 1) * L, :] + bias_ref[u, plane])
        m_old = jnp.where(i == 0, NEG, m_sc[u])
        m_new = m_old
        for sp in parts:
            m_new = jnp.maximum(m_new, jnp.max(sp, axis=0, keepdims=True))
        alpha = jnp.exp2(m_old - m_new)
        probs = [jnp.exp2(sp - m_new) for sp in parts]
        l_new = alpha * l_sc[u]
        for pr in probs:
            l_new = l_new + jnp.sum(pr, axis=0, keepdims=True)
        p_sc[u, slot] = jnp.concatenate(probs, axis=0).astype(BF16)
        alpha_sc[u, slot] = alpha
        lfin_sc[u, slot] = l_new
        l_sc[u] = l_new
        m_sc[u] = m_new

    def weighted_values(u, t, slot, r):
        acc_sc[u, r] = alpha_sc[u, slot] * acc_sc[u, (r - 1) % U] + jnp.dot(
            vt_ref[u, it_ref[t]], p_sc[u, slot], preferred_element_type=F32)
        lout_sc[u, r] = lfin_sc[u, slot]

    def finish(u, j, r):
        out_t = acc_sc[u, r] / lout_sc[u, r]
        rows = pl.ds(pl.multiple_of(j * L, L), L)
        o_ref[rows, hcol(u)] = (out_t.T * _silu(g_ref[rows, hcol(u)])).astype(o_ref.dtype)

    s_sc[:, NS - 1] = jnp.full(s_sc.shape[:1] + s_sc.shape[2:], -jnp.inf, F32)
    p_sc[:, NS - 2] = jnp.zeros(p_sc.shape[:1] + p_sc.shape[2:], BF16)
    alpha_sc[...] = jnp.ones_like(alpha_sc)
    lfin_sc[...] = jnp.ones_like(lfin_sc)
    m_sc[...] = jnp.full_like(m_sc, NEG)
    l_sc[...] = jnp.zeros_like(l_sc)
    acc_sc[...] = jnp.zeros_like(acc_sc)

    def unrolled_steps(k, carry):
        base = U * k
        done = []
        for r in range(U):
            t = base + r
            t_pv = jnp.maximum(t - 2, 0)
            t_sm = jnp.maximum(t - 1, 0)
            for u in heads:
                weighted_values(u, t_pv, (r - 2) % NS, r)
                softmax(u, t_sm, (r - 1) % NS)
                scores(u, t, r % NS)
            done.append((t - 2, t_pv, r))
        for t_real, t_pv, r in done:
            j_done = jt_ref[t_pv]

            @pl.when((t_real >= 0) & (it_ref[t_pv] == last_group(j_done)))
            def _():
                for u in heads:
                    finish(u, j_done, r)

        return carry

    assert U % NS == 0
    n_loop = nt // U
    lax.fori_loop(0, n_loop, unrolled_steps, 0)
    for t in range(n_loop * U, nt + 2):
        r = t % U
        for u in heads:
            if 0 <= t - 2:
                weighted_values(u, t - 2, (r - 2) % NS, r)
                j, i = sched[t - 2]
                if i == last_group(j):
                    finish(u, j, r)
            if 0 <= t - 1 < nt:
                softmax(u, t - 1, (r - 1) % NS)
            if t < nt:
                scores(u, t, r % NS)


def _moba_branch(proj, bias, bsz, seq):
    L = MOBA_BLOCK
    dh = A_HEAD_DIM
    G = MOBA_GROUP
    nb = seq // L
    hp = MOBA_HEADS
    assert min(MOBA_TOPK, nb - 1) == MOBA_TOPK and nb <= GATE_ROWS and nb % G == 0
    assert A_HEADS % hp == 0
    kb, vt, qt, mask = _moba_prep(proj, bsz, seq)
    sched = _moba_schedule(nb)
    jt = jnp.asarray(np.array([j for j, _ in sched], np.int32))
    it = jnp.asarray(np.array([i for _, i in sched], np.int32))
    pairs = A_HEADS // hp
    cg = COL_GA // (hp * dh)
    smem = pl.BlockSpec(memory_space=pltpu.SMEM)
    per_pair = lambda a: pl.BlockSpec((hp,) + a.shape[1:], lambda b, h: (b * pairs + h, 0, 0, 0))
    return pl.pallas_call(
        functools.partial(_moba_kernel, sched=sched),
        grid=(bsz, pairs),
        in_specs=[smem, smem, per_pair(kb), per_pair(vt), per_pair(qt), per_pair(mask),
                  pl.BlockSpec((seq, hp * dh), lambda b, h: (b, cg + h)),
                  pl.BlockSpec((hp, 3, L, L), lambda b, h: (h, 0, 0, 0))],
        out_specs=pl.BlockSpec((seq, hp * dh), lambda b, h: (b, h)),
        out_shape=jax.ShapeDtypeStruct((bsz * seq, BRANCH_WIDTH), BF16),
        scratch_shapes=[
            pltpu.VMEM((hp, nb // G, G * L, 2 * dh), BF16),
            pltpu.VMEM((hp, nb, 2 * dh, L), BF16),
            pltpu.VMEM((hp, MOBA_SLOTS, G * L, L), F32),
            pltpu.VMEM((hp, MOBA_SLOTS, G * L, L), BF16),
            pltpu.VMEM((hp, MOBA_SLOTS, 1, L), F32),
            pltpu.VMEM((hp, MOBA_SLOTS, 1, L), F32),
            pltpu.VMEM((hp, MOBA_UNROLL, 1, L), F32),
            pltpu.VMEM((hp, 1, L), F32),
            pltpu.VMEM((hp, 1, L), F32),
            pltpu.VMEM((hp, MOBA_UNROLL, dh, L), F32),
        ],
        compiler_params=_params(("arbitrary", "arbitrary")),
        name="moba",
    )(jt, it, kb, vt, qt, mask, proj, bias)


SWA_BLOCKS = 4


def _swa_kernel(q_ref, kp_ref, kc_ref, vp_ref, vc_ref, g0_ref, g1_ref, bias_ref, sink_ref, o_ref,
                ot_sc, x_sc, k_sc, vt_sc, s_sc, p_sc, d_sc):
    W = WINDOW
    dh = B_HEAD_DIM
    kvw = B_KV_HEADS * dh
    head = lambda g, h: slice((g * B_GROUP + h) * dh, (g * B_GROUP + h + 1) * dh)
    lane_group = lax.broadcasted_iota(jnp.int32, (2 * W, kvw), 1) // dh
    first = pl.program_id(1) == 0
    for u in range(SWA_BLOCKS):
        rows = slice(u * W, (u + 1) * W)
        if u == 0:
            k_prev, v_prev = kp_ref[...], vp_ref[...]
            plane = jnp.where(first, 1, 0)
        else:
            k_prev, v_prev = kc_ref[(u - 1) * W:u * W, :], vc_ref[(u - 1) * W:u * W, :]
            plane = 0
        qt = q_ref[rows, :].T * (dh ** -0.5 * LOG2E)
        x_sc[u] = jnp.concatenate(
            [jnp.concatenate([qt[head(g, h), :] for g in range(B_KV_HEADS)], axis=0)
             for h in range(B_GROUP)], axis=1).astype(BF16)
        kband = jnp.concatenate([k_prev, kc_ref[rows, :]], axis=0)
        vt_sc[u] = jnp.concatenate([v_prev, vc_ref[rows, :]], axis=0).T.astype(BF16)
        for g in range(B_KV_HEADS):
            k_sc[u, g] = jnp.where(lane_group == g, kband, 0.0).astype(BF16)
        for g in range(B_KV_HEADS):
            s_sc[u, g] = jnp.dot(k_sc[u, g], x_sc[u], preferred_element_type=F32)
        for g in range(B_KV_HEADS):
            s = s_sc[u, g] + bias_ref[g, plane]
            sink = sink_ref[g]
            m = jnp.maximum(jnp.max(s, axis=0, keepdims=True), sink)
            p = jnp.exp2(s - m)
            d_sc[u, g] = jnp.sum(p, axis=0, keepdims=True) + jnp.exp2(sink - m)
            p_sc[u, g] = p.astype(BF16)
        for g in range(B_KV_HEADS):
            o = jnp.dot(vt_sc[u, g * dh:(g + 1) * dh, :], p_sc[u, g],
                        preferred_element_type=F32) / d_sc[u, g]
            for h in range(B_GROUP):
                ot_sc[u, head(g, h), :] = o[:, h * W:(h + 1) * W]
        gate = jnp.concatenate([g0_ref[rows, :], g1_ref[rows, :]], axis=1)
        o_ref[rows, :] = (ot_sc[u].T * _silu(gate)).astype(o_ref.dtype)


def _swa_branch(proj, bias, sink, bsz, seq):
    W = WINDOW
    R = SWA_BLOCKS * W
    nb = seq // W
    ns = seq // R
    kvw = B_KV_HEADS * B_HEAD_DIM
    cq = COL_QB // BRANCH_WIDTH
    ck = COL_KB // kvw
    cv = COL_VB // kvw
    cg = COL_GB // (BRANCH_WIDTH // 2)
    assert COL_QB % BRANCH_WIDTH == 0 and COL_KB % kvw == 0 and COL_VB % kvw == 0
    assert COL_GB % (BRANCH_WIDTH // 2) == 0 and seq % R == 0
    prev = lambda b, n: (b * nb + jnp.maximum(SWA_BLOCKS * n - 1, 0))
    cur = lambda c, w: pl.BlockSpec((R, w), lambda b, n: (b * ns + n, c))
    return pl.pallas_call(
        _swa_kernel,
        grid=(bsz, ns),
        in_specs=[
            cur(cq, BRANCH_WIDTH),
            pl.BlockSpec((W, kvw), lambda b, n: (prev(b, n), ck)),
            cur(ck, kvw),
            pl.BlockSpec((W, kvw), lambda b, n: (prev(b, n), cv)),
            cur(cv, kvw),
            cur(cg, BRANCH_WIDTH // 2),
            cur(cg + 1, BRANCH_WIDTH // 2),
            pl.BlockSpec(bias.shape, lambda b, n: (0, 0, 0, 0)),
            pl.BlockSpec(sink.shape, lambda b, n: (0, 0, 0)),
        ],
        out_specs=cur(0, BRANCH_WIDTH),
        out_shape=jax.ShapeDtypeStruct((bsz * seq, BRANCH_WIDTH), BF16),
        scratch_shapes=[
            pltpu.VMEM((SWA_BLOCKS, BRANCH_WIDTH, W), F32),
            pltpu.VMEM((SWA_BLOCKS, kvw, B_GROUP * W), BF16),
            pltpu.VMEM((SWA_BLOCKS, B_KV_HEADS, 2 * W, kvw), BF16),
            pltpu.VMEM((SWA_BLOCKS, kvw, 2 * W), BF16),
            pltpu.VMEM((SWA_BLOCKS, B_KV_HEADS, 2 * W, B_GROUP * W), F32),
            pltpu.VMEM((SWA_BLOCKS, B_KV_HEADS, 2 * W, B_GROUP * W), BF16),
            pltpu.VMEM((SWA_BLOCKS, B_KV_HEADS, 1, B_GROUP * W), F32),
        ],
        compiler_params=_params(("parallel", "arbitrary")),
        name="swa",
    )(proj, proj, proj, proj, proj, proj, proj, bias, sink)


def _swa_sink_rows(sinks):
    s = (sinks.astype(F32) * LOG2E).reshape(B_KV_HEADS, 1, B_GROUP, 1)
    return jnp.broadcast_to(s, (B_KV_HEADS, 1, B_GROUP, WINDOW)).reshape(
        B_KV_HEADS, 1, B_GROUP * WINDOW)


SUBLANES = 8


def _scan_steps(a, b, index, length, axis):
    k = 1
    while k < length:
        keep = index >= k
        a_prev = jnp.where(keep, pltpu.roll(a, k, axis), 1.0)
        b_prev = jnp.where(keep, pltpu.roll(b, k, axis), 0.0)
        b = a * b_prev + b
        a = a * a_prev
        k *= 2
    return a, b


LRU_BLOCKS = 2


def _rglru_kernel(xc_ref, gc_ref, cw_ref, cb_ref, wr_ref, br_ref, wi_ref, bi_ref, lam_ref, o_ref,
                  xpad_sc, h_sc, a_sc, b_sc, hin_sc, *, seq):
    T = LRU_CHUNK
    C = C_BLOCK_DIM
    PAD = 8
    blocks = range(LRU_BLOCKS)
    ch = lambda u: slice(u * C, (u + 1) * C)
    for u in blocks:
        xpad_sc[u, :PAD, :] = jnp.zeros((PAD, C), F32)
        xpad_sc[u, PAD:, :] = xc_ref[:, ch(u)]
    h_sc[...] = jnp.zeros_like(h_sc)
    nlam = -lam_ref[...]
    softplus = jnp.maximum(nlam, 0.0) + jnp.log1p(jnp.exp(-jnp.abs(nlam)))
    decay = -LRU_C * softplus
    tiles = T // SUBLANES
    sub = lax.broadcasted_iota(jnp.int32, (tiles, SUBLANES, C), 1)
    tile_row = lax.broadcasted_iota(jnp.int32, (tiles, C), 0)

    def chunk(c, carry):
        t0 = pl.multiple_of(c * T, T)
        for u in blocks:
            conv = cb_ref[:, ch(u)]
            for w in range(CONV_WIDTH):
                off = PAD - (CONV_WIDTH - 1) + w
                conv = conv + cw_ref[w:w + 1, ch(u)] * xpad_sc[u, pl.ds(t0 + off, T), :]
            cbf = conv.astype(BF16)
            r = _sigmoid(jnp.dot(cbf, wr_ref[u], preferred_element_type=F32) + br_ref[:, ch(u)])
            i = _sigmoid(jnp.dot(cbf, wi_ref[u], preferred_element_type=F32) + bi_ref[:, ch(u)])
            log_a = r * decay[:, ch(u)]
            a = jnp.exp(log_a)
            u2 = -jnp.tanh(log_a) * (a * a + 1.0)
            b = jnp.where(u2 > 0.0, u2 * lax.rsqrt(u2), 0.0) * (i * conv)
            a, b = _scan_steps(a.reshape(tiles, SUBLANES, C), b.reshape(tiles, SUBLANES, C),
                               sub, SUBLANES, 1)
            a_sc[u] = a.reshape(T, C)
            b_sc[u] = b.reshape(T, C)
            last = pl.ds(SUBLANES - 1, tiles, stride=SUBLANES)
            a_tile, b_tile = _scan_steps(a_sc[u, last, :], b_sc[u, last, :], tile_row, tiles, 0)
            h_out = a_tile * h_sc[u] + b_tile
            hin_sc[u] = jnp.where(tile_row >= 1, pltpu.roll(h_out, 1, 0), h_sc[u])
            h_sc[u] = h_out[tiles - 1:tiles, :]
            h_in = jnp.concatenate(
                [jnp.broadcast_to(hin_sc[u, t:t + 1, :], (SUBLANES, C)) for t in range(tiles)],
                axis=0)
            h = a_sc[u] * h_in + b_sc[u]
            o_ref[pl.ds(t0, T), ch(u)] = (h * _silu(gc_ref[pl.ds(t0, T), ch(u)])).astype(
                o_ref.dtype)
        return carry

    lax.fori_loop(0, seq // T, chunk, 0)


def _rglru_branch(proj, conv_w, conv_b, w_r_bf, b_r, w_i_bf, b_i, lam, layer, bsz, seq):
    nbk = LRU_BLOCKS
    cd = C_BLOCK_DIM
    wd = nbk * cd
    assert C_BLOCKS % nbk == 0 and COL_XC % wd == 0 and COL_GC % wd == 0
    cx = COL_XC // wd
    cg = COL_GC // wd
    vspec = pl.BlockSpec((None, 1, wd), lambda b, c: (layer, 0, c))
    wspec = pl.BlockSpec((None, nbk, cd, cd), lambda b, c: (layer, c, 0, 0))
    return pl.pallas_call(
        functools.partial(_rglru_kernel, seq=seq),
        grid=(bsz, C_BLOCKS // nbk),
        in_specs=[
            pl.BlockSpec((seq, wd), lambda b, c: (b, cx + c)),
            pl.BlockSpec((seq, wd), lambda b, c: (b, cg + c)),
            pl.BlockSpec((None, CONV_WIDTH, wd), lambda b, c: (layer, 0, c)),
            vspec, wspec, vspec, wspec, vspec, vspec,
        ],
        out_specs=pl.BlockSpec((seq, wd), lambda b, c: (b, c)),
        out_shape=jax.ShapeDtypeStruct((bsz * seq, BRANCH_WIDTH), BF16),
        scratch_shapes=[pltpu.VMEM((nbk, seq + 8, cd), F32), pltpu.VMEM((nbk, 1, cd), F32),
                        pltpu.VMEM((nbk, LRU_CHUNK, cd), F32),
                        pltpu.VMEM((nbk, LRU_CHUNK, cd), F32),
                        pltpu.VMEM((nbk, LRU_CHUNK // SUBLANES, cd), F32)],
        compiler_params=_params(("parallel", "parallel")),
        name="rglru",
    )(proj, proj, conv_w, conv_b, w_r_bf, b_r, w_i_bf, b_i, lam)


def _merge_kernel(ya_ref, yb_ref, yc_ref, ma_ref, mb_ref, mc_ref, wbr_ref, o_ref):
    for c in range(o_ref.shape[0] // MERGE_ROWS):
        rows = slice(c * MERGE_ROWS, (c + 1) * MERGE_ROWS)
        merged = None
        for n, (y_ref, m_ref) in enumerate(((ya_ref, ma_ref), (yb_ref, mb_ref), (yc_ref, mc_ref))):
            term = _sigmoid(m_ref[rows, :].astype(F32)) * jnp.dot(
                y_ref[rows, :], wbr_ref[n], preferred_element_type=F32)
            merged = term if merged is None else merged + term
        o_ref[rows, :] = merged.astype(o_ref.dtype)


def _merge(ya, yb, yc, gates, wbr_bf, layer):
    t = ya.shape[0]
    d = wbr_bf.shape[3]
    tm, tn = MERGE_TM, MERGE_TN
    assert d % tn == 0
    cm = 0
    per = d // tn
    yspec = pl.BlockSpec((tm, BRANCH_WIDTH), lambda i, c: (i, 0))
    mspec = lambda n: pl.BlockSpec((tm, tn), lambda i, c: (i, cm + n * per + c))
    return pl.pallas_call(
        _merge_kernel,
        grid=(t // tm, per),
        in_specs=[
            yspec, yspec, yspec, mspec(0), mspec(1), mspec(2),
            pl.BlockSpec((None, 3, BRANCH_WIDTH, tn), lambda i, c: (layer, 0, 0, c)),
        ],
        out_specs=pl.BlockSpec((tm, tn), lambda i, c: (i, c)),
        out_shape=jax.ShapeDtypeStruct((t, d), BF16),
        compiler_params=_params(("parallel", "arbitrary")),
        name="merge",
    )(ya, yb, yc, gates, gates, gates, wbr_bf)


BF16_SUBLANES = 16


def _out_ple_kernel(mg_ref, x_ref, p_ref, wout_ref, g_ref, wpg_ref, wpp_ref, fg_ref, *rest,
                    final, cast):
    if cast:
        wsrc_ref, o_ref, wdst_ref = rest
        wdst_ref[...] = wsrc_ref[...].astype(wdst_ref.dtype)
    else:
        (o_ref,) = rest
    x = x_ref[...] + jnp.dot(mg_ref[...], wout_ref[...], preferred_element_type=F32)
    h = _rms(x, g_ref[...]).astype(BF16)
    gate = _sigmoid(jnp.dot(h, wpg_ref[...], preferred_element_type=F32))
    emb = jnp.dot(p_ref[...].astype(BF16), wpp_ref[...], preferred_element_type=F32)
    y = x + gate * emb
    if final:
        y = _rms(y, fg_ref[...])
    o_ref[...] = y


def _out_ple(merged, x2d, p3d, wout_bf, g, wpg_bf, wpp_bf, final_g, layer, final, w_next=None):
    t, d = x2d.shape
    tm = PLE_TM
    steps = t // tm
    in_extra, out_extra, args_extra = [], [], []
    out_shape = jax.ShapeDtypeStruct((t, d), F32)
    if w_next is not None:
        _, wd, wn = w_next.shape
        assert wd % (steps * BF16_SUBLANES) == 0
        in_extra = [pl.BlockSpec((None, wd // steps, wn), lambda i: (layer + 1, i, 0))]
        out_extra = [pl.BlockSpec((wd // steps, wn), lambda i: (i, 0))]
        args_extra = [w_next]
        out_shape = [out_shape, jax.ShapeDtypeStruct((wd, wn), BF16)]
    wspec = lambda r: pl.BlockSpec((None, r, d), lambda i: (layer, 0, 0),
                                   pipeline_mode=pl.Buffered(1))
    row = lambda w: pl.BlockSpec((tm, w), lambda i: (i, 0))
    out = pl.pallas_call(
        functools.partial(_out_ple_kernel, final=final, cast=w_next is not None),
        grid=(steps,),
        in_specs=[row(d), row(d), pl.BlockSpec((None, tm, PLE_DIM), lambda i: (layer, i, 0)),
                  wspec(d), pl.BlockSpec((None, 1, d), lambda i: (layer, 0, 0)), wspec(d),
                  wspec(PLE_DIM), pl.BlockSpec((1, d), lambda i: (0, 0))] + in_extra,
        out_specs=[row(d)] + out_extra if out_extra else row(d),
        out_shape=out_shape,
        compiler_params=_params(("parallel",)),
        name="out_ple",
    )(merged, x2d, p3d, wout_bf, g, wpg_bf, wpp_bf, final_g.reshape(1, d), *args_extra)
    if w_next is None:
        return out, None
    return out[0], out[1].reshape(1, wd, wn)


def kernel(x, p, rpe_table, norm_g, w_in, sinks, conv_w, conv_b, w_r, b_r, w_i, b_i, lam, w_br,
           w_out, ple_norm_g, w_pg, w_pp, final_norm_g):
    bsz, seq, d = x.shape
    depth = w_in.shape[0]
    assert d == D_MODEL and w_in.shape[2] == IN_WIDTH
    assert seq % MOBA_BLOCK == 0 and seq % LRU_CHUNK == 0 and (bsz * seq) % PROJ_TM == 0
    t = bsz * seq
    xf = x.reshape(t, d)
    table = rpe_table.astype(F32).T * LOG2E
    bias_a = _bias_lookup(table[:A_HEADS], _moba_bucket_planes(), 1)
    bias_b = _bias_lookup(table[A_HEADS:], _swa_bucket_planes(), B_GROUP)
    bf = lambda w: w.astype(BF16)
    vec = lambda v: v.reshape(depth, 1, v.shape[-1])
    w_r_bf, w_i_bf, w_br_bf, w_out_bf, w_pg_bf, w_pp_bf = map(
        bf, (w_r, w_i, w_br, w_out, w_pg, w_pp))
    w_in_cur = bf(w_in[:1])
    p3d = p.reshape(depth, t, PLE_DIM)
    for i in range(depth):
        proj, gates = _in_proj(xf, vec(norm_g), w_in_cur, i, 0)
        ya = _moba_branch(proj, bias_a, bsz, seq)
        yb = _swa_branch(proj, bias_b, _swa_sink_rows(sinks[i]), bsz, seq)
        yc = _rglru_branch(proj, conv_w, vec(conv_b), w_r_bf, vec(b_r), w_i_bf, vec(b_i),
                           vec(lam), i, bsz, seq)
        merged = _merge(ya, yb, yc, gates, w_br_bf, i)
        xf, w_in_cur = _out_ple(merged, xf, p3d, w_out_bf, vec(ple_norm_g), w_pg_bf, w_pp_bf,
                                final_norm_g, i, final=(i == depth - 1),
                                w_next=w_in if i + 1 < depth else None)
    return xf.reshape(bsz, seq, d)
```

```python
import functools
import math

import jax
import jax.numpy as jnp
import numpy as np
from jax import lax
from jax.experimental import pallas as pl
from jax.experimental.pallas import tpu as pltpu

F32 = jnp.float32
BF16 = jnp.bfloat16

D_MODEL = 2048
PLE_DIM = 256
BRANCH_WIDTH = 1024
A_HEADS = 8
A_HEAD_DIM = 128
MOBA_BLOCK = 256
MOBA_TOPK = 3
B_Q_HEADS = 16
B_KV_HEADS = 4
B_GROUP = B_Q_HEADS // B_KV_HEADS
B_HEAD_DIM = 64
WINDOW = 128
C_BLOCKS = 8
C_BLOCK_DIM = 128
CONV_WIDTH = 4
LRU_C = 8.0
RPE_BUCKETS = 32
RPE_MAX_DIST = 128
EPS = 1e-6
NEG = -1e30
LOG2E = math.log2(math.e)

COL_QA = 0
COL_KA = COL_QA + BRANCH_WIDTH
COL_VA = COL_KA + BRANCH_WIDTH
COL_GA = COL_VA + BRANCH_WIDTH
COL_QB = COL_GA + BRANCH_WIDTH
COL_KB = COL_QB + BRANCH_WIDTH
COL_VB = COL_KB + B_KV_HEADS * B_HEAD_DIM
COL_GB = COL_VB + B_KV_HEADS * B_HEAD_DIM
COL_XC = COL_GB + BRANCH_WIDTH
COL_GC = COL_XC + BRANCH_WIDTH
COL_MG = COL_GC + BRANCH_WIDTH
IN_WIDTH = COL_MG + 3 * D_MODEL

PROJ_TM = 2048
PROJ_TN = 512
MERGE_TM = 2048
MERGE_TN = 512
MERGE_ROWS = 128
PLE_TM = 512
LRU_CHUNK = 256

VMEM_LIMIT = 62 * 1024 * 1024


def _params(sem):
    return pltpu.CompilerParams(dimension_semantics=sem, vmem_limit_bytes=VMEM_LIMIT)


def _rms(xf, g):
    return xf * lax.rsqrt(jnp.mean(xf * xf, axis=-1, keepdims=True) + EPS) * g


def _in_proj_kernel(x_ref, g_ref, w_ref, o_ref, mg_ref, h_sc):
    @pl.when(pl.program_id(1) == 0)
    def _():
        h_sc[...] = _rms(x_ref[...], g_ref[...]).astype(BF16)

    acc = jnp.dot(h_sc[...], w_ref[...], preferred_element_type=F32)
    o_ref[...] = acc
    mg_ref[...] = acc.astype(mg_ref.dtype)


def _in_proj(x2d, g, w_bf, layer, w_layer):
    t, d = x2d.shape
    n = w_bf.shape[2]
    assert COL_MG % PROJ_TN == 0
    main_tiles = COL_MG // PROJ_TN
    return pl.pallas_call(
        _in_proj_kernel,
        grid=(t // PROJ_TM, n // PROJ_TN),
        in_specs=[
            pl.BlockSpec((PROJ_TM, d), lambda i, j: (i, 0)),
            pl.BlockSpec((None, 1, d), lambda i, j: (layer, 0, 0)),
            pl.BlockSpec((None, d, PROJ_TN), lambda i, j: (w_layer, 0, j)),
        ],
        out_specs=[
            pl.BlockSpec((PROJ_TM, PROJ_TN), lambda i, j: (i, jnp.minimum(j, main_tiles))),
            pl.BlockSpec((PROJ_TM, PROJ_TN), lambda i, j: (i, jnp.maximum(j - main_tiles, 0))),
        ],
        out_shape=[jax.ShapeDtypeStruct((t, COL_MG + PROJ_TN), F32),
                   jax.ShapeDtypeStruct((t, n - COL_MG), BF16)],
        scratch_shapes=[pltpu.VMEM((PROJ_TM, d), BF16)],
        compiler_params=_params(("parallel", "arbitrary")),
        name="in_proj",
    )(x2d, g, w_bf)


MASKED_BUCKET = RPE_BUCKETS


def _t5_bucket_np(dist):
    n = np.maximum(dist, 0)
    max_exact = RPE_BUCKETS // 2
    ratio = np.log(np.maximum(n, 1).astype(np.float32) / np.float32(max_exact)) / np.float32(
        math.log(RPE_MAX_DIST / max_exact))
    large = max_exact + (ratio * np.float32(RPE_BUCKETS - max_exact)).astype(np.int32)
    large = np.minimum(large, RPE_BUCKETS - 1)
    return np.where(n < max_exact, n, large).astype(np.int32)


def _moba_bucket_planes():
    L = MOBA_BLOCK
    ki = np.arange(L)[:, None]
    qi = np.arange(L)[None, :]
    d_own = qi - ki
    d_far = 2 * L + qi - ki
    assert d_far.min() >= RPE_MAX_DIST
    own = np.where(d_own >= 0, _t5_bucket_np(d_own), MASKED_BUCKET)
    return np.stack([own, _t5_bucket_np(L + qi - ki), _t5_bucket_np(d_far)]).astype(np.int32)


def _swa_bucket_planes():
    W = WINDOW
    kk = np.arange(2 * W)[:, None]
    qi = np.arange(W)[None, :]
    dist = qi - kk + W
    ok = (dist >= 0) & (dist < W)
    b = _t5_bucket_np(dist)
    return np.stack([np.where(ok, b, MASKED_BUCKET),
                     np.where(ok & (kk >= W), b, MASKED_BUCKET)]).astype(np.int32)


def _bias_lookup_kernel(tab_ref, idx_ref, o_ref):
    h = pl.program_id(0)
    for p in range(idx_ref.shape[0]):
        idx = idx_ref[p]
        acc = jnp.full(idx.shape, NEG, F32)
        for b in range(RPE_BUCKETS):
            acc = jnp.where(idx == b, tab_ref[h * RPE_BUCKETS + b], acc)
        o_ref[0, p] = acc


def _bias_lookup(table_hb, planes, heads_per_row):
    nh = table_hb.shape[0]
    npl, r, c = planes.shape
    return pl.pallas_call(
        _bias_lookup_kernel,
        grid=(nh,),
        in_specs=[
            pl.BlockSpec(memory_space=pltpu.SMEM),
            pl.BlockSpec((npl, r, c), lambda h: (0, 0, 0)),
        ],
        out_specs=pl.BlockSpec((1, npl, r, c),
                               lambda h: (h // heads_per_row, 0, 0, h % heads_per_row)),
        out_shape=jax.ShapeDtypeStruct((nh // heads_per_row, npl, r, c * heads_per_row), F32),
        compiler_params=_params(("parallel",)),
        name="bias_lookup",
    )(table_hb.reshape(-1), jnp.asarray(planes))


GATE_ROWS = 16


_sigmoid = jax.nn.sigmoid


def _silu(x):
    return x * _sigmoid(x)


MOBA_GROUP = 2
MOBA_UNROLL = 2
MOBA_SLOTS = 2
MOBA_HEADS = 2


def _moba_prep_kernel(q_ref, k_ref, v_ref, kb_ref, vt_ref, qt_ref, mask_ref, kmean_sc, *, nb):
    L = MOBA_BLOCK
    dh = A_HEAD_DIM
    G = MOBA_GROUP

    kmean_sc[...] = jnp.zeros_like(kmean_sc)
    for n in range(nb):
        rows = slice((n % G) * L, (n % G + 1) * L)
        kn = k_ref[n * L:(n + 1) * L, :]
        kb_ref[0, n // G, rows, :] = kn.astype(BF16)
        kmean_sc[n:n + 1, :] = jnp.sum(kn, axis=0, keepdims=True) * (1.0 / L)
        vt_ref[0, n // G, :, rows] = v_ref[n * L:(n + 1) * L, :].T.astype(BF16)

    blk = lax.broadcasted_iota(jnp.int32, (GATE_ROWS, L), 0)
    blk_f = blk.astype(F32)
    for j in range(nb):
        qt = q_ref[j * L:(j + 1) * L, :].T
        gate = jnp.dot(kmean_sc[...], qt, precision=lax.Precision.HIGHEST,
                       preferred_element_type=F32)
        gm = jnp.where(blk < j, gate, NEG)
        picked = jnp.zeros((GATE_ROWS, L), F32)
        for _ in range(MOBA_TOPK):
            mx = jnp.max(gm, axis=0, keepdims=True)
            first = jnp.min(jnp.where(gm == mx, blk_f, float(GATE_ROWS)), axis=0, keepdims=True)
            hit = blk_f == first
            picked = jnp.where(hit, 1.0, picked)
            gm = jnp.where(hit, -jnp.inf, gm)
        allowed = ((picked > 0.0) & (blk < j)) | (blk == j)
        qt_ref[0, j] = (qt * (dh ** -0.5 * LOG2E)).astype(BF16)
        mask_ref[0, j] = jnp.where(allowed, 0.0, NEG).astype(BF16)


def _moba_prep(proj, bsz, seq):
    L = MOBA_BLOCK
    dh = A_HEAD_DIM
    G = MOBA_GROUP
    nb = seq // L
    nh = bsz * A_HEADS
    cq, ck, cv = (c // dh for c in (COL_QA, COL_KA, COL_VA))
    col = lambda c: pl.BlockSpec((seq, dh), lambda b, h: (b, c + h))
    shapes = [(nh, nb // G, G * L, dh), (nh, nb // G, dh, G * L), (nh, nb, dh, L),
              (nh, nb, GATE_ROWS, L)]
    return pl.pallas_call(
        functools.partial(_moba_prep_kernel, nb=nb),
        grid=(bsz, A_HEADS),
        in_specs=[col(cq), col(ck), col(cv)],
        out_specs=[pl.BlockSpec((1,) + s[1:], lambda b, h: (b * A_HEADS + h, 0, 0, 0))
                   for s in shapes],
        out_shape=[jax.ShapeDtypeStruct(s, BF16) for s in shapes],
        scratch_shapes=[pltpu.VMEM((GATE_ROWS, dh), F32)],
        compiler_params=_params(("parallel", "parallel")),
        name="moba_prep",
    )(proj, proj, proj)


def _moba_schedule(nb):
    G = MOBA_GROUP
    return [(j, i) for j in range(nb) for i in range((j + G) // G)]


def _moba_kernel(jt_ref, it_ref, kb_ref, vt_ref, qt_ref, mask_ref, g_ref, bias_ref, o_ref,
                 kaug_sc, qaug_sc, s_sc, p_sc, alpha_sc, lfin_sc, lout_sc, m_sc, l_sc, acc_sc, *,
                 sched):
    L = MOBA_BLOCK
    dh = A_HEAD_DIM
    G = MOBA_GROUP
    U = MOBA_UNROLL
    NS = MOBA_SLOTS
    nt = len(sched)
    heads = range(MOBA_HEADS)
    hcol = lambda u: slice(u * dh, (u + 1) * dh)
    last_group = lambda j: j // G

    @pl.when((pl.program_id(0) == 0) & (pl.program_id(1) == 0))
    def _():
        lanes = lax.broadcasted_iota(jnp.int32, (L, dh), 1)
        for u in heads:
            for n in range(kaug_sc.shape[1] * G):
                rows = slice((n % G) * L, (n % G + 1) * L)
                kaug_sc[u, n // G, rows, dh:] = jnp.where(lanes == n, 1.0, 0.0).astype(BF16)
            qaug_sc[u, :, dh + GATE_ROWS:, :] = jnp.zeros(
                (qaug_sc.shape[1], dh - GATE_ROWS, L), BF16)

    for u in heads:
        kaug_sc[u, :, :, :dh] = kb_ref[u]
        qaug_sc[u, :, :dh, :] = qt_ref[u]
        qaug_sc[u, :, dh:dh + GATE_ROWS, :] = mask_ref[u]

    def scores(u, t, slot):
        s_sc[u, slot] = jnp.dot(kaug_sc[u, it_ref[t]], qaug_sc[u, jt_ref[t]],
                                preferred_element_type=F32)

    def softmax(u, t, slot):
        j = jt_ref[t]
        i = it_ref[t]
        s = s_sc[u, slot]
        parts = []
        for g in range(G):
            n = i * G + g
            plane = jnp.where(n == j, 0, jnp.where(n == j - 1, 1, 2))
            parts.append(s[g * L:(g + 1) * L, :] + bias_ref[u, plane])
        m_old = jnp.where(i == 0, NEG, m_sc[u])
        m_new = m_old
        for sp in parts:
            m_new = jnp.maximum(m_new, jnp.max(sp, axis=0, keepdims=True))
        alpha = jnp.exp2(m_old - m_new)
        probs = [jnp.exp2(sp - m_new) for sp in parts]
        l_new = alpha * l_sc[u]
        for pr in probs:
            l_new = l_new + jnp.sum(pr, axis=0, keepdims=True)
        p_sc[u, slot] = jnp.concatenate(probs, axis=0).astype(BF16)
        alpha_sc[u, slot] = alpha
        lfin_sc[u, slot] = l_new
        l_sc[u] = l_new
        m_sc[u] = m_new

    def weighted_values(u, t, slot, r):
        acc_sc[u, r] = alpha_sc[u, slot] * acc_sc[u, (r - 1) % U] + jnp.dot(
            vt_ref[u, it_ref[t]], p_sc[u, slot], preferred_element_type=F32)
        lout_sc[u, r] = lfin_sc[u, slot]

    def finish(u, j, r):
        out_t = acc_sc[u, r] / lout_sc[u, r]
        rows = pl.ds(pl.multiple_of(j * L, L), L)
        o_ref[rows, hcol(u)] = (out_t.T * _silu(g_ref[rows, hcol(u)])).astype(o_ref.dtype)

    s_sc[:, NS - 1] = jnp.full(s_sc.shape[:1] + s_sc.shape[2:], -jnp.inf, F32)
    p_sc[:, NS - 2] = jnp.zeros(p_sc.shape[:1] + p_sc.shape[2:], BF16)
    alpha_sc[...] = jnp.ones_like(alpha_sc)
    lfin_sc[...] = jnp.ones_like(lfin_sc)
    m_sc[...] = jnp.full_like(m_sc, NEG)
    l_sc[...] = jnp.zeros_like(l_sc)
    acc_sc[...] = jnp.zeros_like(acc_sc)

    def unrolled_steps(k, carry):
        base = U * k
        done = []
        for r in range(U):
            t = base + r
            t_pv = jnp.maximum(t - 2, 0)
            t_sm = jnp.maximum(t - 1, 0)
            for u in heads:
                weighted_values(u, t_pv, (r - 2) % NS, r)
                softmax(u, t_sm, (r - 1) % NS)
                scores(u, t, r % NS)
            done.append((t - 2, t_pv, r))
        for t_real, t_pv, r in done:
            j_done = jt_ref[t_pv]

            @pl.when((t_real >= 0) & (it_ref[t_pv] == last_group(j_done)))
            def _():
                for u in heads:
                    finish(u, j_done, r)

        return carry

    assert U % NS == 0
    n_loop = nt // U
    lax.fori_loop(0, n_loop, unrolled_steps, 0)
    for t in range(n_loop * U, nt + 2):
        r = t % U
        for u in heads:
            if 0 <= t - 2:
                weighted_values(u, t - 2, (r - 2) % NS, r)
                j, i = sched[t - 2]
                if i == last_group(j):
                    finish(u, j, r)
            if 0 <= t - 1 < nt:
                softmax(u, t - 1, (r - 1) % NS)
            if t < nt:
                scores(u, t, r % NS)


def _moba_branch(proj, bias, bsz, seq):
    L = MOBA_BLOCK
    dh = A_HEAD_DIM
    G = MOBA_GROUP
    nb = seq // L
    hp = MOBA_HEADS
    assert min(MOBA_TOPK, nb - 1) == MOBA_TOPK and nb <= GATE_ROWS and nb % G == 0
    assert A_HEADS % hp == 0
    kb, vt, qt, mask = _moba_prep(proj, bsz, seq)
    sched = _moba_schedule(nb)
    jt = jnp.asarray(np.array([j for j, _ in sched], np.int32))
    it = jnp.asarray(np.array([i for _, i in sched], np.int32))
    pairs = A_HEADS // hp
    cg = COL_GA // (hp * dh)
    smem = pl.BlockSpec(memory_space=pltpu.SMEM)
    per_pair = lambda a: pl.BlockSpec((hp,) + a.shape[1:], lambda b, h: (b * pairs + h, 0, 0, 0))
    return pl.pallas_call(
        functools.partial(_moba_kernel, sched=sched),
        grid=(bsz, pairs),
        in_specs=[smem, smem, per_pair(kb), per_pair(vt), per_pair(qt), per_pair(mask),
                  pl.BlockSpec((seq, hp * dh), lambda b, h: (b, cg + h)),
                  pl.BlockSpec((hp, 3, L, L), lambda b, h: (h, 0, 0, 0))],
        out_specs=pl.BlockSpec((seq, hp * dh), lambda b, h: (b, h)),
        out_shape=jax.ShapeDtypeStruct((bsz * seq, BRANCH_WIDTH), BF16),
        scratch_shapes=[
            pltpu.VMEM((hp, nb // G, G * L, 2 * dh), BF16),
            pltpu.VMEM((hp, nb, 2 * dh, L), BF16),
            pltpu.VMEM((hp, MOBA_SLOTS, G * L, L), F32),
            pltpu.VMEM((hp, MOBA_SLOTS, G * L, L), BF16),
            pltpu.VMEM((hp, MOBA_SLOTS, 1, L), F32),
            pltpu.VMEM((hp, MOBA_SLOTS, 1, L), F32),
            pltpu.VMEM((hp, MOBA_UNROLL, 1, L), F32),
            pltpu.VMEM((hp, 1, L), F32),
            pltpu.VMEM((hp, 1, L), F32),
            pltpu.VMEM((hp, MOBA_UNROLL, dh, L), F32),
        ],
        compiler_params=_params(("arbitrary", "arbitrary")),
        name="moba",
    )(jt, it, kb, vt, qt, mask, proj, bias)


SWA_BLOCKS = 4


def _swa_kernel(q_ref, kp_ref, kc_ref, vp_ref, vc_ref, g0_ref, g1_ref, bias_ref, sink_ref, o_ref,
                ot_sc, x_sc, k_sc, vt_sc, s_sc, p_sc, d_sc):
    W = WINDOW
    dh = B_HEAD_DIM
    kvw = B_KV_HEADS * dh
    head = lambda g, h: slice((g * B_GROUP + h) * dh, (g * B_GROUP + h + 1) * dh)
    lane_group = lax.broadcasted_iota(jnp.int32, (2 * W, kvw), 1) // dh
    first = pl.program_id(1) == 0
    for u in range(SWA_BLOCKS):
        rows = slice(u * W, (u + 1) * W)
        if u == 0:
            k_prev, v_prev = kp_ref[...], vp_ref[...]
            plane = jnp.where(first, 1, 0)
        else:
            k_prev, v_prev = kc_ref[(u - 1) * W:u * W, :], vc_ref[(u - 1) * W:u * W, :]
            plane = 0
        qt = q_ref[rows, :].T * (dh ** -0.5 * LOG2E)
        x_sc[u] = jnp.concatenate(
            [jnp.concatenate([qt[head(g, h), :] for g in range(B_KV_HEADS)], axis=0)
             for h in range(B_GROUP)], axis=1).astype(BF16)
        kband = jnp.concatenate([k_prev, kc_ref[rows, :]], axis=0)
        vt_sc[u] = jnp.concatenate([v_prev, vc_ref[rows, :]], axis=0).T.astype(BF16)
        for g in range(B_KV_HEADS):
            k_sc[u, g] = jnp.where(lane_group == g, kband, 0.0).astype(BF16)
        for g in range(B_KV_HEADS):
            s_sc[u, g] = jnp.dot(k_sc[u, g], x_sc[u], preferred_element_type=F32)
        for g in range(B_KV_HEADS):
            s = s_sc[u, g] + bias_ref[g, plane]
            sink = sink_ref[g]
            m = jnp.maximum(jnp.max(s, axis=0, keepdims=True), sink)
            p = jnp.exp2(s - m)
            d_sc[u, g] = jnp.sum(p, axis=0, keepdims=True) + jnp.exp2(sink - m)
            p_sc[u, g] = p.astype(BF16)
        for g in range(B_KV_HEADS):
            o = jnp.dot(vt_sc[u, g * dh:(g + 1) * dh, :], p_sc[u, g],
                        preferred_element_type=F32) / d_sc[u, g]
            for h in range(B_GROUP):
                ot_sc[u, head(g, h), :] = o[:, h * W:(h + 1) * W]
        gate = jnp.concatenate([g0_ref[rows, :], g1_ref[rows, :]], axis=1)
        o_ref[rows, :] = (ot_sc[u].T * _silu(gate)).astype(o_ref.dtype)


def _swa_branch(proj, bias, sink, bsz, seq):
    W = WINDOW
    R = SWA_BLOCKS * W
    nb = seq // W
    ns = seq // R
    kvw = B_KV_HEADS * B_HEAD_DIM
    cq = COL_QB // BRANCH_WIDTH
    ck = COL_KB // kvw
    cv = COL_VB // kvw
    cg = COL_GB // (BRANCH_WIDTH // 2)
    assert COL_QB % BRANCH_WIDTH == 0 and COL_KB % kvw == 0 and COL_VB % kvw == 0
    assert COL_GB % (BRANCH_WIDTH // 2) == 0 and seq % R == 0
    prev = lambda b, n: (b * nb + jnp.maximum(SWA_BLOCKS * n - 1, 0))
    cur = lambda c, w: pl.BlockSpec((R, w), lambda b, n: (b * ns + n, c))
    return pl.pallas_call(
        _swa_kernel,
        grid=(bsz, ns),
        in_specs=[
            cur(cq, BRANCH_WIDTH),
            pl.BlockSpec((W, kvw), lambda b, n: (prev(b, n), ck)),
            cur(ck, kvw),
            pl.BlockSpec((W, kvw), lambda b, n: (prev(b, n), cv)),
            cur(cv, kvw),
            cur(cg, BRANCH_WIDTH // 2),
            cur(cg + 1, BRANCH_WIDTH // 2),
            pl.BlockSpec(bias.shape, lambda b, n: (0, 0, 0, 0)),
            pl.BlockSpec(sink.shape, lambda b, n: (0, 0, 0)),
        ],
        out_specs=cur(0, BRANCH_WIDTH),
        out_shape=jax.ShapeDtypeStruct((bsz * seq, BRANCH_WIDTH), BF16),
        scratch_shapes=[
            pltpu.VMEM((SWA_BLOCKS, BRANCH_WIDTH, W), F32),
            pltpu.VMEM((SWA_BLOCKS, kvw, B_GROUP * W), BF16),
            pltpu.VMEM((SWA_BLOCKS, B_KV_HEADS, 2 * W, kvw), BF16),
            pltpu.VMEM((SWA_BLOCKS, kvw, 2 * W), BF16),
            pltpu.VMEM((SWA_BLOCKS, B_KV_HEADS, 2 * W, B_GROUP * W), F32),
            pltpu.VMEM((SWA_BLOCKS, B_KV_HEADS, 2 * W, B_GROUP * W), BF16),
            pltpu.VMEM((SWA_BLOCKS, B_KV_HEADS, 1, B_GROUP * W), F32),
        ],
        compiler_params=_params(("parallel", "arbitrary")),
        name="swa",
    )(proj, proj, proj, proj, proj, proj, proj, bias, sink)


def _swa_sink_rows(sinks):
    s = (sinks.astype(F32) * LOG2E).reshape(B_KV_HEADS, 1, B_GROUP, 1)
    return jnp.broadcast_to(s, (B_KV_HEADS, 1, B_GROUP, WINDOW)).reshape(
        B_KV_HEADS, 1, B_GROUP * WINDOW)


SUBLANES = 8


def _scan_steps(a, b, index, length, axis):
    k = 1
    while k < length:
        keep = index >= k
        a_prev = jnp.where(keep, pltpu.roll(a, k, axis), 1.0)
        b_prev = jnp.where(keep, pltpu.roll(b, k, axis), 0.0)
        b = a * b_prev + b
        a = a * a_prev
        k *= 2
    return a, b


LRU_BLOCKS = 2


def _rglru_kernel(xc_ref, gc_ref, cw_ref, cb_ref, wr_ref, br_ref, wi_ref, bi_ref, lam_ref, *rest,
                  seq, n_cast):
    cast_src, (o_ref, *cast_dst) = rest[:n_cast], rest[n_cast:2 * n_cast + 1]
    xpad_sc, h_sc, a_sc, b_sc, hin_sc = rest[2 * n_cast + 1:]
    for src, dst in zip(cast_src, cast_dst):
        dst[...] = src[...].astype(dst.dtype)
    T = LRU_CHUNK
    C = C_BLOCK_DIM
    PAD = 8
    blocks = range(LRU_BLOCKS)
    ch = lambda u: slice(u * C, (u + 1) * C)
    for u in blocks:
        xpad_sc[u, :PAD, :] = jnp.zeros((PAD, C), F32)
        xpad_sc[u, PAD:, :] = xc_ref[:, ch(u)]
    h_sc[...] = jnp.zeros_like(h_sc)
    nlam = -lam_ref[...]
    softplus = jnp.maximum(nlam, 0.0) + jnp.log1p(jnp.exp(-jnp.abs(nlam)))
    decay = -LRU_C * softplus
    tiles = T // SUBLANES
    sub = lax.broadcasted_iota(jnp.int32, (tiles, SUBLANES, C), 1)
    tile_row = lax.broadcasted_iota(jnp.int32, (tiles, C), 0)

    def chunk(c, carry):
        t0 = pl.multiple_of(c * T, T)
        for u in blocks:
            conv = cb_ref[:, ch(u)]
            for w in range(CONV_WIDTH):
                off = PAD - (CONV_WIDTH - 1) + w
                conv = conv + cw_ref[w:w + 1, ch(u)] * xpad_sc[u, pl.ds(t0 + off, T), :]
            cbf = conv.astype(BF16)
            r = _sigmoid(jnp.dot(cbf, wr_ref[u], preferred_element_type=F32) + br_ref[:, ch(u)])
            i = _sigmoid(jnp.dot(cbf, wi_ref[u], preferred_element_type=F32) + bi_ref[:, ch(u)])
            log_a = r * decay[:, ch(u)]
            a = jnp.exp(log_a)
            u2 = -jnp.tanh(log_a) * (a * a + 1.0)
            b = jnp.where(u2 > 0.0, u2 * lax.rsqrt(u2), 0.0) * (i * conv)
            a, b = _scan_steps(a.reshape(tiles, SUBLANES, C), b.reshape(tiles, SUBLANES, C),
                               sub, SUBLANES, 1)
            a_sc[u] = a.reshape(T, C)
            b_sc[u] = b.reshape(T, C)
            last = pl.ds(SUBLANES - 1, tiles, stride=SUBLANES)
            a_tile, b_tile = _scan_steps(a_sc[u, last, :], b_sc[u, last, :], tile_row, tiles, 0)
            h_out = a_tile * h_sc[u] + b_tile
            hin_sc[u] = jnp.where(tile_row >= 1, pltpu.roll(h_out, 1, 0), h_sc[u])
            h_sc[u] = h_out[tiles - 1:tiles, :]
            h_in = jnp.concatenate(
                [jnp.broadcast_to(hin_sc[u, t:t + 1, :], (SUBLANES, C)) for t in range(tiles)],
                axis=0)
            h = a_sc[u] * h_in + b_sc[u]
            o_ref[pl.ds(t0, T), ch(u)] = (h * _silu(gc_ref[pl.ds(t0, T), ch(u)])).astype(
                o_ref.dtype)
        return carry

    lax.fori_loop(0, seq // T, chunk, 0)


def _rglru_branch(proj, conv_w, conv_b, w_r_bf, b_r, w_i_bf, b_i, lam, layer, bsz, seq, cast=()):
    nbk = LRU_BLOCKS
    cd = C_BLOCK_DIM
    wd = nbk * cd
    assert C_BLOCKS % nbk == 0 and COL_XC % wd == 0 and COL_GC % wd == 0
    cx = COL_XC // wd
    cg = COL_GC // wd
    vspec = pl.BlockSpec((None, 1, wd), lambda b, c: (layer, 0, c))
    wspec = pl.BlockSpec((None, nbk, cd, cd), lambda b, c: (layer, c, 0, 0))
    ncb = C_BLOCKS // nbk
    steps = bsz * ncb
    flat = [w.reshape(-1, w.shape[-1]) for w in cast]
    assert all(f.shape[0] % (steps * BF16_SUBLANES) == 0 for f in flat)
    slab = lambda f: pl.BlockSpec((f.shape[0] // steps, f.shape[1]), lambda b, c: (b * ncb + c, 0))
    out = pl.pallas_call(
        functools.partial(_rglru_kernel, seq=seq, n_cast=len(cast)),
        grid=(bsz, ncb),
        in_specs=[
            pl.BlockSpec((seq, wd), lambda b, c: (b, cx + c)),
            pl.BlockSpec((seq, wd), lambda b, c: (b, cg + c)),
            pl.BlockSpec((None, CONV_WIDTH, wd), lambda b, c: (layer, 0, c)),
            vspec, wspec, vspec, wspec, vspec, vspec,
        ] + [slab(f) for f in flat],
        out_specs=[pl.BlockSpec((seq, wd), lambda b, c: (b, c))] + [slab(f) for f in flat],
        out_shape=[jax.ShapeDtypeStruct((bsz * seq, BRANCH_WIDTH), BF16)]
        + [jax.ShapeDtypeStruct(f.shape, BF16) for f in flat],
        scratch_shapes=[pltpu.VMEM((nbk, seq + 8, cd), F32), pltpu.VMEM((nbk, 1, cd), F32),
                        pltpu.VMEM((nbk, LRU_CHUNK, cd), F32),
                        pltpu.VMEM((nbk, LRU_CHUNK, cd), F32),
                        pltpu.VMEM((nbk, LRU_CHUNK // SUBLANES, cd), F32)],
        compiler_params=_params(("parallel", "parallel")),
        name="rglru",
    )(proj, proj, conv_w, conv_b, w_r_bf, b_r, w_i_bf, b_i, lam, *flat)
    return out[0], [o.reshape(w.shape) for o, w in zip(out[1:], cast)]


def _merge_kernel(ya_ref, yb_ref, yc_ref, ma_ref, mb_ref, mc_ref, wbr_ref, o_ref):
    for c in range(o_ref.shape[0] // MERGE_ROWS):
        rows = slice(c * MERGE_ROWS, (c + 1) * MERGE_ROWS)
        merged = None
        for n, (y_ref, m_ref) in enumerate(((ya_ref, ma_ref), (yb_ref, mb_ref), (yc_ref, mc_ref))):
            term = _sigmoid(m_ref[rows, :].astype(F32)) * jnp.dot(
                y_ref[rows, :], wbr_ref[n], preferred_element_type=F32)
            merged = term if merged is None else merged + term
        o_ref[rows, :] = merged.astype(o_ref.dtype)


def _merge(ya, yb, yc, gates, wbr_bf, layer):
    t = ya.shape[0]
    d = wbr_bf.shape[3]
    tm, tn = MERGE_TM, MERGE_TN
    assert d % tn == 0
    cm = 0
    per = d // tn
    yspec = pl.BlockSpec((tm, BRANCH_WIDTH), lambda i, c: (i, 0))
    mspec = lambda n: pl.BlockSpec((tm, tn), lambda i, c: (i, cm + n * per + c))
    return pl.pallas_call(
        _merge_kernel,
        grid=(t // tm, per),
        in_specs=[
            yspec, yspec, yspec, mspec(0), mspec(1), mspec(2),
            pl.BlockSpec((None, 3, BRANCH_WIDTH, tn), lambda i, c: (layer, 0, 0, c)),
        ],
        out_specs=pl.BlockSpec((tm, tn), lambda i, c: (i, c)),
        out_shape=jax.ShapeDtypeStruct((t, d), BF16),
        compiler_params=_params(("parallel", "arbitrary")),
        name="merge",
    )(ya, yb, yc, gates, gates, gates, wbr_bf)


BF16_SUBLANES = 16


def _out_ple_kernel(mg_ref, x_ref, p_ref, wout_ref, g_ref, wpg_ref, wpp_ref, fg_ref, *rest,
                    final, cast):
    if cast:
        wsrc_ref, o_ref, wdst_ref = rest
        wdst_ref[...] = wsrc_ref[...].astype(wdst_ref.dtype)
    else:
        (o_ref,) = rest
    x = x_ref[...] + jnp.dot(mg_ref[...], wout_ref[...], preferred_element_type=F32)
    h = _rms(x, g_ref[...]).astype(BF16)
    gate = _sigmoid(jnp.dot(h, wpg_ref[...], preferred_element_type=F32))
    emb = jnp.dot(p_ref[...].astype(BF16), wpp_ref[...], preferred_element_type=F32)
    y = x + gate * emb
    if final:
        y = _rms(y, fg_ref[...])
    o_ref[...] = y


def _out_ple(merged, x2d, p3d, wout_bf, g, wpg_bf, wpp_bf, final_g, layer, final, w_next=None):
    t, d = x2d.shape
    tm = PLE_TM
    steps = t // tm
    in_extra, out_extra, args_extra = [], [], []
    out_shape = jax.ShapeDtypeStruct((t, d), F32)
    if w_next is not None:
        _, wd, wn = w_next.shape
        assert wd % (steps * BF16_SUBLANES) == 0
        in_extra = [pl.BlockSpec((None, wd // steps, wn), lambda i: (layer + 1, i, 0))]
        out_extra = [pl.BlockSpec((wd // steps, wn), lambda i: (i, 0))]
        args_extra = [w_next]
        out_shape = [out_shape, jax.ShapeDtypeStruct((wd, wn), BF16)]
    wspec = lambda r: pl.BlockSpec((None, r, d), lambda i: (layer, 0, 0),
                                   pipeline_mode=pl.Buffered(1))
    row = lambda w: pl.BlockSpec((tm, w), lambda i: (i, 0))
    out = pl.pallas_call(
        functools.partial(_out_ple_kernel, final=final, cast=w_next is not None),
        grid=(steps,),
        in_specs=[row(d), row(d), pl.BlockSpec((None, tm, PLE_DIM), lambda i: (layer, i, 0)),
                  wspec(d), pl.BlockSpec((None, 1, d), lambda i: (layer, 0, 0)), wspec(d),
                  wspec(PLE_DIM), pl.BlockSpec((1, d), lambda i: (0, 0))] + in_extra,
        out_specs=[row(d)] + out_extra if out_extra else row(d),
        out_shape=out_shape,
        compiler_params=_params(("parallel",)),
        name="out_ple",
    )(merged, x2d, p3d, wout_bf, g, wpg_bf, wpp_bf, final_g.reshape(1, d), *args_extra)
    if w_next is None:
        return out, None
    return out[0], out[1].reshape(1, wd, wn)


def kernel(x, p, rpe_table, norm_g, w_in, sinks, conv_w, conv_b, w_r, b_r, w_i, b_i, lam, w_br,
           w_out, ple_norm_g, w_pg, w_pp, final_norm_g):
    bsz, seq, d = x.shape
    depth = w_in.shape[0]
    assert d == D_MODEL and w_in.shape[2] == IN_WIDTH
    assert seq % MOBA_BLOCK == 0 and seq % LRU_CHUNK == 0 and (bsz * seq) % PROJ_TM == 0
    t = bsz * seq
    xf = x.reshape(t, d)
    table = rpe_table.astype(F32).T * LOG2E
    bias_a = _bias_lookup(table[:A_HEADS], _moba_bucket_planes(), 1)
    bias_b = _bias_lookup(table[A_HEADS:], _swa_bucket_planes(), B_GROUP)
    bf = lambda w: w.astype(BF16)
    vec = lambda v: v.reshape(depth, 1, v.shape[-1])
    w_r_bf, w_i_bf = bf(w_r), bf(w_i)
    later = (w_br, w_out, w_pg, w_pp)
    w_in_cur = bf(w_in[:1])
    p3d = p.reshape(depth, t, PLE_DIM)
    for i in range(depth):
        proj, gates = _in_proj(xf, vec(norm_g), w_in_cur, i, 0)
        ya = _moba_branch(proj, bias_a, bsz, seq)
        yb = _swa_branch(proj, bias_b, _swa_sink_rows(sinks[i]), bsz, seq)
        yc, cast = _rglru_branch(proj, conv_w, vec(conv_b), w_r_bf, vec(b_r), w_i_bf, vec(b_i),
                                 vec(lam), i, bsz, seq, cast=later if i == 0 else ())
        if i == 0:
            w_br_bf, w_out_bf, w_pg_bf, w_pp_bf = cast
        merged = _merge(ya, yb, yc, gates, w_br_bf, i)
        xf, w_in_cur = _out_ple(merged, xf, p3d, w_out_bf, vec(ple_norm_g), w_pg_bf, w_pp_bf,
                                final_norm_g, i, final=(i == depth - 1),
                                w_next=w_in if i + 1 < depth else None)
    return xf.reshape(bsz, seq, d)
```

```python
import functools
import math

import jax
import jax.numpy as jnp
import numpy as np
from jax import lax
from jax.experimental import pallas as pl
from jax.experimental.pallas import tpu as pltpu

F32 = jnp.float32
BF16 = jnp.bfloat16

D_MODEL = 2048
PLE_DIM = 256
BRANCH_WIDTH = 1024
A_HEADS = 8
A_HEAD_DIM = 128
MOBA_BLOCK = 256
MOBA_TOPK = 3
B_Q_HEADS = 16
B_KV_HEADS = 4
B_GROUP = B_Q_HEADS // B_KV_HEADS
B_HEAD_DIM = 64
WINDOW = 128
C_BLOCKS = 8
C_BLOCK_DIM = 128
CONV_WIDTH = 4
LRU_C = 8.0
RPE_BUCKETS = 32
RPE_MAX_DIST = 128
EPS = 1e-6
NEG = -1e30
LOG2E = math.log2(math.e)

SUBLANES = 8
BF16_SUBLANES = 16

COL_QA = 0
COL_KA = COL_QA + BRANCH_WIDTH
COL_VA = COL_KA + BRANCH_WIDTH
COL_GA = COL_VA + BRANCH_WIDTH
COL_QB = COL_GA + BRANCH_WIDTH
COL_KB = COL_QB + BRANCH_WIDTH
COL_VB = COL_KB + B_KV_HEADS * B_HEAD_DIM
COL_GB = COL_VB + B_KV_HEADS * B_HEAD_DIM
COL_XC = COL_GB + BRANCH_WIDTH
COL_GC = COL_XC + BRANCH_WIDTH
COL_MG = COL_GC + BRANCH_WIDTH
IN_WIDTH = COL_MG + 3 * D_MODEL

PROJ_TM = 2048
PROJ_TN = 512
MERGE_TM = 2048
MERGE_TN = 512
MERGE_ROWS = 128
PLE_TM = 512
LRU_CHUNK = 256

VMEM_LIMIT = 62 * 1024 * 1024


def _params(sem):
    return pltpu.CompilerParams(dimension_semantics=sem, vmem_limit_bytes=VMEM_LIMIT)


def _rms(xf, g):
    return xf * lax.rsqrt(jnp.mean(xf * xf, axis=-1, keepdims=True) + EPS) * g


def _in_proj_kernel(x_ref, g_ref, w_ref, o_ref, mg_ref, h_sc):
    @pl.when(pl.program_id(1) == 0)
    def _():
        h_sc[...] = _rms(x_ref[...], g_ref[...]).astype(BF16)

    acc = jnp.dot(h_sc[...], w_ref[...], preferred_element_type=F32)
    o_ref[...] = acc
    mg_ref[...] = acc.astype(mg_ref.dtype)


def _in_proj(x2d, g, w_bf, layer, w_layer):
    t, d = x2d.shape
    n = w_bf.shape[2]
    assert COL_MG % PROJ_TN == 0
    main_tiles = COL_MG // PROJ_TN
    return pl.pallas_call(
        _in_proj_kernel,
        grid=(t // PROJ_TM, n // PROJ_TN),
        in_specs=[
            pl.BlockSpec((PROJ_TM, d), lambda i, j: (i, 0)),
            pl.BlockSpec((None, 1, d), lambda i, j: (layer, 0, 0)),
            pl.BlockSpec((None, d, PROJ_TN), lambda i, j: (w_layer, 0, j)),
        ],
        out_specs=[
            pl.BlockSpec((PROJ_TM, PROJ_TN), lambda i, j: (i, jnp.minimum(j, main_tiles))),
            pl.BlockSpec((PROJ_TM, PROJ_TN), lambda i, j: (i, jnp.maximum(j - main_tiles, 0))),
        ],
        out_shape=[jax.ShapeDtypeStruct((t, COL_MG + PROJ_TN), F32),
                   jax.ShapeDtypeStruct((t, n - COL_MG), BF16)],
        scratch_shapes=[pltpu.VMEM((PROJ_TM, d), BF16)],
        compiler_params=_params(("parallel", "arbitrary")),
        name="in_proj",
    )(x2d, g, w_bf)


MASKED_BUCKET = RPE_BUCKETS


def _t5_bucket_np(dist):
    n = np.maximum(dist, 0)
    max_exact = RPE_BUCKETS // 2
    ratio = np.log(np.maximum(n, 1).astype(np.float32) / np.float32(max_exact)) / np.float32(
        math.log(RPE_MAX_DIST / max_exact))
    large = max_exact + (ratio * np.float32(RPE_BUCKETS - max_exact)).astype(np.int32)
    large = np.minimum(large, RPE_BUCKETS - 1)
    return np.where(n < max_exact, n, large).astype(np.int32)


def _moba_bucket_planes():
    L = MOBA_BLOCK
    ki = np.arange(L)[:, None]
    qi = np.arange(L)[None, :]
    d_own = qi - ki
    d_far = 2 * L + qi - ki
    assert d_far.min() >= RPE_MAX_DIST
    own = np.where(d_own >= 0, _t5_bucket_np(d_own), MASKED_BUCKET)
    return np.stack([own, _t5_bucket_np(L + qi - ki), _t5_bucket_np(d_far)]).astype(np.int32)


def _swa_bucket_planes():
    W = WINDOW
    kk = np.arange(2 * W)[:, None]
    qi = np.arange(W)[None, :]
    dist = qi - kk + W
    ok = (dist >= 0) & (dist < W)
    b = _t5_bucket_np(dist)
    return np.stack([np.where(ok, b, MASKED_BUCKET),
                     np.where(ok & (kk >= W), b, MASKED_BUCKET)]).astype(np.int32)


def _bias_lookup_kernel(tab_ref, idx_ref, o_ref):
    h = pl.program_id(0)
    for p in range(idx_ref.shape[0]):
        idx = idx_ref[p]
        acc = jnp.full(idx.shape, NEG, F32)
        for b in range(RPE_BUCKETS):
            acc = jnp.where(idx == b, tab_ref[h * RPE_BUCKETS + b], acc)
        o_ref[0, p] = acc


def _bias_lookup(table_hb, planes, heads_per_row):
    nh = table_hb.shape[0]
    npl, r, c = planes.shape
    return pl.pallas_call(
        _bias_lookup_kernel,
        grid=(nh,),
        in_specs=[
            pl.BlockSpec(memory_space=pltpu.SMEM),
            pl.BlockSpec((npl, r, c), lambda h: (0, 0, 0)),
        ],
        out_specs=pl.BlockSpec((1, npl, r, c),
                               lambda h: (h // heads_per_row, 0, 0, h % heads_per_row)),
        out_shape=jax.ShapeDtypeStruct((nh // heads_per_row, npl, r, c * heads_per_row), F32),
        compiler_params=_params(("parallel",)),
        name="bias_lookup",
    )(table_hb.reshape(-1), jnp.asarray(planes))


GATE_ROWS = BF16_SUBLANES


_sigmoid = jax.nn.sigmoid


def _silu(x):
    return x * _sigmoid(x)


MOBA_GROUP = 2
MOBA_UNROLL = 2
MOBA_SLOTS = 2
MOBA_HEADS = 2


def _moba_prep_kernel(q_ref, k_ref, v_ref, kb_ref, vt_ref, qt_ref, mask_ref, kmean_sc, *, nb):
    L = MOBA_BLOCK
    dh = A_HEAD_DIM
    G = MOBA_GROUP

    kmean_sc[...] = jnp.zeros_like(kmean_sc)
    for n in range(nb):
        rows = slice((n % G) * L, (n % G + 1) * L)
        kn = k_ref[n * L:(n + 1) * L, :]
        kb_ref[0, n // G, rows, :] = kn.astype(BF16)
        kmean_sc[n:n + 1, :] = jnp.sum(kn, axis=0, keepdims=True) * (1.0 / L)
        vt_ref[0, n // G, :, rows] = v_ref[n * L:(n + 1) * L, :].T.astype(BF16)

    blk = lax.broadcasted_iota(jnp.int32, (GATE_ROWS, L), 0)
    blk_f = blk.astype(F32)
    for j in range(nb):
        qt = q_ref[j * L:(j + 1) * L, :].T
        gate = jnp.dot(kmean_sc[...], qt, precision=lax.Precision.HIGHEST,
                       preferred_element_type=F32)
        gm = jnp.where(blk < j, gate, NEG)
        picked = jnp.zeros((GATE_ROWS, L), F32)
        for _ in range(MOBA_TOPK):
            mx = jnp.max(gm, axis=0, keepdims=True)
            first = jnp.min(jnp.where(gm == mx, blk_f, float(GATE_ROWS)), axis=0, keepdims=True)
            hit = blk_f == first
            picked = jnp.where(hit, 1.0, picked)
            gm = jnp.where(hit, -jnp.inf, gm)
        allowed = ((picked > 0.0) & (blk < j)) | (blk == j)
        qt_ref[0, j] = (qt * (dh ** -0.5 * LOG2E)).astype(BF16)
        mask_ref[0, j] = jnp.where(allowed, 0.0, NEG).astype(BF16)


def _moba_prep(proj, bsz, seq):
    L = MOBA_BLOCK
    dh = A_HEAD_DIM
    G = MOBA_GROUP
    nb = seq // L
    nh = bsz * A_HEADS
    cq, ck, cv = (c // dh for c in (COL_QA, COL_KA, COL_VA))
    col = lambda c: pl.BlockSpec((seq, dh), lambda b, h: (b, c + h))
    shapes = [(nh, nb // G, G * L, dh), (nh, nb // G, dh, G * L), (nh, nb, dh, L),
              (nh, nb, GATE_ROWS, L)]
    return pl.pallas_call(
        functools.partial(_moba_prep_kernel, nb=nb),
        grid=(bsz, A_HEADS),
        in_specs=[col(cq), col(ck), col(cv)],
        out_specs=[pl.BlockSpec((1,) + s[1:], lambda b, h: (b * A_HEADS + h, 0, 0, 0))
                   for s in shapes],
        out_shape=[jax.ShapeDtypeStruct(s, BF16) for s in shapes],
        scratch_shapes=[pltpu.VMEM((GATE_ROWS, dh), F32)],
        compiler_params=_params(("parallel", "parallel")),
        name="moba_prep",
    )(proj, proj, proj)


def _moba_schedule(nb):
    G = MOBA_GROUP
    return [(j, i) for j in range(nb) for i in range((j + G) // G)]


def _moba_kernel(jt_ref, it_ref, kb_ref, vt_ref, qt_ref, mask_ref, g_ref, bias_ref, o_ref,
                 kaug_sc, qaug_sc, s_sc, p_sc, alpha_sc, lfin_sc, lout_sc, m_sc, l_sc, acc_sc, *,
                 sched):
    L = MOBA_BLOCK
    dh = A_HEAD_DIM
    G = MOBA_GROUP
    U = MOBA_UNROLL
    NS = MOBA_SLOTS
    nt = len(sched)
    heads = range(MOBA_HEADS)
    hcol = lambda u: slice(u * dh, (u + 1) * dh)
    last_group = lambda j: j // G

    @pl.when((pl.program_id(0) == 0) & (pl.program_id(1) == 0))
    def _():
        lanes = lax.broadcasted_iota(jnp.int32, (L, dh), 1)
        for u in heads:
            for n in range(kaug_sc.shape[1] * G):
                rows = slice((n % G) * L, (n % G + 1) * L)
                kaug_sc[u, n // G, rows, dh:] = jnp.where(lanes == n, 1.0, 0.0).astype(BF16)
            qaug_sc[u, :, dh + GATE_ROWS:, :] = jnp.zeros(
                (qaug_sc.shape[1], dh - GATE_ROWS, L), BF16)

    for u in heads:
        kaug_sc[u, :, :, :dh] = kb_ref[u]
        qaug_sc[u, :, :dh, :] = qt_ref[u]
        qaug_sc[u, :, dh:dh + GATE_ROWS, :] = mask_ref[u]

    def scores(u, t, slot):
        s_sc[u, slot] = jnp.dot(kaug_sc[u, it_ref[t]], qaug_sc[u, jt_ref[t]],
                                preferred_element_type=F32)

    def softmax(u, t, slot):
        j = jt_ref[t]
        i = it_ref[t]
        s = s_sc[u, slot]
        parts = []
        for g in range(G):
            n = i * G + g
            plane = jnp.where(n == j, 0, jnp.where(n == j - 1, 1, 2))
            parts.append(s[g * L:(g + 1) * L, :] + bias_ref[u, plane])
        m_old = jnp.where(i == 0, NEG, m_sc[u])
        m_new = m_old
        for sp in parts:
            m_new = jnp.maximum(m_new, jnp.max(sp, axis=0, keepdims=True))
        alpha = jnp.exp2(m_old - m_new)
        probs = [jnp.exp2(sp - m_new) for sp in parts]
        l_new = alpha * l_sc[u]
        for pr in probs:
            l_new = l_new + jnp.sum(pr, axis=0, keepdims=True)
        p_sc[u, slot] = jnp.concatenate(probs, axis=0).astype(BF16)
        alpha_sc[u, slot] = alpha
        lfin_sc[u, slot] = l_new
        l_sc[u] = l_new
        m_sc[u] = m_new

    def weighted_values(u, t, slot, r):
        acc_sc[u, r] = alpha_sc[u, slot] * acc_sc[u, (r - 1) % U] + jnp.dot(
            vt_ref[u, it_ref[t]], p_sc[u, slot], preferred_element_type=F32)
        lout_sc[u, r] = lfin_sc[u, slot]

    def finish(u, j, r):
        out_t = acc_sc[u, r] / lout_sc[u, r]
        rows = pl.ds(pl.multiple_of(j * L, L), L)
        o_ref[rows, hcol(u)] = (out_t.T * _silu(g_ref[rows, hcol(u)])).astype(o_ref.dtype)

    s_sc[:, NS - 1] = jnp.full(s_sc.shape[:1] + s_sc.shape[2:], -jnp.inf, F32)
    p_sc[:, NS - 2] = jnp.zeros(p_sc.shape[:1] + p_sc.shape[2:], BF16)
    alpha_sc[...] = jnp.ones_like(alpha_sc)
    lfin_sc[...] = jnp.ones_like(lfin_sc)
    m_sc[...] = jnp.full_like(m_sc, NEG)
    l_sc[...] = jnp.zeros_like(l_sc)
    acc_sc[...] = jnp.zeros_like(acc_sc)

    def unrolled_steps(k, carry):
        base = U * k
        done = []
        for r in range(U):
            t = base + r
            t_pv = jnp.maximum(t - 2, 0)
            t_sm = jnp.maximum(t - 1, 0)
            for u in heads:
                weighted_values(u, t_pv, (r - 2) % NS, r)
                softmax(u, t_sm, (r - 1) % NS)
                scores(u, t, r % NS)
            done.append((t - 2, t_pv, r))
        for t_real, t_pv, r in done:
            j_done = jt_ref[t_pv]

            @pl.when((t_real >= 0) & (it_ref[t_pv] == last_group(j_done)))
            def _():
                for u in heads:
                    finish(u, j_done, r)

        return carry

    assert U % NS == 0
    n_loop = nt // U
    lax.fori_loop(0, n_loop, unrolled_steps, 0)
    for t in range(n_loop * U, nt + 2):
        r = t % U
        for u in heads:
            if 0 <= t - 2:
                weighted_values(u, t - 2, (r - 2) % NS, r)
                j, i = sched[t - 2]
                if i == last_group(j):
                    finish(u, j, r)
            if 0 <= t - 1 < nt:
                softmax(u, t - 1, (r - 1) % NS)
            if t < nt:
                scores(u, t, r % NS)


def _moba_branch(proj, bias, bsz, seq):
    L = MOBA_BLOCK
    dh = A_HEAD_DIM
    G = MOBA_GROUP
    nb = seq // L
    hp = MOBA_HEADS
    assert min(MOBA_TOPK, nb - 1) == MOBA_TOPK and nb <= GATE_ROWS and nb % G == 0
    assert A_HEADS % hp == 0
    kb, vt, qt, mask = _moba_prep(proj, bsz, seq)
    sched = _moba_schedule(nb)
    jt = jnp.asarray(np.array([j for j, _ in sched], np.int32))
    it = jnp.asarray(np.array([i for _, i in sched], np.int32))
    pairs = A_HEADS // hp
    cg = COL_GA // (hp * dh)
    smem = pl.BlockSpec(memory_space=pltpu.SMEM)
    per_pair = lambda a: pl.BlockSpec((hp,) + a.shape[1:], lambda b, h: (b * pairs + h, 0, 0, 0))
    return pl.pallas_call(
        functools.partial(_moba_kernel, sched=sched),
        grid=(bsz, pairs),
        in_specs=[smem, smem, per_pair(kb), per_pair(vt), per_pair(qt), per_pair(mask),
                  pl.BlockSpec((seq, hp * dh), lambda b, h: (b, cg + h)),
                  pl.BlockSpec((hp, 3, L, L), lambda b, h: (h, 0, 0, 0))],
        out_specs=pl.BlockSpec((seq, hp * dh), lambda b, h: (b, h)),
        out_shape=jax.ShapeDtypeStruct((bsz * seq, BRANCH_WIDTH), BF16),
        scratch_shapes=[
            pltpu.VMEM((hp, nb // G, G * L, 2 * dh), BF16),
            pltpu.VMEM((hp, nb, 2 * dh, L), BF16),
            pltpu.VMEM((hp, MOBA_SLOTS, G * L, L), F32),
            pltpu.VMEM((hp, MOBA_SLOTS, G * L, L), BF16),
            pltpu.VMEM((hp, MOBA_SLOTS, 1, L), F32),
            pltpu.VMEM((hp, MOBA_SLOTS, 1, L), F32),
            pltpu.VMEM((hp, MOBA_UNROLL, 1, L), F32),
            pltpu.VMEM((hp, 1, L), F32),
            pltpu.VMEM((hp, 1, L), F32),
            pltpu.VMEM((hp, MOBA_UNROLL, dh, L), F32),
        ],
        compiler_params=_params(("arbitrary", "arbitrary")),
        name="moba",
    )(jt, it, kb, vt, qt, mask, proj, bias)


SWA_BLOCKS = 4


def _swa_kernel(q_ref, kp_ref, kc_ref, vp_ref, vc_ref, g0_ref, g1_ref, bias_ref, sink_ref, o_ref,
                ot_sc, x_sc, k_sc, vt_sc, s_sc, p_sc, d_sc):
    W = WINDOW
    dh = B_HEAD_DIM
    kvw = B_KV_HEADS * dh
    head = lambda g, h: slice((g * B_GROUP + h) * dh, (g * B_GROUP + h + 1) * dh)
    lane_group = lax.broadcasted_iota(jnp.int32, (2 * W, kvw), 1) // dh
    first = pl.program_id(1) == 0
    for u in range(SWA_BLOCKS):
        rows = slice(u * W, (u + 1) * W)
        if u == 0:
            k_prev, v_prev = kp_ref[...], vp_ref[...]
            plane = jnp.where(first, 1, 0)
        else:
            k_prev, v_prev = kc_ref[(u - 1) * W:u * W, :], vc_ref[(u - 1) * W:u * W, :]
            plane = 0
        qt = q_ref[rows, :].T * (dh ** -0.5 * LOG2E)
        x_sc[u] = jnp.concatenate(
            [jnp.concatenate([qt[head(g, h), :] for g in range(B_KV_HEADS)], axis=0)
             for h in range(B_GROUP)], axis=1).astype(BF16)
        kband = jnp.concatenate([k_prev, kc_ref[rows, :]], axis=0)
        vt_sc[u] = jnp.concatenate([v_prev, vc_ref[rows, :]], axis=0).T.astype(BF16)
        for g in range(B_KV_HEADS):
            k_sc[u, g] = jnp.where(lane_group == g, kband, 0.0).astype(BF16)
        for g in range(B_KV_HEADS):
            s_sc[u, g] = jnp.dot(k_sc[u, g], x_sc[u], preferred_element_type=F32)
        for g in range(B_KV_HEADS):
            s = s_sc[u, g] + bias_ref[g, plane]
            sink = sink_ref[g]
            m = jnp.maximum(jnp.max(s, axis=0, keepdims=True), sink)
            p = jnp.exp2(s - m)
            d_sc[u, g] = jnp.sum(p, axis=0, keepdims=True) + jnp.exp2(sink - m)
            p_sc[u, g] = p.astype(BF16)
        for g in range(B_KV_HEADS):
            o = jnp.dot(vt_sc[u, g * dh:(g + 1) * dh, :], p_sc[u, g],
                        preferred_element_type=F32) / d_sc[u, g]
            for h in range(B_GROUP):
                ot_sc[u, head(g, h), :] = o[:, h * W:(h + 1) * W]
        gate = jnp.concatenate([g0_ref[rows, :], g1_ref[rows, :]], axis=1)
        o_ref[rows, :] = (ot_sc[u].T * _silu(gate)).astype(o_ref.dtype)


def _swa_branch(proj, bias, sink, bsz, seq):
    W = WINDOW
    R = SWA_BLOCKS * W
    nb = seq // W
    ns = seq // R
    kvw = B_KV_HEADS * B_HEAD_DIM
    cq = COL_QB // BRANCH_WIDTH
    ck = COL_KB // kvw
    cv = COL_VB // kvw
    cg = COL_GB // (BRANCH_WIDTH // 2)
    assert COL_QB % BRANCH_WIDTH == 0 and COL_KB % kvw == 0 and COL_VB % kvw == 0
    assert COL_GB % (BRANCH_WIDTH // 2) == 0 and seq % R == 0
    prev = lambda b, n: (b * nb + jnp.maximum(SWA_BLOCKS * n - 1, 0))
    cur = lambda c, w: pl.BlockSpec((R, w), lambda b, n: (b * ns + n, c))
    return pl.pallas_call(
        _swa_kernel,
        grid=(bsz, ns),
        in_specs=[
            cur(cq, BRANCH_WIDTH),
            pl.BlockSpec((W, kvw), lambda b, n: (prev(b, n), ck)),
            cur(ck, kvw),
            pl.BlockSpec((W, kvw), lambda b, n: (prev(b, n), cv)),
            cur(cv, kvw),
            cur(cg, BRANCH_WIDTH // 2),
            cur(cg + 1, BRANCH_WIDTH // 2),
            pl.BlockSpec(bias.shape, lambda b, n: (0, 0, 0, 0)),
            pl.BlockSpec(sink.shape, lambda b, n: (0, 0, 0)),
        ],
        out_specs=cur(0, BRANCH_WIDTH),
        out_shape=jax.ShapeDtypeStruct((bsz * seq, BRANCH_WIDTH), BF16),
        scratch_shapes=[
            pltpu.VMEM((SWA_BLOCKS, BRANCH_WIDTH, W), F32),
            pltpu.VMEM((SWA_BLOCKS, kvw, B_GROUP * W), BF16),
            pltpu.VMEM((SWA_BLOCKS, B_KV_HEADS, 2 * W, kvw), BF16),
            pltpu.VMEM((SWA_BLOCKS, kvw, 2 * W), BF16),
            pltpu.VMEM((SWA_BLOCKS, B_KV_HEADS, 2 * W, B_GROUP * W), F32),
            pltpu.VMEM((SWA_BLOCKS, B_KV_HEADS, 2 * W, B_GROUP * W), BF16),
            pltpu.VMEM((SWA_BLOCKS, B_KV_HEADS, 1, B_GROUP * W), F32),
        ],
        compiler_params=_params(("parallel", "arbitrary")),
        name="swa",
    )(proj, proj, proj, proj, proj, proj, proj, bias, sink)


def _swa_sink_rows(sinks):
    s = (sinks.astype(F32) * LOG2E).reshape(B_KV_HEADS, 1, B_GROUP, 1)
    return jnp.broadcast_to(s, (B_KV_HEADS, 1, B_GROUP, WINDOW)).reshape(
        B_KV_HEADS, 1, B_GROUP * WINDOW)


def _scan_steps(a, b, index, length, axis):
    k = 1
    while k < length:
        keep = index >= k
        a_prev = jnp.where(keep, pltpu.roll(a, k, axis), 1.0)
        b_prev = jnp.where(keep, pltpu.roll(b, k, axis), 0.0)
        b = a * b_prev + b
        a = a * a_prev
        k *= 2
    return a, b


LRU_BLOCKS = 2


def _rglru_kernel(xc_ref, gc_ref, cw_ref, cb_ref, wr_ref, br_ref, wi_ref, bi_ref, lam_ref, *rest,
                  seq, n_cast):
    cast_src, (o_ref, *cast_dst) = rest[:n_cast], rest[n_cast:2 * n_cast + 1]
    xpad_sc, h_sc, a_sc, b_sc, hin_sc = rest[2 * n_cast + 1:]
    for src, dst in zip(cast_src, cast_dst):
        dst[...] = src[...].astype(dst.dtype)
    T = LRU_CHUNK
    C = C_BLOCK_DIM
    PAD = SUBLANES
    blocks = range(LRU_BLOCKS)
    ch = lambda u: slice(u * C, (u + 1) * C)
    for u in blocks:
        xpad_sc[u, :PAD, :] = jnp.zeros((PAD, C), F32)
        xpad_sc[u, PAD:, :] = xc_ref[:, ch(u)]
    h_sc[...] = jnp.zeros_like(h_sc)
    nlam = -lam_ref[...]
    softplus = jnp.maximum(nlam, 0.0) + jnp.log1p(jnp.exp(-jnp.abs(nlam)))
    decay = -LRU_C * softplus
    tiles = T // SUBLANES
    sub = lax.broadcasted_iota(jnp.int32, (tiles, SUBLANES, C), 1)
    tile_row = lax.broadcasted_iota(jnp.int32, (tiles, C), 0)

    def chunk(c, carry):
        t0 = pl.multiple_of(c * T, T)
        for u in blocks:
            conv = cb_ref[:, ch(u)]
            for w in range(CONV_WIDTH):
                off = PAD - (CONV_WIDTH - 1) + w
                conv = conv + cw_ref[w:w + 1, ch(u)] * xpad_sc[u, pl.ds(t0 + off, T), :]
            cbf = conv.astype(BF16)
            r = _sigmoid(jnp.dot(cbf, wr_ref[u], preferred_element_type=F32) + br_ref[:, ch(u)])
            i = _sigmoid(jnp.dot(cbf, wi_ref[u], preferred_element_type=F32) + bi_ref[:, ch(u)])
            log_a = r * decay[:, ch(u)]
            a = jnp.exp(log_a)
            u2 = -jnp.tanh(log_a) * (a * a + 1.0)
            b = jnp.where(u2 > 0.0, u2 * lax.rsqrt(u2), 0.0) * (i * conv)
            a, b = _scan_steps(a.reshape(tiles, SUBLANES, C), b.reshape(tiles, SUBLANES, C),
                               sub, SUBLANES, 1)
            a_sc[u] = a.reshape(T, C)
            b_sc[u] = b.reshape(T, C)
            last = pl.ds(SUBLANES - 1, tiles, stride=SUBLANES)
            a_tile, b_tile = _scan_steps(a_sc[u, last, :], b_sc[u, last, :], tile_row, tiles, 0)
            h_out = a_tile * h_sc[u] + b_tile
            hin_sc[u] = jnp.where(tile_row >= 1, pltpu.roll(h_out, 1, 0), h_sc[u])
            h_sc[u] = h_out[tiles - 1:tiles, :]
            h_in = jnp.concatenate(
                [jnp.broadcast_to(hin_sc[u, t:t + 1, :], (SUBLANES, C)) for t in range(tiles)],
                axis=0)
            h = a_sc[u] * h_in + b_sc[u]
            o_ref[pl.ds(t0, T), ch(u)] = (h * _silu(gc_ref[pl.ds(t0, T), ch(u)])).astype(
                o_ref.dtype)
        return carry

    lax.fori_loop(0, seq // T, chunk, 0)


def _rglru_branch(proj, conv_w, conv_b, w_r_bf, b_r, w_i_bf, b_i, lam, layer, bsz, seq, cast=()):
    nbk = LRU_BLOCKS
    cd = C_BLOCK_DIM
    wd = nbk * cd
    assert C_BLOCKS % nbk == 0 and COL_XC % wd == 0 and COL_GC % wd == 0
    cx = COL_XC // wd
    cg = COL_GC // wd
    vspec = pl.BlockSpec((None, 1, wd), lambda b, c: (layer, 0, c))
    wspec = pl.BlockSpec((None, nbk, cd, cd), lambda b, c: (layer, c, 0, 0))
    ncb = C_BLOCKS // nbk
    steps = bsz * ncb
    flat = [w.reshape(-1, w.shape[-1]) for w in cast]
    assert all(f.shape[0] % (steps * BF16_SUBLANES) == 0 for f in flat)
    slab = lambda f: pl.BlockSpec((f.shape[0] // steps, f.shape[1]), lambda b, c: (b * ncb + c, 0))
    out = pl.pallas_call(
        functools.partial(_rglru_kernel, seq=seq, n_cast=len(cast)),
        grid=(bsz, ncb),
        in_specs=[
            pl.BlockSpec((seq, wd), lambda b, c: (b, cx + c)),
            pl.BlockSpec((seq, wd), lambda b, c: (b, cg + c)),
            pl.BlockSpec((None, CONV_WIDTH, wd), lambda b, c: (layer, 0, c)),
            vspec, wspec, vspec, wspec, vspec, vspec,
        ] + [slab(f) for f in flat],
        out_specs=[pl.BlockSpec((seq, wd), lambda b, c: (b, c))] + [slab(f) for f in flat],
        out_shape=[jax.ShapeDtypeStruct((bsz * seq, BRANCH_WIDTH), BF16)]
        + [jax.ShapeDtypeStruct(f.shape, BF16) for f in flat],
        scratch_shapes=[pltpu.VMEM((nbk, seq + SUBLANES, cd), F32), pltpu.VMEM((nbk, 1, cd), F32),
                        pltpu.VMEM((nbk, LRU_CHUNK, cd), F32),
                        pltpu.VMEM((nbk, LRU_CHUNK, cd), F32),
                        pltpu.VMEM((nbk, LRU_CHUNK // SUBLANES, cd), F32)],
        compiler_params=_params(("parallel", "parallel")),
        name="rglru",
    )(proj, proj, conv_w, conv_b, w_r_bf, b_r, w_i_bf, b_i, lam, *flat)
    return out[0], [o.reshape(w.shape) for o, w in zip(out[1:], cast)]


def _merge_kernel(ya_ref, yb_ref, yc_ref, ma_ref, mb_ref, mc_ref, wbr_ref, o_ref):
    for c in range(o_ref.shape[0] // MERGE_ROWS):
        rows = slice(c * MERGE_ROWS, (c + 1) * MERGE_ROWS)
        merged = None
        for n, (y_ref, m_ref) in enumerate(((ya_ref, ma_ref), (yb_ref, mb_ref), (yc_ref, mc_ref))):
            term = _sigmoid(m_ref[rows, :].astype(F32)) * jnp.dot(
                y_ref[rows, :], wbr_ref[n], preferred_element_type=F32)
            merged = term if merged is None else merged + term
        o_ref[rows, :] = merged.astype(o_ref.dtype)


def _merge(ya, yb, yc, gates, wbr_bf, layer):
    t = ya.shape[0]
    d = wbr_bf.shape[3]
    tm, tn = MERGE_TM, MERGE_TN
    assert d % tn == 0
    cm = 0
    per = d // tn
    yspec = pl.BlockSpec((tm, BRANCH_WIDTH), lambda i, c: (i, 0))
    mspec = lambda n: pl.BlockSpec((tm, tn), lambda i, c: (i, cm + n * per + c))
    return pl.pallas_call(
        _merge_kernel,
        grid=(t // tm, per),
        in_specs=[
            yspec, yspec, yspec, mspec(0), mspec(1), mspec(2),
            pl.BlockSpec((None, 3, BRANCH_WIDTH, tn), lambda i, c: (layer, 0, 0, c)),
        ],
        out_specs=pl.BlockSpec((tm, tn), lambda i, c: (i, c)),
        out_shape=jax.ShapeDtypeStruct((t, d), BF16),
        compiler_params=_params(("parallel", "arbitrary")),
        name="merge",
    )(ya, yb, yc, gates, gates, gates, wbr_bf)


def _out_ple_kernel(mg_ref, x_ref, p_ref, wout_ref, g_ref, wpg_ref, wpp_ref, fg_ref, *rest,
                    final, cast):
    if cast:
        wsrc_ref, o_ref, wdst_ref = rest
        wdst_ref[...] = wsrc_ref[...].astype(wdst_ref.dtype)
    else:
        (o_ref,) = rest
    x = x_ref[...] + jnp.dot(mg_ref[...], wout_ref[...], preferred_element_type=F32)
    h = _rms(x, g_ref[...]).astype(BF16)
    gate = _sigmoid(jnp.dot(h, wpg_ref[...], preferred_element_type=F32))
    emb = jnp.dot(p_ref[...].astype(BF16), wpp_ref[...], preferred_element_type=F32)
    y = x + gate * emb
    if final:
        y = _rms(y, fg_ref[...])
    o_ref[...] = y


def _out_ple(merged, x2d, p3d, wout_bf, g, wpg_bf, wpp_bf, final_g, layer, final, w_next=None):
    t, d = x2d.shape
    tm = PLE_TM
    steps = t // tm
    in_extra, out_extra, args_extra = [], [], []
    out_shape = jax.ShapeDtypeStruct((t, d), F32)
    if w_next is not None:
        _, wd, wn = w_next.shape
        assert wd % (steps * BF16_SUBLANES) == 0
        in_extra = [pl.BlockSpec((None, wd // steps, wn), lambda i: (layer + 1, i, 0))]
        out_extra = [pl.BlockSpec((wd // steps, wn), lambda i: (i, 0))]
        args_extra = [w_next]
        out_shape = [out_shape, jax.ShapeDtypeStruct((wd, wn), BF16)]
    wspec = lambda r: pl.BlockSpec((None, r, d), lambda i: (layer, 0, 0),
                                   pipeline_mode=pl.Buffered(1))
    row = lambda w: pl.BlockSpec((tm, w), lambda i: (i, 0))
    out = pl.pallas_call(
        functools.partial(_out_ple_kernel, final=final, cast=w_next is not None),
        grid=(steps,),
        in_specs=[row(d), row(d), pl.BlockSpec((None, tm, PLE_DIM), lambda i: (layer, i, 0)),
                  wspec(d), pl.BlockSpec((None, 1, d), lambda i: (layer, 0, 0)), wspec(d),
                  wspec(PLE_DIM), pl.BlockSpec((1, d), lambda i: (0, 0))] + in_extra,
        out_specs=[row(d)] + out_extra if out_extra else row(d),
        out_shape=out_shape,
        compiler_params=_params(("parallel",)),
        name="out_ple",
    )(merged, x2d, p3d, wout_bf, g, wpg_bf, wpp_bf, final_g.reshape(1, d), *args_extra)
    if w_next is None:
        return out, None
    return out[0], out[1].reshape(1, wd, wn)


def kernel(x, p, rpe_table, norm_g, w_in, sinks, conv_w, conv_b, w_r, b_r, w_i, b_i, lam, w_br,
           w_out, ple_norm_g, w_pg, w_pp, final_norm_g):
    bsz, seq, d = x.shape
    depth = w_in.shape[0]
    assert d == D_MODEL and w_in.shape[2] == IN_WIDTH
    assert seq % MOBA_BLOCK == 0 and seq % LRU_CHUNK == 0 and (bsz * seq) % PROJ_TM == 0
    t = bsz * seq
    xf = x.reshape(t, d)
    table = rpe_table.astype(F32).T * LOG2E
    bias_a = _bias_lookup(table[:A_HEADS], _moba_bucket_planes(), 1)
    bias_b = _bias_lookup(table[A_HEADS:], _swa_bucket_planes(), B_GROUP)
    bf = lambda w: w.astype(BF16)
    vec = lambda v: v.reshape(depth, 1, v.shape[-1])
    w_r_bf, w_i_bf = bf(w_r), bf(w_i)
    later = (w_br, w_out, w_pg, w_pp)
    w_in_cur = bf(w_in[:1])
    p3d = p.reshape(depth, t, PLE_DIM)
    for i in range(depth):
        proj, gates = _in_proj(xf, vec(norm_g), w_in_cur, i, 0)
        ya = _moba_branch(proj, bias_a, bsz, seq)
        yb = _swa_branch(proj, bias_b, _swa_sink_rows(sinks[i]), bsz, seq)
        yc, cast = _rglru_branch(proj, conv_w, vec(conv_b), w_r_bf, vec(b_r), w_i_bf, vec(b_i),
                                 vec(lam), i, bsz, seq, cast=later if i == 0 else ())
        if i == 0:
            w_br_bf, w_out_bf, w_pg_bf, w_pp_bf = cast
        merged = _merge(ya, yb, yc, gates, w_br_bf, i)
        xf, w_in_cur = _out_ple(merged, xf, p3d, w_out_bf, vec(ple_norm_g), w_pg_bf, w_pp_bf,
                                final_norm_g, i, final=(i == depth - 1),
                                w_next=w_in if i + 1 < depth else None)
    return xf.reshape(bsz, seq, d)
```

```python
import functools
import math

import jax
import jax.numpy as jnp
import numpy as np
from jax import lax
from jax.experimental import pallas as pl
from jax.experimental.pallas import tpu as pltpu

F32 = jnp.float32
BF16 = jnp.bfloat16

D_MODEL = 2048
PLE_DIM = 256
BRANCH_WIDTH = 1024
A_HEADS = 8
A_HEAD_DIM = 128
MOBA_BLOCK = 256
MOBA_TOPK = 3
B_Q_HEADS = 16
B_KV_HEADS = 4
B_GROUP = B_Q_HEADS // B_KV_HEADS
B_HEAD_DIM = 64
WINDOW = 128
C_BLOCKS = 8
C_BLOCK_DIM = 128
CONV_WIDTH = 4
LRU_C = 8.0
RPE_BUCKETS = 32
RPE_MAX_DIST = 128
EPS = 1e-6
NEG = -1e30
LOG2E = math.log2(math.e)

SUBLANES = 8
BF16_SUBLANES = 16

COL_QA = 0
COL_KA = COL_QA + BRANCH_WIDTH
COL_VA = COL_KA + BRANCH_WIDTH
COL_GA = COL_VA + BRANCH_WIDTH
COL_QB = COL_GA + BRANCH_WIDTH
COL_KB = COL_QB + BRANCH_WIDTH
COL_VB = COL_KB + B_KV_HEADS * B_HEAD_DIM
COL_GB = COL_VB + B_KV_HEADS * B_HEAD_DIM
COL_XC = COL_GB + BRANCH_WIDTH
COL_GC = COL_XC + BRANCH_WIDTH
COL_MG = COL_GC + BRANCH_WIDTH
IN_WIDTH = COL_MG + 3 * D_MODEL

PROJ_TM = 2048
PROJ_TN = 512
MERGE_TM = 2048
MERGE_TN = 512
MERGE_ROWS = 128
PLE_TM = 512
LRU_CHUNK = 256

VMEM_LIMIT = 62 * 1024 * 1024


def _params(sem):
    return pltpu.CompilerParams(dimension_semantics=sem, vmem_limit_bytes=VMEM_LIMIT)


def _rms(xf, g):
    return xf * lax.rsqrt(jnp.mean(xf * xf, axis=-1, keepdims=True) + EPS) * g


def _in_proj_kernel(x_ref, g_ref, w_ref, o_ref, mg_ref, h_sc):
    @pl.when(pl.program_id(1) == 0)
    def _():
        h_sc[...] = _rms(x_ref[...], g_ref[...]).astype(BF16)

    acc = jnp.dot(h_sc[...], w_ref[...], preferred_element_type=F32)
    o_ref[...] = acc
    mg_ref[...] = acc.astype(mg_ref.dtype)


def _in_proj(x2d, g, w_bf, layer, w_layer):
    t, d = x2d.shape
    n = w_bf.shape[2]
    assert COL_MG % PROJ_TN == 0
    main_tiles = COL_MG // PROJ_TN
    return pl.pallas_call(
        _in_proj_kernel,
        grid=(t // PROJ_TM, n // PROJ_TN),
        in_specs=[
            pl.BlockSpec((PROJ_TM, d), lambda i, j: (i, 0)),
            pl.BlockSpec((None, 1, d), lambda i, j: (layer, 0, 0)),
            pl.BlockSpec((None, d, PROJ_TN), lambda i, j: (w_layer, 0, j)),
        ],
        out_specs=[
            pl.BlockSpec((PROJ_TM, PROJ_TN), lambda i, j: (i, jnp.minimum(j, main_tiles))),
            pl.BlockSpec((PROJ_TM, PROJ_TN), lambda i, j: (i, jnp.maximum(j - main_tiles, 0))),
        ],
        out_shape=[jax.ShapeDtypeStruct((t, COL_MG + PROJ_TN), F32),
                   jax.ShapeDtypeStruct((t, n - COL_MG), BF16)],
        scratch_shapes=[pltpu.VMEM((PROJ_TM, d), BF16)],
        compiler_params=_params(("parallel", "arbitrary")),
        name="in_proj",
    )(x2d, g, w_bf)


MASKED_BUCKET = RPE_BUCKETS


def _t5_bucket_np(dist):
    n = np.maximum(dist, 0)
    max_exact = RPE_BUCKETS // 2
    ratio = np.log(np.maximum(n, 1).astype(np.float32) / np.float32(max_exact)) / np.float32(
        math.log(RPE_MAX_DIST / max_exact))
    large = max_exact + (ratio * np.float32(RPE_BUCKETS - max_exact)).astype(np.int32)
    large = np.minimum(large, RPE_BUCKETS - 1)
    return np.where(n < max_exact, n, large).astype(np.int32)


def _moba_bucket_planes():
    L = MOBA_BLOCK
    ki = np.arange(L)[:, None]
    qi = np.arange(L)[None, :]
    d_own = qi - ki
    d_far = 2 * L + qi - ki
    assert d_far.min() >= RPE_MAX_DIST
    own = np.where(d_own >= 0, _t5_bucket_np(d_own), MASKED_BUCKET)
    return np.stack([own, _t5_bucket_np(L + qi - ki), _t5_bucket_np(d_far)]).astype(np.int32)


def _swa_bucket_planes():
    W = WINDOW
    kk = np.arange(2 * W)[:, None]
    qi = np.arange(W)[None, :]
    dist = qi - kk + W
    ok = (dist >= 0) & (dist < W)
    b = _t5_bucket_np(dist)
    return np.stack([np.where(ok, b, MASKED_BUCKET),
                     np.where(ok & (kk >= W), b, MASKED_BUCKET)]).astype(np.int32)


def _bias_lookup_kernel(tab_ref, idx_ref, o_ref):
    h = pl.program_id(0)
    for p in range(idx_ref.shape[0]):
        idx = idx_ref[p]
        acc = jnp.full(idx.shape, NEG, F32)
        for b in range(RPE_BUCKETS):
            acc = jnp.where(idx == b, tab_ref[h * RPE_BUCKETS + b], acc)
        o_ref[0, p] = acc


def _bias_lookup(table_hb, planes, heads_per_row):
    nh = table_hb.shape[0]
    npl, r, c = planes.shape
    return pl.pallas_call(
        _bias_lookup_kernel,
        grid=(nh,),
        in_specs=[
            pl.BlockSpec(memory_space=pltpu.SMEM),
            pl.BlockSpec((npl, r, c), lambda h: (0, 0, 0)),
        ],
        out_specs=pl.BlockSpec((1, npl, r, c),
                               lambda h: (h // heads_per_row, 0, 0, h % heads_per_row)),
        out_shape=jax.ShapeDtypeStruct((nh // heads_per_row, npl, r, c * heads_per_row), F32),
        compiler_params=_params(("parallel",)),
        name="bias_lookup",
    )(table_hb.reshape(-1), jnp.asarray(planes))


GATE_ROWS = BF16_SUBLANES


_sigmoid = jax.nn.sigmoid


def _silu(x):
    return x * _sigmoid(x)


MOBA_GROUP = 2
MOBA_UNROLL = 2
MOBA_SLOTS = 2
MOBA_HEADS = 2


def _moba_prep_kernel(q_ref, k_ref, v_ref, kb_ref, vt_ref, qt_ref, mask_ref, kmean_sc, *, nb):
    L = MOBA_BLOCK
    dh = A_HEAD_DIM
    G = MOBA_GROUP

    kmean_sc[...] = jnp.zeros_like(kmean_sc)
    for n in range(nb):
        rows = slice((n % G) * L, (n % G + 1) * L)
        kn = k_ref[n * L:(n + 1) * L, :]
        kb_ref[0, n // G, rows, :] = kn.astype(BF16)
        kmean_sc[n:n + 1, :] = jnp.sum(kn, axis=0, keepdims=True) * (1.0 / L)
        vt_ref[0, n // G, :, rows] = v_ref[n * L:(n + 1) * L, :].T.astype(BF16)

    blk = lax.broadcasted_iota(jnp.int32, (GATE_ROWS, L), 0)
    blk_f = blk.astype(F32)
    for j in range(nb):
        qt = q_ref[j * L:(j + 1) * L, :].T
        gate = jnp.dot(kmean_sc[...], qt, precision=lax.Precision.HIGHEST,
                       preferred_element_type=F32)
        gm = jnp.where(blk < j, gate, NEG)
        picked = jnp.zeros((GATE_ROWS, L), F32)
        for _ in range(MOBA_TOPK):
            mx = jnp.max(gm, axis=0, keepdims=True)
            first = jnp.min(jnp.where(gm == mx, blk_f, float(GATE_ROWS)), axis=0, keepdims=True)
            hit = blk_f == first
            picked = jnp.where(hit, 1.0, picked)
            gm = jnp.where(hit, -jnp.inf, gm)
        allowed = ((picked > 0.0) & (blk < j)) | (blk == j)
        qt_ref[0, j] = (qt * (dh ** -0.5 * LOG2E)).astype(BF16)
        mask_ref[0, j] = jnp.where(allowed, 0.0, NEG).astype(BF16)


def _moba_prep(proj, bsz, seq):
    L = MOBA_BLOCK
    dh = A_HEAD_DIM
    G = MOBA_GROUP
    nb = seq // L
    nh = bsz * A_HEADS
    cq, ck, cv = (c // dh for c in (COL_QA, COL_KA, COL_VA))
    col = lambda c: pl.BlockSpec((seq, dh), lambda b, h: (b, c + h))
    shapes = [(nh, nb // G, G * L, dh), (nh, nb // G, dh, G * L), (nh, nb, dh, L),
              (nh, nb, GATE_ROWS, L)]
    return pl.pallas_call(
        functools.partial(_moba_prep_kernel, nb=nb),
        grid=(bsz, A_HEADS),
        in_specs=[col(cq), col(ck), col(cv)],
        out_specs=[pl.BlockSpec((1,) + s[1:], lambda b, h: (b * A_HEADS + h, 0, 0, 0))
                   for s in shapes],
        out_shape=[jax.ShapeDtypeStruct(s, BF16) for s in shapes],
        scratch_shapes=[pltpu.VMEM((GATE_ROWS, dh), F32)],
        compiler_params=_params(("parallel", "parallel")),
        name="moba_prep",
    )(proj, proj, proj)


def _moba_schedule(nb):
    G = MOBA_GROUP
    return [(j, i) for j in range(nb) for i in range((j + G) // G)]


def _moba_kernel(jt_ref, it_ref, kb_ref, vt_ref, qt_ref, mask_ref, g_ref, bias_ref, o_ref,
                 kaug_sc, qaug_sc, s_sc, p_sc, alpha_sc, lfin_sc, lout_sc, m_sc, l_sc, acc_sc, *,
                 sched):
    L = MOBA_BLOCK
    dh = A_HEAD_DIM
    G = MOBA_GROUP
    U = MOBA_UNROLL
    NS = MOBA_SLOTS
    nt = len(sched)
    heads = range(MOBA_HEADS)
    hcol = lambda u: slice(u * dh, (u + 1) * dh)
    last_group = lambda j: j // G

    @pl.when((pl.program_id(0) == 0) & (pl.program_id(1) == 0))
    def _():
        lanes = lax.broadcasted_iota(jnp.int32, (L, dh), 1)
        for u in heads:
            for n in range(kaug_sc.shape[1] * G):
                rows = slice((n % G) * L, (n % G + 1) * L)
                kaug_sc[u, n // G, rows, dh:] = jnp.where(lanes == n, 1.0, 0.0).astype(BF16)
            qaug_sc[u, :, dh + GATE_ROWS:, :] = jnp.zeros(
                (qaug_sc.shape[1], dh - GATE_ROWS, L), BF16)

    for u in heads:
        kaug_sc[u, :, :, :dh] = kb_ref[u]
        qaug_sc[u, :, :dh, :] = qt_ref[u]
        qaug_sc[u, :, dh:dh + GATE_ROWS, :] = mask_ref[u]

    def scores(u, t, slot):
        s_sc[u, slot] = jnp.dot(kaug_sc[u, it_ref[t]], qaug_sc[u, jt_ref[t]],
                                preferred_element_type=F32)

    def softmax(u, t, slot):
        j = jt_ref[t]
        i = it_ref[t]
        s = s_sc[u, slot]
        parts = []
        for g in range(G):
            n = i * G + g
            plane = jnp.where(n == j, 0, jnp.where(n == j - 1, 1, 2))
            parts.append(s[g * L:(g + 1) * L, :] + bias_ref[u, plane])
        m_old = jnp.where(i == 0, NEG, m_sc[u])
        m_new = m_old
        for sp in parts:
            m_new = jnp.maximum(m_new, jnp.max(sp, axis=0, keepdims=True))
        alpha = jnp.exp2(m_old - m_new)
        probs = [jnp.exp2(sp - m_new) for sp in parts]
        l_new = alpha * l_sc[u]
        for pr in probs:
            l_new = l_new + jnp.sum(pr, axis=0, keepdims=True)
        p_sc[u, slot] = jnp.concatenate(probs, axis=0).astype(BF16)
        alpha_sc[u, slot] = alpha
        lfin_sc[u, slot] = l_new
        l_sc[u] = l_new
        m_sc[u] = m_new

    def weighted_values(u, t, slot):
        j = jt_ref[t]
        acc_sc[u, j] = alpha_sc[u, slot] * acc_sc[u, j] + jnp.dot(
            vt_ref[u, it_ref[t]], p_sc[u, slot], preferred_element_type=F32)
        lout_sc[u, j] = lfin_sc[u, slot]

    def finish(u, j):
        out_t = acc_sc[u, j] / lout_sc[u, j]
        rows = slice(j * L, (j + 1) * L)
        o_ref[rows, hcol(u)] = (out_t.T * _silu(g_ref[rows, hcol(u)])).astype(o_ref.dtype)

    s_sc[:, NS - 1] = jnp.full(s_sc.shape[:1] + s_sc.shape[2:], -jnp.inf, F32)
    p_sc[:, NS - 2] = jnp.zeros(p_sc.shape[:1] + p_sc.shape[2:], BF16)
    alpha_sc[...] = jnp.ones_like(alpha_sc)
    lfin_sc[...] = jnp.ones_like(lfin_sc)
    m_sc[...] = jnp.full_like(m_sc, NEG)
    l_sc[...] = jnp.zeros_like(l_sc)
    acc_sc[...] = jnp.zeros_like(acc_sc)

    def unrolled_steps(k, carry):
        base = U * k
        for r in range(U):
            t = base + r
            t_pv = jnp.maximum(t - 2, 0)
            t_sm = jnp.maximum(t - 1, 0)
            for u in heads:
                weighted_values(u, t_pv, (r - 2) % NS)
                softmax(u, t_sm, (r - 1) % NS)
                scores(u, t, r % NS)
        return carry

    assert U % NS == 0
    n_loop = nt // U
    lax.fori_loop(0, n_loop, unrolled_steps, 0)
    for t in range(n_loop * U, nt + 2):
        r = t % U
        for u in heads:
            if 0 <= t - 2:
                weighted_values(u, t - 2, (r - 2) % NS)
            if 0 <= t - 1 < nt:
                softmax(u, t - 1, (r - 1) % NS)
            if t < nt:
                scores(u, t, r % NS)
    for u in heads:
        for j in range(qaug_sc.shape[1]):
            finish(u, j)


def _moba_branch(proj, bias, bsz, seq):
    L = MOBA_BLOCK
    dh = A_HEAD_DIM
    G = MOBA_GROUP
    nb = seq // L
    hp = MOBA_HEADS
    assert min(MOBA_TOPK, nb - 1) == MOBA_TOPK and nb <= GATE_ROWS and nb % G == 0
    assert A_HEADS % hp == 0
    kb, vt, qt, mask = _moba_prep(proj, bsz, seq)
    sched = _moba_schedule(nb)
    jt = jnp.asarray(np.array([j for j, _ in sched], np.int32))
    it = jnp.asarray(np.array([i for _, i in sched], np.int32))
    pairs = A_HEADS // hp
    cg = COL_GA // (hp * dh)
    smem = pl.BlockSpec(memory_space=pltpu.SMEM)
    per_pair = lambda a: pl.BlockSpec((hp,) + a.shape[1:], lambda b, h: (b * pairs + h, 0, 0, 0))
    return pl.pallas_call(
        functools.partial(_moba_kernel, sched=sched),
        grid=(bsz, pairs),
        in_specs=[smem, smem, per_pair(kb), per_pair(vt), per_pair(qt), per_pair(mask),
                  pl.BlockSpec((seq, hp * dh), lambda b, h: (b, cg + h)),
                  pl.BlockSpec((hp, 3, L, L), lambda b, h: (h, 0, 0, 0))],
        out_specs=pl.BlockSpec((seq, hp * dh), lambda b, h: (b, h)),
        out_shape=jax.ShapeDtypeStruct((bsz * seq, BRANCH_WIDTH), BF16),
        scratch_shapes=[
            pltpu.VMEM((hp, nb // G, G * L, 2 * dh), BF16),
            pltpu.VMEM((hp, nb, 2 * dh, L), BF16),
            pltpu.VMEM((hp, MOBA_SLOTS, G * L, L), F32),
            pltpu.VMEM((hp, MOBA_SLOTS, G * L, L), BF16),
            pltpu.VMEM((hp, MOBA_SLOTS, 1, L), F32),
            pltpu.VMEM((hp, MOBA_SLOTS, 1, L), F32),
            pltpu.VMEM((hp, nb, 1, L), F32),
            pltpu.VMEM((hp, 1, L), F32),
            pltpu.VMEM((hp, 1, L), F32),
            pltpu.VMEM((hp, nb, dh, L), F32),
        ],
        compiler_params=_params(("arbitrary", "arbitrary")),
        name="moba",
    )(jt, it, kb, vt, qt, mask, proj, bias)


SWA_BLOCKS = 4


def _swa_kernel(q_ref, kp_ref, kc_ref, vp_ref, vc_ref, g0_ref, g1_ref, bias_ref, sink_ref, o_ref,
                ot_sc, x_sc, k_sc, vt_sc, s_sc, p_sc, d_sc):
    W = WINDOW
    dh = B_HEAD_DIM
    kvw = B_KV_HEADS * dh
    head = lambda g, h: slice((g * B_GROUP + h) * dh, (g * B_GROUP + h + 1) * dh)
    lane_group = lax.broadcasted_iota(jnp.int32, (2 * W, kvw), 1) // dh
    first = pl.program_id(1) == 0
    for u in range(SWA_BLOCKS):
        rows = slice(u * W, (u + 1) * W)
        if u == 0:
            k_prev, v_prev = kp_ref[...], vp_ref[...]
            plane = jnp.where(first, 1, 0)
        else:
            k_prev, v_prev = kc_ref[(u - 1) * W:u * W, :], vc_ref[(u - 1) * W:u * W, :]
            plane = 0
        qt = q_ref[rows, :].T * (dh ** -0.5 * LOG2E)
        x_sc[u] = jnp.concatenate(
            [jnp.concatenate([qt[head(g, h), :] for g in range(B_KV_HEADS)], axis=0)
             for h in range(B_GROUP)], axis=1).astype(BF16)
        kband = jnp.concatenate([k_prev, kc_ref[rows, :]], axis=0)
        vt_sc[u] = jnp.concatenate([v_prev, vc_ref[rows, :]], axis=0).T.astype(BF16)
        for g in range(B_KV_HEADS):
            k_sc[u, g] = jnp.where(lane_group == g, kband, 0.0).astype(BF16)
        for g in range(B_KV_HEADS):
            s_sc[u, g] = jnp.dot(k_sc[u, g], x_sc[u], preferred_element_type=F32)
        for g in range(B_KV_HEADS):
            s = s_sc[u, g] + bias_ref[g, plane]
            sink = sink_ref[g]
            m = jnp.maximum(jnp.max(s, axis=0, keepdims=True), sink)
            p = jnp.exp2(s - m)
            d_sc[u, g] = jnp.sum(p, axis=0, keepdims=True) + jnp.exp2(sink - m)
            p_sc[u, g] = p.astype(BF16)
        for g in range(B_KV_HEADS):
            o = jnp.dot(vt_sc[u, g * dh:(g + 1) * dh, :], p_sc[u, g],
                        preferred_element_type=F32) / d_sc[u, g]
            for h in range(B_GROUP):
                ot_sc[u, head(g, h), :] = o[:, h * W:(h + 1) * W]
        gate = jnp.concatenate([g0_ref[rows, :], g1_ref[rows, :]], axis=1)
        o_ref[rows, :] = (ot_sc[u].T * _silu(gate)).astype(o_ref.dtype)


def _swa_branch(proj, bias, sink, bsz, seq):
    W = WINDOW
    R = SWA_BLOCKS * W
    nb = seq // W
    ns = seq // R
    kvw = B_KV_HEADS * B_HEAD_DIM
    cq = COL_QB // BRANCH_WIDTH
    ck = COL_KB // kvw
    cv = COL_VB // kvw
    cg = COL_GB // (BRANCH_WIDTH // 2)
    assert COL_QB % BRANCH_WIDTH == 0 and COL_KB % kvw == 0 and COL_VB % kvw == 0
    assert COL_GB % (BRANCH_WIDTH // 2) == 0 and seq % R == 0
    prev = lambda b, n: (b * nb + jnp.maximum(SWA_BLOCKS * n - 1, 0))
    cur = lambda c, w: pl.BlockSpec((R, w), lambda b, n: (b * ns + n, c))
    return pl.pallas_call(
        _swa_kernel,
        grid=(bsz, ns),
        in_specs=[
            cur(cq, BRANCH_WIDTH),
            pl.BlockSpec((W, kvw), lambda b, n: (prev(b, n), ck)),
            cur(ck, kvw),
            pl.BlockSpec((W, kvw), lambda b, n: (prev(b, n), cv)),
            cur(cv, kvw),
            cur(cg, BRANCH_WIDTH // 2),
            cur(cg + 1, BRANCH_WIDTH // 2),
            pl.BlockSpec(bias.shape, lambda b, n: (0, 0, 0, 0)),
            pl.BlockSpec(sink.shape, lambda b, n: (0, 0, 0)),
        ],
        out_specs=cur(0, BRANCH_WIDTH),
        out_shape=jax.ShapeDtypeStruct((bsz * seq, BRANCH_WIDTH), BF16),
        scratch_shapes=[
            pltpu.VMEM((SWA_BLOCKS, BRANCH_WIDTH, W), F32),
            pltpu.VMEM((SWA_BLOCKS, kvw, B_GROUP * W), BF16),
            pltpu.VMEM((SWA_BLOCKS, B_KV_HEADS, 2 * W, kvw), BF16),
            pltpu.VMEM((SWA_BLOCKS, kvw, 2 * W), BF16),
            pltpu.VMEM((SWA_BLOCKS, B_KV_HEADS, 2 * W, B_GROUP * W), F32),
            pltpu.VMEM((SWA_BLOCKS, B_KV_HEADS, 2 * W, B_GROUP * W), BF16),
            pltpu.VMEM((SWA_BLOCKS, B_KV_HEADS, 1, B_GROUP * W), F32),
        ],
        compiler_params=_params(("parallel", "arbitrary")),
        name="swa",
    )(proj, proj, proj, proj, proj, proj, proj, bias, sink)


def _swa_sink_rows(sinks):
    s = (sinks.astype(F32) * LOG2E).reshape(B_KV_HEADS, 1, B_GROUP, 1)
    return jnp.broadcast_to(s, (B_KV_HEADS, 1, B_GROUP, WINDOW)).reshape(
        B_KV_HEADS, 1, B_GROUP * WINDOW)


def _scan_steps(a, b, index, length, axis):
    k = 1
    while k < length:
        keep = index >= k
        a_prev = jnp.where(keep, pltpu.roll(a, k, axis), 1.0)
        b_prev = jnp.where(keep, pltpu.roll(b, k, axis), 0.0)
        b = a * b_prev + b
        a = a * a_prev
        k *= 2
    return a, b


LRU_BLOCKS = 2


def _rglru_kernel(xc_ref, gc_ref, cw_ref, cb_ref, wr_ref, br_ref, wi_ref, bi_ref, lam_ref, *rest,
                  seq, n_cast):
    cast_src, (o_ref, *cast_dst) = rest[:n_cast], rest[n_cast:2 * n_cast + 1]
    xpad_sc, h_sc, a_sc, b_sc, hin_sc = rest[2 * n_cast + 1:]
    for src, dst in zip(cast_src, cast_dst):
        dst[...] = src[...].astype(dst.dtype)
    T = LRU_CHUNK
    C = C_BLOCK_DIM
    PAD = SUBLANES
    blocks = range(LRU_BLOCKS)
    ch = lambda u: slice(u * C, (u + 1) * C)
    for u in blocks:
        xpad_sc[u, :PAD, :] = jnp.zeros((PAD, C), F32)
        xpad_sc[u, PAD:, :] = xc_ref[:, ch(u)]
    h_sc[...] = jnp.zeros_like(h_sc)
    nlam = -lam_ref[...]
    softplus = jnp.maximum(nlam, 0.0) + jnp.log1p(jnp.exp(-jnp.abs(nlam)))
    decay = -LRU_C * softplus
    tiles = T // SUBLANES
    sub = lax.broadcasted_iota(jnp.int32, (tiles, SUBLANES, C), 1)
    tile_row = lax.broadcasted_iota(jnp.int32, (tiles, C), 0)

    def chunk(c, carry):
        t0 = pl.multiple_of(c * T, T)
        for u in blocks:
            conv = cb_ref[:, ch(u)]
            for w in range(CONV_WIDTH):
                off = PAD - (CONV_WIDTH - 1) + w
                conv = conv + cw_ref[w:w + 1, ch(u)] * xpad_sc[u, pl.ds(t0 + off, T), :]
            cbf = conv.astype(BF16)
            r = _sigmoid(jnp.dot(cbf, wr_ref[u], preferred_element_type=F32) + br_ref[:, ch(u)])
            i = _sigmoid(jnp.dot(cbf, wi_ref[u], preferred_element_type=F32) + bi_ref[:, ch(u)])
            log_a = r * decay[:, ch(u)]
            a = jnp.exp(log_a)
            u2 = -jnp.tanh(log_a) * (a * a + 1.0)
            b = jnp.where(u2 > 0.0, u2 * lax.rsqrt(u2), 0.0) * (i * conv)
            a, b = _scan_steps(a.reshape(tiles, SUBLANES, C), b.reshape(tiles, SUBLANES, C),
                               sub, SUBLANES, 1)
            a_sc[u] = a.reshape(T, C)
            b_sc[u] = b.reshape(T, C)
            last = pl.ds(SUBLANES - 1, tiles, stride=SUBLANES)
            a_tile, b_tile = _scan_steps(a_sc[u, last, :], b_sc[u, last, :], tile_row, tiles, 0)
            h_out = a_tile * h_sc[u] + b_tile
            hin_sc[u] = jnp.where(tile_row >= 1, pltpu.roll(h_out, 1, 0), h_sc[u])
            h_sc[u] = h_out[tiles - 1:tiles, :]
            h_in = jnp.concatenate(
                [jnp.broadcast_to(hin_sc[u, t:t + 1, :], (SUBLANES, C)) for t in range(tiles)],
                axis=0)
            h = a_sc[u] * h_in + b_sc[u]
            o_ref[pl.ds(t0, T), ch(u)] = (h * _silu(gc_ref[pl.ds(t0, T), ch(u)])).astype(
                o_ref.dtype)
        return carry

    lax.fori_loop(0, seq // T, chunk, 0)


def _rglru_branch(proj, conv_w, conv_b, w_r_bf, b_r, w_i_bf, b_i, lam, layer, bsz, seq, cast=()):
    nbk = LRU_BLOCKS
    cd = C_BLOCK_DIM
    wd = nbk * cd
    assert C_BLOCKS % nbk == 0 and COL_XC % wd == 0 and COL_GC % wd == 0
    cx = COL_XC // wd
    cg = COL_GC // wd
    vspec = pl.BlockSpec((None, 1, wd), lambda b, c: (layer, 0, c))
    wspec = pl.BlockSpec((None, nbk, cd, cd), lambda b, c: (layer, c, 0, 0))
    ncb = C_BLOCKS // nbk
    steps = bsz * ncb
    flat = [w.reshape(-1, w.shape[-1]) for w in cast]
    assert all(f.shape[0] % (steps * BF16_SUBLANES) == 0 for f in flat)
    slab = lambda f: pl.BlockSpec((f.shape[0] // steps, f.shape[1]), lambda b, c: (b * ncb + c, 0))
    out = pl.pallas_call(
        functools.partial(_rglru_kernel, seq=seq, n_cast=len(cast)),
        grid=(bsz, ncb),
        in_specs=[
            pl.BlockSpec((seq, wd), lambda b, c: (b, cx + c)),
            pl.BlockSpec((seq, wd), lambda b, c: (b, cg + c)),
            pl.BlockSpec((None, CONV_WIDTH, wd), lambda b, c: (layer, 0, c)),
            vspec, wspec, vspec, wspec, vspec, vspec,
        ] + [slab(f) for f in flat],
        out_specs=[pl.BlockSpec((seq, wd), lambda b, c: (b, c))] + [slab(f) for f in flat],
        out_shape=[jax.ShapeDtypeStruct((bsz * seq, BRANCH_WIDTH), BF16)]
        + [jax.ShapeDtypeStruct(f.shape, BF16) for f in flat],
        scratch_shapes=[pltpu.VMEM((nbk, seq + SUBLANES, cd), F32), pltpu.VMEM((nbk, 1, cd), F32),
                        pltpu.VMEM((nbk, LRU_CHUNK, cd), F32),
                        pltpu.VMEM((nbk, LRU_CHUNK, cd), F32),
                        pltpu.VMEM((nbk, LRU_CHUNK // SUBLANES, cd), F32)],
        compiler_params=_params(("parallel", "parallel")),
        name="rglru",
    )(proj, proj, conv_w, conv_b, w_r_bf, b_r, w_i_bf, b_i, lam, *flat)
    return out[0], [o.reshape(w.shape) for o, w in zip(out[1:], cast)]


def _merge_kernel(ya_ref, yb_ref, yc_ref, ma_ref, mb_ref, mc_ref, wbr_ref, o_ref):
    for c in range(o_ref.shape[0] // MERGE_ROWS):
        rows = slice(c * MERGE_ROWS, (c + 1) * MERGE_ROWS)
        merged = None
        for n, (y_ref, m_ref) in enumerate(((ya_ref, ma_ref), (yb_ref, mb_ref), (yc_ref, mc_ref))):
            term = _sigmoid(m_ref[rows, :].astype(F32)) * jnp.dot(
                y_ref[rows, :], wbr_ref[n], preferred_element_type=F32)
            merged = term if merged is None else merged + term
        o_ref[rows, :] = merged.astype(o_ref.dtype)


def _merge(ya, yb, yc, gates, wbr_bf, layer):
    t = ya.shape[0]
    d = wbr_bf.shape[3]
    tm, tn = MERGE_TM, MERGE_TN
    assert d % tn == 0
    cm = 0
    per = d // tn
    yspec = pl.BlockSpec((tm, BRANCH_WIDTH), lambda i, c: (i, 0))
    mspec = lambda n: pl.BlockSpec((tm, tn), lambda i, c: (i, cm + n * per + c))
    return pl.pallas_call(
        _merge_kernel,
        grid=(t // tm, per),
        in_specs=[
            yspec, yspec, yspec, mspec(0), mspec(1), mspec(2),
            pl.BlockSpec((None, 3, BRANCH_WIDTH, tn), lambda i, c: (layer, 0, 0, c)),
        ],
        out_specs=pl.BlockSpec((tm, tn), lambda i, c: (i, c)),
        out_shape=jax.ShapeDtypeStruct((t, d), BF16),
        compiler_params=_params(("parallel", "arbitrary")),
        name="merge",
    )(ya, yb, yc, gates, gates, gates, wbr_bf)


def _out_ple_kernel(mg_ref, x_ref, p_ref, wout_ref, g_ref, wpg_ref, wpp_ref, fg_ref, *rest,
                    final, cast):
    if cast:
        wsrc_ref, o_ref, wdst_ref = rest
        wdst_ref[...] = wsrc_ref[...].astype(wdst_ref.dtype)
    else:
        (o_ref,) = rest
    x = x_ref[...] + jnp.dot(mg_ref[...], wout_ref[...], preferred_element_type=F32)
    h = _rms(x, g_ref[...]).astype(BF16)
    gate = _sigmoid(jnp.dot(h, wpg_ref[...], preferred_element_type=F32))
    emb = jnp.dot(p_ref[...].astype(BF16), wpp_ref[...], preferred_element_type=F32)
    y = x + gate * emb
    if final:
        y = _rms(y, fg_ref[...])
    o_ref[...] = y


def _out_ple(merged, x2d, p3d, wout_bf, g, wpg_bf, wpp_bf, final_g, layer, final, w_next=None):
    t, d = x2d.shape
    tm = PLE_TM
    steps = t // tm
    in_extra, out_extra, args_extra = [], [], []
    out_shape = jax.ShapeDtypeStruct((t, d), F32)
    if w_next is not None:
        _, wd, wn = w_next.shape
        assert wd % (steps * BF16_SUBLANES) == 0
        in_extra = [pl.BlockSpec((None, wd // steps, wn), lambda i: (layer + 1, i, 0))]
        out_extra = [pl.BlockSpec((wd // steps, wn), lambda i: (i, 0))]
        args_extra = [w_next]
        out_shape = [out_shape, jax.ShapeDtypeStruct((wd, wn), BF16)]
    wspec = lambda r: pl.BlockSpec((None, r, d), lambda i: (layer, 0, 0),
                                   pipeline_mode=pl.Buffered(1))
    row = lambda w: pl.BlockSpec((tm, w), lambda i: (i, 0))
    out = pl.pallas_call(
        functools.partial(_out_ple_kernel, final=final, cast=w_next is not None),
        grid=(steps,),
        in_specs=[row(d), row(d), pl.BlockSpec((None, tm, PLE_DIM), lambda i: (layer, i, 0)),
                  wspec(d), pl.BlockSpec((None, 1, d), lambda i: (layer, 0, 0)), wspec(d),
                  wspec(PLE_DIM), pl.BlockSpec((1, d), lambda i: (0, 0))] + in_extra,
        out_specs=[row(d)] + out_extra if out_extra else row(d),
        out_shape=out_shape,
        compiler_params=_params(("parallel",)),
        name="out_ple",
    )(merged, x2d, p3d, wout_bf, g, wpg_bf, wpp_bf, final_g.reshape(1, d), *args_extra)
    if w_next is None:
        return out, None
    return out[0], out[1].reshape(1, wd, wn)


def kernel(x, p, rpe_table, norm_g, w_in, sinks, conv_w, conv_b, w_r, b_r, w_i, b_i, lam, w_br,
           w_out, ple_norm_g, w_pg, w_pp, final_norm_g):
    bsz, seq, d = x.shape
    depth = w_in.shape[0]
    assert d == D_MODEL and w_in.shape[2] == IN_WIDTH
    assert seq % MOBA_BLOCK == 0 and seq % LRU_CHUNK == 0 and (bsz * seq) % PROJ_TM == 0
    t = bsz * seq
    xf = x.reshape(t, d)
    table = rpe_table.astype(F32).T * LOG2E
    bias_a = _bias_lookup(table[:A_HEADS], _moba_bucket_planes(), 1)
    bias_b = _bias_lookup(table[A_HEADS:], _swa_bucket_planes(), B_GROUP)
    bf = lambda w: w.astype(BF16)
    vec = lambda v: v.reshape(depth, 1, v.shape[-1])
    w_r_bf, w_i_bf = bf(w_r), bf(w_i)
    later = (w_br, w_out, w_pg, w_pp)
    w_in_cur = bf(w_in[:1])
    p3d = p.reshape(depth, t, PLE_DIM)
    for i in range(depth):
        proj, gates = _in_proj(xf, vec(norm_g), w_in_cur, i, 0)
        ya = _moba_branch(proj, bias_a, bsz, seq)
        yb = _swa_branch(proj, bias_b, _swa_sink_rows(sinks[i]), bsz, seq)
        yc, cast = _rglru_branch(proj, conv_w, vec(conv_b), w_r_bf, vec(b_r), w_i_bf, vec(b_i),
                                 vec(lam), i, bsz, seq, cast=later if i == 0 else ())
        if i == 0:
            w_br_bf, w_out_bf, w_pg_bf, w_pp_bf = cast
        merged = _merge(ya, yb, yc, gates, w_br_bf, i)
        xf, w_in_cur = _out_ple(merged, xf, p3d, w_out_bf, vec(ple_norm_g), w_pg_bf, w_pp_bf,
                                final_norm_g, i, final=(i == depth - 1),
                                w_next=w_in if i + 1 < depth else None)
    return xf.reshape(bsz, seq, d)
```
